```python
import math
import numpy as np
import jax
import jax.numpy as jnp
from jax import lax

D_MODEL = 2048
BATCH = 4
SEQ = 2048
DEPTH = 4
DEC_BATCH = 128
DEC_SEQ = 4
PAST_LEN = 16384
PAGE_SIZE = 128

N_BRANCH = 3
BRANCH_W = 3 * D_MODEL // 8
DK_A = 128
DV_A = 128
H_A = BRANCH_W // DV_A
CONV_W = 4
CHUNK_A = 64
DK_B = 64
DV_B = 128
H_B = BRANCH_W // DV_B
GLA_RANK = 16
GLA_NORMALIZER = 16.0
CHUNK_B = 16
N_C = 64
H_C = BRANCH_W // N_C
DECAY_LORA = 64
AAA_LORA = 64
GATE_LORA = 128
GN_EPS = 64e-5
D_FF = 5504
NORM_EPS = 1e-6

QKV_A = 2 * H_A * DK_A + H_A * DV_A
A_COLS = (QKV_A, H_A * DV_A, H_A, H_A)
B_COLS = (H_B * DK_B, H_B * DK_B, H_B * DV_B, H_B * DV_B, GLA_RANK)
C_COLS = (BRANCH_W, BRANCH_W, BRANCH_W, DECAY_LORA, AAA_LORA, GATE_LORA)
N_A = sum(A_COLS)
N_B = sum(B_COLS)
N_CC = sum(C_COLS)
N_GATE = N_BRANCH * D_MODEL
IN_COLS = (N_A, N_B, N_CC, N_GATE)
N_IN = sum(IN_COLS)

kernel_name = 'hybrid_gdn_gla_rwkv7_macaron_step'


def _split(x, sizes):
    return jnp.split(x, np.cumsum(sizes)[:-1].tolist(), axis=-1)


def _heads(x, h):
    return x.reshape(x.shape[:2] + (h, -1))


def _rmsnorm(x, g):
    xf = x.astype(jnp.float32)
    y = xf * lax.rsqrt(jnp.mean(xf * xf, axis=-1, keepdims=True) + NORM_EPS)
    return (y * g.astype(jnp.float32)).astype(x.dtype)


def _head_rmsnorm(o, g):
    return o * lax.rsqrt(jnp.mean(o * o, axis=-1, keepdims=True) + NORM_EPS) * g.astype(jnp.float32)


def _l2norm(x):
    return x * lax.rsqrt(jnp.sum(x * x, axis=-1, keepdims=True) + 1e-6)


def _swiglu(x, w_up, w_down):
    gate, up = jnp.split(x @ w_up, 2, axis=-1)
    return (jax.nn.silu(gate) * up) @ w_down


def _to_chunks(x, c):
    b, t = x.shape[:2]
    pad = (-t) % c
    x = jnp.pad(x, [(0, 0), (0, pad)] + [(0, 0)] * (x.ndim - 2))
    x = x.reshape((b, (t + pad) // c, c) + x.shape[2:])
    return jnp.moveaxis(x, 3, 1)


def _from_chunks(o, t):
    b, h, n, c, v = o.shape
    return jnp.moveaxis(o, 1, 3).reshape(b, n * c, h, v)[:, :t]


def _gated_delta_chunked(q, k, v, g, beta, s0):
    t = q.shape[1]
    c = min(CHUNK_A, t)
    q, k, v, g, beta = (_to_chunks(a, c) for a in (q, k, v, g, beta))
    gc = jnp.cumsum(g, axis=-1)
    causal = jnp.tril(jnp.ones((c, c), dtype=bool))
    strict = jnp.tril(jnp.ones((c, c), dtype=bool), -1)
    diff = gc[..., :, None] - gc[..., None, :]
    decay = jnp.where(causal, jnp.exp(jnp.where(causal, diff, 0.0)), 0.0)
    kb = k * beta[..., None]
    vb = v * beta[..., None]
    lower = jnp.where(strict, jnp.einsum('bhnik,bhnjk->bhnij', kb, k) * decay, 0.0)
    eye = jnp.eye(c, dtype=q.dtype)
    tmat = lax.linalg.triangular_solve(eye + lower, jnp.broadcast_to(eye, lower.shape),
                                       left_side=True, lower=True)
    u_base = tmat @ vb
    w_k = tmat @ (kb * jnp.exp(gc)[..., None])
    attn = jnp.einsum('bhnik,bhnjk->bhnij', q, k) * decay
    q_dec = q * jnp.exp(gc)[..., None]
    k_dec = k * jnp.exp(gc[..., -1:] - gc)[..., None]
    g_last = jnp.exp(gc[..., -1])

    def step(s, xs):
        u_b, wk, at, qd, kd, gl = xs
        u = u_b - jnp.einsum('bhck,bhkv->bhcv', wk, s)
        o = jnp.einsum('bhck,bhkv->bhcv', qd, s) + jnp.einsum('bhij,bhjv->bhiv', at, u)
        s = gl[..., None, None] * s + jnp.einsum('bhck,bhcv->bhkv', kd, u)
        return s, o

    xs = tuple(jnp.moveaxis(a, 2, 0) for a in (u_base, w_k, attn, q_dec, k_dec, g_last))
    s, o = lax.scan(step, s0, xs)
    return _from_chunks(jnp.moveaxis(o, 0, 2), t), s


def _gla_chunked(q, k, v, gk, s0):
    t = q.shape[1]
    c = min(CHUNK_B, t)
    q, k, v, gk = (_to_chunks(a, c) for a in (q, k, v, gk))
    bc = jnp.cumsum(gk, axis=-2)
    q_in = q * jnp.exp(bc)
    k_in = k * jnp.exp(-bc)
    causal = jnp.tril(jnp.ones((c, c), dtype=bool))
    attn = jnp.where(causal, jnp.einsum('bhnik,bhnjk->bhnij', q_in, k_in), 0.0)
    o_intra = attn @ v
    k_dec = k * jnp.exp(bc[..., -1:, :] - bc)
    a_last = jnp.exp(bc[..., -1, :])

    def step(s, xs):
        qi, kd, vv, al = xs
        o = jnp.einsum('bhck,bhkv->bhcv', qi, s)
        s = al[..., None] * s + jnp.einsum('bhck,bhcv->bhkv', kd, vv)
        return s, o

    xs = tuple(jnp.moveaxis(a, 2, 0) for a in (q_in, k_dec, v, a_last))
    s, o_inter = lax.scan(step, s0, xs)
    return _from_chunks(o_intra + jnp.moveaxis(o_inter, 0, 2), t), s


def _rwkv7_scan(r, w, k, v, a, b, s0):
    def step(s, xs):
        rt, wt, kt, vt, at, bt = xs
        sa = jnp.einsum('bhvk,bhk->bhv', s, at)
        s = s * wt[:, :, None, :] + sa[..., None] * bt[:, :, None, :] + vt[..., None] * kt[:, :, None, :]
        return s, jnp.einsum('bhvk,bhk->bhv', s, rt)

    xs = tuple(jnp.moveaxis(z, 1, 0) for z in (r, w, k, v, a, b))
    s, o = lax.scan(step, s0, xs)
    return jnp.moveaxis(o, 0, 1), s


def _mixer(xn, s_gdn, conv_buf, s_gla, s_rwkv, shift_buf, p):
    bsz, t, _ = xn.shape
    dt = xn.dtype
    f32 = jnp.float32
    pa, pb, pc, pg = _split(xn @ p['w_in'], IN_COLS)

    qkv, z_a, b_a, a_a = _split(pa, A_COLS)
    full = jnp.concatenate([conv_buf.astype(dt), qkv], axis=1)
    conv = sum(full[:, j:j + t] * p['conv_a'][j] for j in range(CONV_W))
    new_conv = full[:, t:]
    q_a, k_a, v_a = _split(jax.nn.silu(conv.astype(f32)), (H_A * DK_A, H_A * DK_A, H_A * DV_A))
    q_a = _l2norm(_heads(q_a, H_A)) * DK_A ** -0.5
    k_a = _l2norm(_heads(k_a, H_A))
    beta = jax.nn.sigmoid(b_a.astype(f32))
    g_a = -jnp.exp(p['a_log'].astype(f32)) * jax.nn.softplus(a_a.astype(f32) + p['dt_bias'].astype(f32))
    o_a, s_gdn = _gated_delta_chunked(q_a, k_a, _heads(v_a, H_A), g_a, beta, s_gdn.astype(f32))
    y_a = (_head_rmsnorm(o_a, p['gdn_norm']) * jax.nn.silu(_heads(z_a.astype(f32), H_A))).reshape(bsz, t, BRANCH_W)

    q_b, k_b, v_b, og_b, gk_low = _split(pb, B_COLS)
    gk = jax.nn.log_sigmoid((gk_low @ p['gla_gk_up'] + p['gla_gk_bias']).astype(f32)) / GLA_NORMALIZER
    o_b, s_gla = _gla_chunked(_heads(q_b.astype(f32), H_B) * DK_B ** -0.5, _heads(k_b.astype(f32), H_B),
                              _heads(v_b.astype(f32), H_B), _heads(gk, H_B), s_gla.astype(f32))
    y_b = (_head_rmsnorm(o_b, p['gla_norm']) * jax.nn.silu(_heads(og_b.astype(f32), H_B))).reshape(bsz, t, BRANCH_W)

    prev = jnp.concatenate([shift_buf.astype(dt), pc[:, :-1]], axis=1)
    new_shift = pc[:, t - 1:]
    xc = pc + (prev - pc) * p['rwkv_mu']
    r_c, k_c, v_c, w_low, a_low, g_low = _split(xc, C_COLS)
    w_log = -jax.nn.softplus(-(p['rwkv_w0'] + jnp.tanh(w_low) @ p['rwkv_w_up']).astype(f32)) - 0.5
    decay = jnp.exp(-jnp.exp(w_log))
    a_c = jax.nn.sigmoid((p['rwkv_a0'] + a_low @ p['rwkv_a_up']).astype(f32))
    g_c = (jax.nn.sigmoid(g_low) @ p['rwkv_g_up']).astype(f32)
    k_c = k_c.astype(f32)
    kk = _l2norm(_heads(k_c * p['rwkv_k_k'].astype(f32), H_C))
    k_c = _heads(k_c * (1.0 + (a_c - 1.0) * p['rwkv_k_a'].astype(f32)), H_C)
    r_c = _heads(r_c.astype(f32), H_C)
    v_c = _heads(v_c.astype(f32), H_C)
    o_c, s_rwkv = _rwkv7_scan(r_c, _heads(decay, H_C), k_c, v_c, -kk, kk * _heads(a_c, H_C), s_rwkv.astype(f32))
    mu = jnp.mean(o_c, axis=-1, keepdims=True)
    var = jnp.mean(jnp.square(o_c - mu), axis=-1, keepdims=True)
    o_c = ((o_c - mu) * lax.rsqrt(var + GN_EPS)).reshape(bsz, t, BRANCH_W) * p['rwkv_ln_w'].astype(f32) + p['rwkv_ln_b'].astype(f32)
    bonus = jnp.sum(r_c * k_c * p['rwkv_r_k'].astype(f32).reshape(H_C, N_C), axis=-1, keepdims=True) * v_c
    y_c = (o_c + bonus.reshape(bsz, t, BRANCH_W)) * g_c

    ys = jnp.stack([y_a, y_b, y_c], axis=2).astype(dt)
    gates = jax.nn.sigmoid(pg.astype(f32)).astype(dt).reshape(bsz, t, N_BRANCH, D_MODEL)
    merged = jnp.sum(gates * jnp.einsum('btnw,nwd->btnd', ys, p['w_branch']), axis=2)
    return merged @ p['w_out'], (s_gdn, new_conv, s_gla, s_rwkv, new_shift)


def _trunk(x, s_gdn, conv_buf, s_gla, s_rwkv, shift_buf, weights, norm_final):
    new = ([], [], [], [], [])
    for l in range(DEPTH):
        p = {name: arr[l] for name, arr in weights.items()}
        h = x + 0.5 * _swiglu(_rmsnorm(x, p['norm_ff1']), p['w_ff1_up'], p['w_ff1_down'])
        m, st = _mixer(_rmsnorm(h, p['norm_mix']), s_gdn[l], conv_buf[l], s_gla[l], s_rwkv[l], shift_buf[l], p)
        h = h + m
        x = h + 0.5 * _swiglu(_rmsnorm(h, p['norm_ff2']), p['w_ff2_up'], p['w_ff2_down'])
        for lst, s in zip(new, st):
            lst.append(s.astype(x.dtype))
    return (_rmsnorm(x, norm_final),) + tuple(jnp.stack(lst) for lst in new)


def setup_inputs(seed: int = 0) -> dict:
    key = jax.random.key(seed)
    ks = iter(jax.random.split(key, 48))
    f32 = jnp.float32
    L = DEPTH

    def nrm(shape, scale):
        return jax.random.normal(next(ks), shape, f32) * scale

    def gain(shape):
        return 1.0 + nrm(shape, 0.1)

    def unif(shape, lo, hi):
        return jax.random.uniform(next(ks), shape, f32, lo, hi)

    dt_init = jnp.exp(unif((L, H_A), math.log(1e-3), math.log(1e-1)))
    return {
        'x_prompt': nrm((BATCH, SEQ, D_MODEL), 1.0),
        'x_sample': nrm((DEC_BATCH, DEC_SEQ, D_MODEL), 1.0),
        'state_gdn': nrm((L, DEC_BATCH, H_A, DK_A, DV_A), 0.1),
        'state_gdn_conv': nrm((L, DEC_BATCH, CONV_W - 1, QKV_A), 1.0),
        'state_gla': nrm((L, DEC_BATCH, H_B, DK_B, DV_B), 1.0),
        'state_rwkv': nrm((L, DEC_BATCH, H_C, N_C, N_C), 0.5),
        'state_rwkv_shift': nrm((L, DEC_BATCH, 1, N_CC), 1.0),
        'w_in': nrm((L, D_MODEL, N_IN), D_MODEL ** -0.5),
        'conv_a': nrm((L, CONV_W, QKV_A), CONV_W ** -0.5),
        'a_log': jnp.log(unif((L, H_A), 1.0, 16.0)),
        'dt_bias': dt_init + jnp.log(-jnp.expm1(-dt_init)),
        'gdn_norm': gain((L, DV_A)),
        'gla_gk_up': nrm((L, GLA_RANK, H_B * DK_B), GLA_RANK ** -0.5),
        'gla_gk_bias': nrm((L, H_B * DK_B), 0.1),
        'gla_norm': gain((L, DV_B)),
        'rwkv_mu': unif((L, N_CC), 0.0, 1.0),
        'rwkv_w0': unif((L, BRANCH_W), -6.0, -1.0),
        'rwkv_w_up': nrm((L, DECAY_LORA, BRANCH_W), 0.5 * DECAY_LORA ** -0.5),
        'rwkv_a0': nrm((L, BRANCH_W), 0.1),
        'rwkv_a_up': nrm((L, AAA_LORA, BRANCH_W), AAA_LORA ** -0.5),
        'rwkv_g_up': nrm((L, GATE_LORA, BRANCH_W), GATE_LORA ** -0.5),
        'rwkv_k_k': 0.85 + nrm((L, BRANCH_W), 0.05),
        'rwkv_k_a': 1.0 + nrm((L, BRANCH_W), 0.05),
        'rwkv_r_k': nrm((L, BRANCH_W), 0.1),
        'rwkv_ln_w': gain((L, BRANCH_W)),
        'rwkv_ln_b': nrm((L, BRANCH_W), 0.01),
        'w_branch': nrm((L, N_BRANCH, BRANCH_W, D_MODEL), BRANCH_W ** -0.5),
        'w_out': nrm((L, D_MODEL, D_MODEL), D_MODEL ** -0.5),
        'norm_ff1': gain((L, D_MODEL)),
        'w_ff1_up': nrm((L, D_MODEL, 2 * D_FF), D_MODEL ** -0.5),
        'w_ff1_down': nrm((L, D_FF, D_MODEL), D_FF ** -0.5),
        'norm_mix': gain((L, D_MODEL)),
        'norm_ff2': gain((L, D_MODEL)),
        'w_ff2_up': nrm((L, D_MODEL, 2 * D_FF), D_MODEL ** -0.5),
        'w_ff2_down': nrm((L, D_FF, D_MODEL), D_FF ** -0.5),
        'norm_final': gain((D_MODEL,)),
    }


def reference(x_prompt, x_sample, state_gdn, state_gdn_conv, state_gla, state_rwkv, state_rwkv_shift,
              w_in, conv_a, a_log, dt_bias, gdn_norm, gla_gk_up, gla_gk_bias, gla_norm,
              rwkv_mu, rwkv_w0, rwkv_w_up, rwkv_a0, rwkv_a_up, rwkv_g_up, rwkv_k_k, rwkv_k_a, rwkv_r_k,
              rwkv_ln_w, rwkv_ln_b, w_branch, w_out, norm_ff1, w_ff1_up, w_ff1_down, norm_mix,
              norm_ff2, w_ff2_up, w_ff2_down, norm_final):
    weights = dict(w_in=w_in, conv_a=conv_a, a_log=a_log, dt_bias=dt_bias, gdn_norm=gdn_norm,
                   gla_gk_up=gla_gk_up, gla_gk_bias=gla_gk_bias, gla_norm=gla_norm,
                   rwkv_mu=rwkv_mu, rwkv_w0=rwkv_w0, rwkv_w_up=rwkv_w_up, rwkv_a0=rwkv_a0,
                   rwkv_a_up=rwkv_a_up, rwkv_g_up=rwkv_g_up, rwkv_k_k=rwkv_k_k, rwkv_k_a=rwkv_k_a,
                   rwkv_r_k=rwkv_r_k, rwkv_ln_w=rwkv_ln_w, rwkv_ln_b=rwkv_ln_b, w_branch=w_branch,
                   w_out=w_out, norm_ff1=norm_ff1, w_ff1_up=w_ff1_up, w_ff1_down=w_ff1_down,
                   norm_mix=norm_mix, norm_ff2=norm_ff2, w_ff2_up=w_ff2_up, w_ff2_down=w_ff2_down)

    def zeros_like_state(s):
        return jnp.zeros((DEPTH, BATCH) + s.shape[2:], x_prompt.dtype)

    y_prompt, gdn_p, conv_p, gla_p, rwkv_p, shift_p = _trunk(
        x_prompt, zeros_like_state(state_gdn), zeros_like_state(state_gdn_conv), zeros_like_state(state_gla),
        zeros_like_state(state_rwkv), zeros_like_state(state_rwkv_shift), weights, norm_final)
    y_sample, gdn_s, conv_s, gla_s, rwkv_s, shift_s = _trunk(
        x_sample, state_gdn, state_gdn_conv, state_gla, state_rwkv, state_rwkv_shift, weights, norm_final)
    return (y_prompt, y_sample, gdn_p, gdn_s, conv_p, conv_s, gla_p, gla_s, rwkv_p, rwkv_s, shift_p, shift_s)
```

```python
import functools
import math

import jax
import jax.numpy as jnp
from jax import lax
from jax.experimental import pallas as pl
from jax.experimental.pallas import tpu as pltpu

F32 = jnp.float32
BF16 = jnp.bfloat16
HIGHEST = lax.Precision.HIGHEST

D_MODEL = 2048
N_BRANCH = 3
BRANCH_W = 768
DK_A, DV_A, H_A, CONV_W = 128, 128, 6, 4
DK_B, DV_B, H_B, GLA_RANK, GLA_NORMALIZER = 64, 128, 6, 16, 16.0
N_C, H_C, DECAY_LORA, AAA_LORA, GATE_LORA = 64, 12, 64, 64, 128
GN_EPS = 64e-5
D_FF = 5504
NORM_EPS = 1e-6
QKV_A = 2 * H_A * DK_A + H_A * DV_A
N_A = QKV_A + H_A * DV_A + 2 * H_A
N_B = 2 * H_B * DK_B + 2 * H_B * DV_B + GLA_RANK
N_CC = 3 * BRANCH_W + DECAY_LORA + AAA_LORA + GATE_LORA
N_PAIR = H_C // 2

LANES = 128
SUBLANES = 8
CHUNK = 64
SAMPLE_ROWS = 8
FF_TILE = 512
D_FF_PAD = ((D_FF + FF_TILE - 1) // FF_TILE) * FF_TILE
VMEM_LIMIT = 60 * 1024 * 1024


def _cparams(n_axes):
    return pltpu.CompilerParams(dimension_semantics=("arbitrary",) * n_axes,
                                vmem_limit_bytes=VMEM_LIMIT)


def _pick_tile(n, prefs):
    for t in prefs:
        if n % t == 0:
            return t
    raise ValueError(f"no tile for {n}")


def _bdot(a, b):
    return jnp.dot(a.astype(BF16), b.astype(BF16), preferred_element_type=F32)


def _bdot_nt(a, b):
    return lax.dot_general(a.astype(BF16), b.astype(BF16), (((1,), (1,)), ((), ())),
                           preferred_element_type=F32)


def _bdot_tn(a, b):
    return lax.dot_general(a.astype(BF16), b.astype(BF16), (((0,), (0,)), ((), ())),
                           preferred_element_type=F32)


def _hdot(a, b):
    return jnp.dot(a, b, preferred_element_type=F32, precision=HIGHEST)


def _sigmoid(x):
    return jax.nn.sigmoid(x)


def _silu(x):
    return x * _sigmoid(x)


def _softplus(x):
    return jnp.maximum(x, 0.0) + jnp.log(1.0 + jnp.exp(-jnp.abs(x)))


def _rmsnorm_rows(x, g):
    return x * lax.rsqrt(jnp.mean(x * x, axis=-1, keepdims=True) + NORM_EPS) * g


def _chunk_masks(rows, seq_len):
    sh = int(math.log2(seq_len))
    ri = lax.broadcasted_iota(jnp.int32, (rows, rows), 0)
    ci = lax.broadcasted_iota(jnp.int32, (rows, rows), 1)
    same = (ri >> sh) == (ci >> sh)
    return same & (ri >= ci), same & (ri > ci), same


def _inv_unit_lower(lm, eye, nil):
    x = eye - lm
    p = lm
    k = 2
    while k < nil:
        p = _hdot(p, p)
        x = x + _hdot(x, p)
        k *= 2
    return x


def _row_valid(rows, seq_len, null_rows, width):
    r = lax.broadcasted_iota(jnp.int32, (rows, width), 0)
    return (r & (seq_len - 1)) >= null_rows


def _seq_rows(rows, seq_len, s, width):
    r = lax.broadcasted_iota(jnp.int32, (rows, width), 0)
    return (r >> int(math.log2(seq_len))) == s


def _ffn_body(x_ref, g_ref, wg_ref, wu_ref, wd_ref, o_ref, xn_ref):
    j = pl.program_id(1)

    @pl.when(j == 0)
    def _():
        xn_ref[...] = _rmsnorm_rows(x_ref[...], g_ref[...]).astype(BF16)
        o_ref[...] = jnp.zeros_like(o_ref)

    xn = xn_ref[...]
    gate = jnp.dot(xn, wg_ref[...], preferred_element_type=F32)
    up = jnp.dot(xn, wu_ref[...], preferred_element_type=F32)
    act = (_silu(gate) * up).astype(BF16)
    o_ref[...] += jnp.dot(act, wd_ref[...], preferred_element_type=F32)

    @pl.when(j == pl.num_programs(1) - 1)
    def _():
        o_ref[...] = x_ref[...] + 0.5 * o_ref[...]


def _ffn(x, g, wg, wu, wd):
    n, d = x.shape
    tm = _pick_tile(n, (1088, 544, 512, 256, 128, 64, 32, 16, 8))
    nf = wg.shape[1] // FF_TILE
    return pl.pallas_call(
        _ffn_body,
        grid=(n // tm, nf),
        in_specs=[
            pl.BlockSpec((tm, d), lambda i, j: (i, 0)),
            pl.BlockSpec((1, d), lambda i, j: (0, 0)),
            pl.BlockSpec((d, FF_TILE), lambda i, j: (0, j)),
            pl.BlockSpec((d, FF_TILE), lambda i, j: (0, j)),
            pl.BlockSpec((FF_TILE, d), lambda i, j: (j, 0)),
        ],
        out_specs=pl.BlockSpec((tm, d), lambda i, j: (i, 0)),
        out_shape=jax.ShapeDtypeStruct((n, d), F32),
        scratch_shapes=[pltpu.VMEM((tm, d), BF16)],
        compiler_params=_cparams(2),
        name="ffn",
    )(x, g.reshape(1, d), wg, wu, wd)


def _proj_body(x_ref, g_ref, w_ref, o_ref):
    xn = _rmsnorm_rows(x_ref[...], g_ref[...]).astype(BF16)
    o_ref[...] = jnp.dot(xn, w_ref[...], preferred_element_type=F32)


def _proj(x, g, w, tn):
    n, d = x.shape
    nn = w.shape[1]
    tm = _pick_tile(n, (544, 512, 256, 128, 64, 32, 16, 8))
    return pl.pallas_call(
        _proj_body,
        grid=(nn // tn, n // tm),
        in_specs=[
            pl.BlockSpec((tm, d), lambda j, i: (i, 0)),
            pl.BlockSpec((1, d), lambda j, i: (0, 0)),
            pl.BlockSpec((d, tn), lambda j, i: (0, j)),
        ],
        out_specs=pl.BlockSpec((tm, tn), lambda j, i: (i, j)),
        out_shape=jax.ShapeDtypeStruct((n, nn), F32),
        compiler_params=_cparams(2),
        name="proj",
    )(x, g.reshape(1, d), w)


MERGE_TN = 512


def _merge_body(ya_ref, yb_ref, yc_ref, ga_ref, gb_ref, gc_ref, wb_ref, wo_ref, h_ref, o_ref):
    j = pl.program_id(1)

    @pl.when(j == 0)
    def _():
        o_ref[...] = h_ref[...]

    m = _sigmoid(ga_ref[...]) * jnp.dot(ya_ref[...], wb_ref[0], preferred_element_type=F32)
    m += _sigmoid(gb_ref[...]) * jnp.dot(yb_ref[...], wb_ref[1], preferred_element_type=F32)
    m += _sigmoid(gc_ref[...]) * jnp.dot(yc_ref[...], wb_ref[2], preferred_element_type=F32)
    o_ref[...] += jnp.dot(m.astype(BF16), wo_ref[...], preferred_element_type=F32)


def _merge(ya, yb, yc, pg, wb, wo, h):
    n, d = h.shape
    tm = _pick_tile(n, (544, 512, 256, 128, 64, 32, 16))
    nj = d // MERGE_TN
    yspec = pl.BlockSpec((tm, BRANCH_W), lambda i, j: (i, 0))
    return pl.pallas_call(
        _merge_body,
        grid=(n // tm, nj),
        in_specs=[
            yspec, yspec, yspec,
            pl.BlockSpec((tm, MERGE_TN), lambda i, j: (i, j)),
            pl.BlockSpec((tm, MERGE_TN), lambda i, j: (i, nj + j)),
            pl.BlockSpec((tm, MERGE_TN), lambda i, j: (i, 2 * nj + j)),
            pl.BlockSpec((N_BRANCH, BRANCH_W, MERGE_TN), lambda i, j: (0, 0, j)),
            pl.BlockSpec((MERGE_TN, d), lambda i, j: (j, 0)),
            pl.BlockSpec((tm, d), lambda i, j: (i, 0)),
        ],
        out_specs=pl.BlockSpec((tm, d), lambda i, j: (i, 0)),
        out_shape=jax.ShapeDtypeStruct((n, d), F32),
        compiler_params=_cparams(2),
        name="merge",
    )(ya, yb, yc, pg, pg, pg, wb, wo, h)


def _final_norm_body(x_ref, g_ref, o_ref):
    o_ref[...] = _rmsnorm_rows(x_ref[...], g_ref[...])


def _final_norm(x, g):
    n, d = x.shape
    tm = _pick_tile(n, (544, 512, 256, 128, 64, 32, 16, 8))
    return pl.pallas_call(
        _final_norm_body,
        grid=(n // tm,),
        in_specs=[pl.BlockSpec((tm, d), lambda i: (i, 0)), pl.BlockSpec((1, d), lambda i: (0, 0))],
        out_specs=pl.BlockSpec((tm, d), lambda i: (i, 0)),
        out_shape=jax.ShapeDtypeStruct((n, d), F32),
        compiler_params=_cparams(1),
        name="final_norm",
    )(x, g.reshape(1, d))


def _gdn_body(*refs, nseq, null_rows, has_state):
    if has_state:
        (qkv_ref, z_ref, ba_ref, cs_ref, cw_ref, prm_ref, nw_ref, s0_ref,
         y_ref, so_ref, xbuf) = refs
    else:
        qkv_ref, z_ref, ba_ref, cw_ref, prm_ref, nw_ref, y_ref, so_ref, xbuf = refs
        s0_ref = so_ref
    rows = qkv_ref.shape[0]
    seq_len = rows // nseq
    c = pl.program_id(1)

    @pl.when(c == 0)
    def _():
        xbuf[0:SUBLANES, :] = jnp.zeros((SUBLANES, QKV_A), F32)
        if not has_state:
            so_ref[...] = jnp.zeros_like(so_ref)

    @pl.when(c > 0)
    def _():
        xbuf[0:SUBLANES, :] = xbuf[rows:rows + SUBLANES, :]

    xbuf[SUBLANES:SUBLANES + rows, :] = qkv_ref[...]
    if has_state:
        for s in range(nseq):
            r0 = SUBLANES + s * seq_len + null_rows - (CONV_W - 1)
            xbuf[r0:r0 + CONV_W - 1, :] = cs_ref[s]

    incl, strict, same = _chunk_masks(rows, seq_len)
    eye = (lax.broadcasted_iota(jnp.int32, (rows, rows), 0)
           == lax.broadcasted_iota(jnp.int32, (rows, rows), 1)).astype(F32)
    valid = _row_valid(rows, seq_len, null_rows, LANES) if null_rows else None

    ba = ba_ref[...]
    beta_all = _sigmoid(ba)
    g_all = -jnp.exp(prm_ref[0:1, :]) * _softplus(ba + prm_ref[1:2, :])
    if null_rows:
        g_all = jnp.where(valid, g_all, 0.0)
    lstack = jnp.concatenate([incl.astype(F32), same.astype(F32)], axis=0)
    cs = _hdot(lstack, g_all)
    gc_all, gt_all = cs[0:rows], cs[rows:2 * rows]
    gc_t = gc_all.T

    def conv_cols(c0):
        acc = xbuf[SUBLANES - 3:SUBLANES - 3 + rows, c0:c0 + LANES] * cw_ref[0:1, c0:c0 + LANES]
        for j in range(1, CONV_W):
            acc += (xbuf[SUBLANES - 3 + j:SUBLANES - 3 + j + rows, c0:c0 + LANES]
                    * cw_ref[j:j + 1, c0:c0 + LANES])
        return _silu(acc)

    for h in range(H_A):
        q = conv_cols(h * DK_A)
        k = conv_cols(H_A * DK_A + h * DK_A)
        v = conv_cols(2 * H_A * DK_A + h * DV_A)
        q = q * lax.rsqrt(jnp.sum(q * q, axis=-1, keepdims=True) + 1e-6) * DK_A ** -0.5
        k = k * lax.rsqrt(jnp.sum(k * k, axis=-1, keepdims=True) + 1e-6)
        if null_rows:
            q = jnp.where(valid, q, 0.0)
            k = jnp.where(valid, k, 0.0)
            v = jnp.where(valid, v, 0.0)
        beta = beta_all[:, h:h + 1]
        gcc = gc_all[:, H_A + h:H_A + h + 1]
        gtc = gt_all[:, H_A + h:H_A + h + 1]
        gcr = gc_t[H_A + h:H_A + h + 1, :]
        decay = jnp.where(incl, jnp.exp(jnp.where(incl, gcc - gcr, 0.0)), 0.0)
        kb = k * beta
        vb = v * beta
        lm = jnp.where(strict, _bdot_nt(kb, k) * decay, 0.0)
        tmat = _inv_unit_lower(lm, eye, seq_len)
        egc = jnp.exp(gcc)
        u_base = _bdot(tmat, vb)
        w_k = _bdot(tmat, kb * egc)
        attn = _bdot_nt(q, k) * decay
        q_dec = q * egc
        k_dec = k * jnp.exp(gtc - gcc)
        u = u_base
        o = None
        new_states = []
        for s in range(nseq):
            st = s0_ref[s, h]
            if nseq > 1:
                sel = _seq_rows(rows, seq_len, s, LANES)
                wk_s = jnp.where(sel, w_k, 0.0)
                qd_s = jnp.where(sel, q_dec, 0.0)
                kd_s = jnp.where(sel, k_dec, 0.0)
            else:
                wk_s, qd_s, kd_s = w_k, q_dec, k_dec
            both = _bdot(jnp.concatenate([wk_s, qd_s], axis=0), st)
            u = u - both[0:rows]
            o = both[rows:2 * rows] if o is None else o + both[rows:2 * rows]
            new_states.append((st, kd_s, jnp.exp(gtc[s * seq_len:s * seq_len + 1, :])))
        o = o + _bdot(attn, u)
        for s, (st, kd_s, g_last) in enumerate(new_states):
            so_ref[s, h] = g_last * st + _bdot_tn(kd_s, u)
        z = z_ref[:, h * DV_A:(h + 1) * DV_A]
        y_ref[:, h * DV_A:(h + 1) * DV_A] = (_rmsnorm_rows(o, nw_ref[...]) * _silu(z)).astype(y_ref.dtype)


def _gdn(p1, p3, conv_state, state, conv_w, prm, norm_w, *, nbatch, nchunk, nseq, null_rows):
    has_state = state is not None
    rows = CHUNK
    in_specs = [
        pl.BlockSpec((rows, QKV_A), lambda b, c: (b * nchunk + c, 0)),
        pl.BlockSpec((rows, BRANCH_W), lambda b, c: (b * nchunk + c, QKV_A // BRANCH_W)),
        pl.BlockSpec((rows, LANES), lambda b, c: (b * nchunk + c, N_CC // LANES)),
    ]
    args = [p1, p1, p3]
    if has_state:
        in_specs.append(pl.BlockSpec((nseq, CONV_W - 1, QKV_A), lambda b, c: (b, 0, 0)))
        args.append(conv_state)
    in_specs += [
        pl.BlockSpec((CONV_W, QKV_A), lambda b, c: (0, 0)),
        pl.BlockSpec((SUBLANES, LANES), lambda b, c: (0, 0)),
        pl.BlockSpec((1, DV_A), lambda b, c: (0, 0)),
    ]
    args += [conv_w, prm, norm_w]
    if has_state:
        in_specs.append(pl.BlockSpec((nseq, H_A, DK_A, DV_A), lambda b, c: (b, 0, 0, 0)))
        args.append(state)
    nstate = nbatch * nseq
    return pl.pallas_call(
        functools.partial(_gdn_body, nseq=nseq, null_rows=null_rows, has_state=has_state),
        grid=(nbatch, nchunk),
        in_specs=in_specs,
        out_specs=[
            pl.BlockSpec((rows, BRANCH_W), lambda b, c: (b * nchunk + c, 0)),
            pl.BlockSpec((nseq, H_A, DK_A, DV_A), lambda b, c: (b, 0, 0, 0)),
        ],
        out_shape=[
            jax.ShapeDtypeStruct((nbatch * nchunk * rows, BRANCH_W), BF16),
            jax.ShapeDtypeStruct((nstate, H_A, DK_A, DV_A), F32),
        ],
        scratch_shapes=[pltpu.VMEM((rows + SUBLANES, QKV_A), F32)],
        compiler_params=_cparams(2),
        name="gdn",
    )(*args)


def _gla_body(*refs, nseq, null_rows, has_state):
    if has_state:
        (q_ref, k_ref, v_ref, g_ref, gkl_ref, up_ref, bias_ref, nw_ref, s0_ref, y_ref, so_ref) = refs
    else:
        q_ref, k_ref, v_ref, g_ref, gkl_ref, up_ref, bias_ref, nw_ref, y_ref, so_ref = refs
        s0_ref = so_ref
    rows = q_ref.shape[0]
    seq_len = rows // nseq
    c = pl.program_id(1)
    if not has_state:
        @pl.when(c == 0)
        def _():
            so_ref[...] = jnp.zeros_like(so_ref)

    incl, _, same = _chunk_masks(rows, seq_len)
    ci = lax.broadcasted_iota(jnp.int32, (rows, rows), 1)
    first_half = same & ((ci & (seq_len - 1)) < seq_len // 2)
    wide = H_B * LANES
    x = _bdot(gkl_ref[...], up_ref[...]) + bias_ref[...]
    gk_all = (jnp.minimum(x, 0.0) - jnp.log(1.0 + jnp.exp(-jnp.abs(x)))) / GLA_NORMALIZER
    if null_rows:
        gk_all = jnp.where(_row_valid(rows, seq_len, null_rows, wide), gk_all, 0.0)
    lstack = jnp.concatenate([incl.astype(F32), same.astype(F32), first_half.astype(F32)], axis=0)
    cs = _hdot(lstack, gk_all)
    bc_all, bt_all, an_all = cs[0:rows], cs[rows:2 * rows], cs[2 * rows:3 * rows]
    bt_t = bt_all.T
    valid_k = _row_valid(rows, seq_len, null_rows, DK_B) if null_rows else None
    valid_v = _row_valid(rows, seq_len, null_rows, DV_B) if null_rows else None

    for h in range(H_B):
        c0 = h * LANES
        q = q_ref[:, c0:c0 + DK_B] * DK_B ** -0.5
        k = k_ref[:, c0:c0 + DK_B]
        v = v_ref[:, h * DV_B:(h + 1) * DV_B]
        if null_rows:
            q = jnp.where(valid_k, q, 0.0)
            k = jnp.where(valid_k, k, 0.0)
            v = jnp.where(valid_v, v, 0.0)
        bc = bc_all[:, c0:c0 + DK_B]
        bt = bt_all[:, c0:c0 + DK_B]
        an = an_all[:, c0:c0 + DK_B]
        attn = jnp.where(incl, _bdot_nt(q * jnp.exp(bc - an), k * jnp.exp(an - bc)), 0.0)
        o = _bdot(attn, v)
        q_in = q * jnp.exp(bc)
        k_dec = k * jnp.exp(bt - bc)
        for s in range(nseq):
            st = s0_ref[s, h]
            if nseq > 1:
                qi_s = jnp.where(_seq_rows(rows, seq_len, s, DK_B), q_in, 0.0)
                kd_s = jnp.where(_seq_rows(rows, seq_len, s, DK_B), k_dec, 0.0)
            else:
                qi_s, kd_s = q_in, k_dec
            o = o + _bdot(qi_s, st)
            a_last = jnp.exp(bt_t[c0:c0 + DK_B, s * seq_len:s * seq_len + 1])
            so_ref[s, h] = a_last * st + _bdot_tn(kd_s, v)
        g = g_ref[:, h * DV_B:(h + 1) * DV_B]
        y_ref[:, h * DV_B:(h + 1) * DV_B] = (_rmsnorm_rows(o, nw_ref[...]) * _silu(g)).astype(y_ref.dtype)


def _gla(p2, p3, state, up_w, bias, norm_w, *, nbatch, nchunk, nseq, null_rows):
    has_state = state is not None
    rows = CHUNK
    wide = H_B * LANES

    def col(j):
        return pl.BlockSpec((rows, wide), lambda b, c: (b * nchunk + c, j))

    in_specs = [col(0), col(1), col(2), col(3),
                pl.BlockSpec((rows, LANES), lambda b, c: (b * nchunk + c, N_CC // LANES + 1)),
                pl.BlockSpec((LANES, wide), lambda b, c: (0, 0)),
                pl.BlockSpec((1, wide), lambda b, c: (0, 0)),
                pl.BlockSpec((1, DV_B), lambda b, c: (0, 0))]
    args = [p2, p2, p2, p2, p3, up_w, bias, norm_w]
    if has_state:
        in_specs.append(pl.BlockSpec((nseq, H_B, DK_B, DV_B), lambda b, c: (b, 0, 0, 0)))
        args.append(state)
    return pl.pallas_call(
        functools.partial(_gla_body, nseq=nseq, null_rows=null_rows, has_state=has_state),
        grid=(nbatch, nchunk),
        in_specs=in_specs,
        out_specs=[
            pl.BlockSpec((rows, BRANCH_W), lambda b, c: (b * nchunk + c, 0)),
            pl.BlockSpec((nseq, H_B, DK_B, DV_B), lambda b, c: (b, 0, 0, 0)),
        ],
        out_shape=[
            jax.ShapeDtypeStruct((nbatch * nchunk * rows, BRANCH_W), BF16),
            jax.ShapeDtypeStruct((nbatch * nseq, H_B, DK_B, DV_B), F32),
        ],
        compiler_params=_cparams(2),
        name="gla",
    )(*args)


def _rwkv_body(*refs, nseq, null_rows, has_state):
    if has_state:
        (pc_ref, sh_ref, mu_ref, wwa_ref, gup_ref, vec_ref, s0_ref, y_ref, so_ref, pbuf, sp_ref) = refs
    else:
        pc_ref, mu_ref, wwa_ref, gup_ref, vec_ref, y_ref, so_ref, pbuf, sp_ref = refs
    rows = pc_ref.shape[0]
    seq_len = rows // nseq
    c = pl.program_id(1)
    nlast = pl.num_programs(1) - 1

    @pl.when(c == 0)
    def _():
        pbuf[0:SUBLANES, :] = jnp.zeros((SUBLANES, N_CC), F32)
        if not has_state:
            sp_ref[...] = jnp.zeros_like(sp_ref)

    @pl.when(c > 0)
    def _():
        pbuf[0:SUBLANES, :] = pbuf[rows:rows + SUBLANES, :]

    pbuf[SUBLANES:SUBLANES + rows, :] = pc_ref[...]
    if has_state:
        for s in range(nseq):
            r0 = SUBLANES + s * seq_len + null_rows - 1
            pbuf[r0:r0 + 1, :] = sh_ref[s]

    def xc_cols(c0, w):
        cur = pbuf[SUBLANES:SUBLANES + rows, c0:c0 + w]
        prev = pbuf[SUBLANES - 1:SUBLANES - 1 + rows, c0:c0 + w]
        return cur + (prev - cur) * mu_ref[:, c0:c0 + w]

    incl, strict, same = _chunk_masks(rows, seq_len)
    eye = (lax.broadcasted_iota(jnp.int32, (rows, rows), 0)
           == lax.broadcasted_iota(jnp.int32, (rows, rows), 1)).astype(F32)
    lane = lax.broadcasted_iota(jnp.int32, (rows, LANES), 1)
    lo = lane < N_C
    valid = _row_valid(rows, seq_len, null_rows, LANES) if null_rows else None

    lora = xc_cols(3 * BRANCH_W, DECAY_LORA + AAA_LORA + GATE_LORA)
    wa_in = lora[:, 0:LANES]
    wa_in = jnp.where(lo, jnp.tanh(wa_in), wa_in)
    wa = _bdot(wa_in, wwa_ref[...])
    g_all = _bdot(_sigmoid(lora[:, LANES:2 * LANES]), gup_ref[...])
    w0 = vec_ref[0:1, :]
    a0 = vec_ref[1:2, :]
    w_log = -_softplus(-(w0 + wa[:, 0:BRANCH_W])) - 0.5
    lw_all = -jnp.exp(w_log)
    a_all = _sigmoid(a0 + wa[:, BRANCH_W:2 * BRANCH_W])
    if null_rows:
        lw_all = jnp.where(_row_valid(rows, seq_len, null_rows, BRANCH_W), lw_all, 0.0)
    lstack = jnp.concatenate([incl.astype(F32), same.astype(F32)], axis=0)
    cs = _hdot(lstack, lw_all)
    cum_all, tot_all = cs[0:rows], cs[rows:2 * rows]

    r2 = lax.broadcasted_iota(jnp.int32, (2 * N_C, LANES), 0)
    c2 = lax.broadcasted_iota(jnp.int32, (2 * N_C, LANES), 1)
    blockdiag = (r2 >= N_C) == (c2 >= N_C)

    def seg_sum(xv):
        s_lo = jnp.sum(jnp.where(lo, xv, 0.0), axis=-1, keepdims=True)
        s_hi = jnp.sum(jnp.where(lo, 0.0, xv), axis=-1, keepdims=True)
        return jnp.where(lo, s_lo, s_hi)

    for p in range(N_PAIR):
        c0 = p * LANES
        r = xc_cols(c0, LANES)
        k = xc_cols(BRANCH_W + c0, LANES)
        v = xc_cols(2 * BRANCH_W + c0, LANES)
        a_p = a_all[:, c0:c0 + LANES]
        kkr = k * vec_ref[2:3, c0:c0 + LANES]
        kk = kkr * lax.rsqrt(seg_sum(kkr * kkr) + 1e-6)
        k2 = k * (1.0 + (a_p - 1.0) * vec_ref[3:4, c0:c0 + LANES])
        av = -kk
        bv = kk * a_p
        if null_rows:
            r = jnp.where(valid, r, 0.0)
            k2 = jnp.where(valid, k2, 0.0)
            v = jnp.where(valid, v, 0.0)
            av = jnp.where(valid, av, 0.0)
            bv = jnp.where(valid, bv, 0.0)
        cum = cum_all[:, c0:c0 + LANES]
        lw = lw_all[:, c0:c0 + LANES]
        g_inv = jnp.exp(-cum)
        a_t = av * jnp.exp(cum - lw)
        b_t = bv * g_inv
        k_t = k2 * g_inv
        r_t = r * jnp.exp(cum)

        mats = []
        for hh in range(2):
            msk = lo if hh == 0 else jnp.logical_not(lo)
            lhs = jnp.concatenate([jnp.where(msk, a_t, 0.0), jnp.where(msk, r_t, 0.0)], axis=0)
            mb = _bdot_nt(lhs, b_t)
            mk = _bdot_nt(lhs, k_t)
            a_ab = jnp.where(strict, mb[0:rows], 0.0)
            a_rb = jnp.where(incl, mb[rows:2 * rows], 0.0)
            a_ak = jnp.where(strict, mk[0:rows], 0.0)
            a_rk = jnp.where(incl, mk[rows:2 * rows], 0.0)
            tmat = _inv_unit_lower(-a_ab, eye, seq_len)
            mats.append((tmat, a_rb, a_ak, a_rk))

        states = []
        xs = None
        for s in range(nseq):
            if has_state:
                sv = s0_ref[s, p]
                sp = jnp.where(blockdiag, jnp.concatenate([sv, sv], axis=1), 0.0)
            else:
                sp = sp_ref[p]
            if nseq > 1:
                sel = _seq_rows(rows, seq_len, s, LANES)
                lhs = jnp.concatenate([jnp.where(sel, a_t, 0.0), jnp.where(sel, r_t, 0.0)], axis=0)
            else:
                lhs = jnp.concatenate([a_t, r_t], axis=0)
            both = _bdot_nt(lhs, sp)
            xs = both if xs is None else xs + both
            states.append(sp)
        x_in, rs_in = xs[0:rows], xs[rows:2 * rows]
        yv = x_in + jnp.where(lo, _bdot(mats[0][2], v), _bdot(mats[1][2], v))
        u = jnp.where(lo, _bdot(mats[0][0], yv), _bdot(mats[1][0], yv))
        o = rs_in + jnp.where(lo, _bdot(mats[0][1], u) + _bdot(mats[0][3], v),
                              _bdot(mats[1][1], u) + _bdot(mats[1][3], v))
        uv = jnp.concatenate([u, v], axis=0)
        bk = jnp.concatenate([b_t, k_t], axis=0)
        tot = tot_all[:, c0:c0 + LANES]
        for s, sp in enumerate(states):
            if nseq > 1:
                sel2 = _seq_rows(2 * rows, 2 * rows, 0, LANES)
                rr = lax.broadcasted_iota(jnp.int32, (2 * rows, LANES), 0) & (rows - 1)
                sel2 = (rr >> int(math.log2(seq_len))) == s
                uv_s = jnp.where(sel2, uv, 0.0)
            else:
                uv_s = uv
            upd = jnp.where(blockdiag, _bdot_tn(uv_s, bk), 0.0)
            sp_new = (sp + upd) * jnp.exp(tot[s * seq_len:s * seq_len + 1, :])
            if has_state:
                so_ref[s, p, 0:N_C, :] = sp_new[0:N_C, 0:N_C]
                so_ref[s, p, N_C:2 * N_C, :] = pltpu.roll(sp_new[N_C:2 * N_C, :], N_C, 1)[:, 0:N_C]
            else:
                sp_ref[p] = sp_new

                @pl.when(c == nlast)
                def _():
                    so_ref[0, p, 0:N_C, :] = sp_new[0:N_C, 0:N_C]
                    so_ref[0, p, N_C:2 * N_C, :] = pltpu.roll(sp_new[N_C:2 * N_C, :], N_C, 1)[:, 0:N_C]

        mean = seg_sum(o) * (1.0 / N_C)
        d = o - mean
        var = seg_sum(d * d) * (1.0 / N_C)
        on = d * lax.rsqrt(var + GN_EPS) * vec_ref[5:6, c0:c0 + LANES] + vec_ref[6:7, c0:c0 + LANES]
        bonus = seg_sum(r * k2 * vec_ref[4:5, c0:c0 + LANES]) * v
        y_ref[:, c0:c0 + LANES] = ((on + bonus) * g_all[:, c0:c0 + LANES]).astype(y_ref.dtype)


def _rwkv(p3, shift_state, state, mu, wwa, gup, vec, *, nbatch, nchunk, nseq, null_rows):
    has_state = state is not None
    rows = CHUNK
    in_specs = [pl.BlockSpec((rows, N_CC), lambda b, c: (b * nchunk + c, 0))]
    args = [p3]
    if has_state:
        in_specs.append(pl.BlockSpec((nseq, 1, N_CC), lambda b, c: (b, 0, 0)))
        args.append(shift_state)
    in_specs += [
        pl.BlockSpec((1, N_CC), lambda b, c: (0, 0)),
        pl.BlockSpec((LANES, 2 * BRANCH_W), lambda b, c: (0, 0)),
        pl.BlockSpec((GATE_LORA, BRANCH_W), lambda b, c: (0, 0)),
        pl.BlockSpec((SUBLANES, BRANCH_W), lambda b, c: (0, 0)),
    ]
    args += [mu, wwa, gup, vec]
    if has_state:
        in_specs.append(pl.BlockSpec((nseq, N_PAIR, 2 * N_C, N_C), lambda b, c: (b, 0, 0, 0)))
        args.append(state)
    return pl.pallas_call(
        functools.partial(_rwkv_body, nseq=nseq, null_rows=null_rows, has_state=has_state),
        grid=(nbatch, nchunk),
        in_specs=in_specs,
        out_specs=[
            pl.BlockSpec((rows, BRANCH_W), lambda b, c: (b * nchunk + c, 0)),
            pl.BlockSpec((nseq, N_PAIR, 2 * N_C, N_C), lambda b, c: (b, 0, 0, 0)),
        ],
        out_shape=[
            jax.ShapeDtypeStruct((nbatch * nchunk * rows, BRANCH_W), BF16),
            jax.ShapeDtypeStruct((nbatch * nseq, N_PAIR, 2 * N_C, N_C), F32),
        ],
        scratch_shapes=[pltpu.VMEM((rows + SUBLANES, N_CC), F32),
                        pltpu.VMEM((N_PAIR, 2 * N_C, 2 * N_C), F32)],
        compiler_params=_cparams(2),
        name="rwkv",
    )(*args)


def _pad_cols(w, n):
    return jnp.pad(w, ((0, 0), (0, n - w.shape[1])))


def _prep_in(w_in):
    d = w_in.shape[0]
    wa = w_in[:, :N_A]
    wb = w_in[:, N_A:N_A + N_B]
    wc = w_in[:, N_A + N_B:N_A + N_B + N_CC]
    wg = w_in[:, N_A + N_B + N_CC:]
    w1 = wa[:, :QKV_A + H_A * DV_A]

    def pad_heads(w):
        return jnp.pad(w.reshape(d, H_B, DK_B), ((0, 0), (0, 0), (0, LANES - DK_B))).reshape(d, H_B * LANES)

    kq = H_B * DK_B
    w2 = jnp.concatenate([pad_heads(wb[:, :kq]), pad_heads(wb[:, kq:2 * kq]),
                          wb[:, 2 * kq:2 * kq + 2 * H_B * DV_B]], axis=1)
    misc = jnp.concatenate([_pad_cols(wa[:, QKV_A + H_A * DV_A:], LANES),
                            _pad_cols(wb[:, 2 * kq + 2 * H_B * DV_B:], LANES)], axis=1)
    w3 = jnp.concatenate([wc, misc], axis=1)
    return w1.astype(BF16), w2.astype(BF16), w3.astype(BF16), wg.astype(BF16)


def _prep_ffn(w_up, w_down):
    wg = _pad_cols(w_up[:, :D_FF], D_FF_PAD).astype(BF16)
    wu = _pad_cols(w_up[:, D_FF:], D_FF_PAD).astype(BF16)
    wd = jnp.pad(w_down, ((0, D_FF_PAD - D_FF), (0, 0))).astype(BF16)
    return wg, wu, wd


def _pad_sample_rows(p, nb, ts):
    n = p.shape[-1]
    return jnp.pad(p.reshape(nb, ts, n), ((0, 0), (SAMPLE_ROWS - ts, 0), (0, 0))).reshape(nb * SAMPLE_ROWS, n)


def kernel(x_prompt, x_sample, state_gdn, state_gdn_conv, state_gla, state_rwkv, state_rwkv_shift, w_in, conv_a, a_log, dt_bias, gdn_norm, gla_gk_up, gla_gk_bias, gla_norm, rwkv_mu, rwkv_w0, rwkv_w_up, rwkv_a0, rwkv_a_up, rwkv_g_up, rwkv_k_k, rwkv_k_a, rwkv_r_k, rwkv_ln_w, rwkv_ln_b, w_branch, w_out, norm_ff1, w_ff1_up, w_ff1_down, norm_mix, norm_ff2, w_ff2_up, w_ff2_down, norm_final):
    bp, tp, d = x_prompt.shape
    bs, ts, _ = x_sample.shape
    depth = w_in.shape[0]
    n_p, n_s = bp * tp, bs * ts
    null_rows = SAMPLE_ROWS - ts
    assert tp % CHUNK == 0 and CONV_W - 1 <= null_rows < SAMPLE_ROWS
    seq_per_chunk = CHUNK // SAMPLE_ROWS
    assert bs % seq_per_chunk == 0
    nchunk_p = tp // CHUNK
    nstep_s = bs // seq_per_chunk

    x = jnp.concatenate([x_prompt.reshape(n_p, d), x_sample.reshape(n_s, d)], axis=0)
    outs = {k: [] for k in ("gdn_p", "gdn_s", "conv_p", "conv_s", "gla_p", "gla_s",
                            "rwkv_p", "rwkv_s", "shift_p", "shift_s")}
    for l in range(depth):
        h = _ffn(x, norm_ff1[l], *_prep_ffn(w_ff1_up[l], w_ff1_down[l]))

        w1, w2, w3, wg = _prep_in(w_in[l])
        p1 = _proj(h, norm_mix[l], w1, w1.shape[1] // 2)
        p2 = _proj(h, norm_mix[l], w2, w2.shape[1] // 2)
        p3 = _proj(h, norm_mix[l], w3, w3.shape[1] // 2)
        pg = _proj(h, norm_mix[l], wg, D_MODEL)
        p1s, p2s, p3s = (_pad_sample_rows(p[n_p:], bs, ts) for p in (p1, p2, p3))

        prm = jnp.zeros((SUBLANES, LANES), F32)
        prm = prm.at[0, H_A:2 * H_A].set(a_log[l]).at[1, H_A:2 * H_A].set(dt_bias[l])
        gnorm = gdn_norm[l].reshape(1, DV_A)
        up_w = jnp.pad(gla_gk_up[l].reshape(GLA_RANK, H_B, DK_B),
                       ((0, LANES - GLA_RANK), (0, 0), (0, LANES - DK_B))).reshape(LANES, H_B * LANES).astype(BF16)
        gk_bias = jnp.pad(gla_gk_bias[l].reshape(H_B, DK_B), ((0, 0), (0, LANES - DK_B))).reshape(1, H_B * LANES)
        lnorm = gla_norm[l].reshape(1, DV_B)
        mu = rwkv_mu[l].reshape(1, N_CC)
        wwa = jnp.zeros((LANES, 2 * BRANCH_W), F32)
        wwa = wwa.at[0:DECAY_LORA, 0:BRANCH_W].set(rwkv_w_up[l])
        wwa = wwa.at[DECAY_LORA:DECAY_LORA + AAA_LORA, BRANCH_W:].set(rwkv_a_up[l]).astype(BF16)
        gup = rwkv_g_up[l].astype(BF16)
        vec = jnp.stack([rwkv_w0[l], rwkv_a0[l], rwkv_k_k[l], rwkv_k_a[l], rwkv_r_k[l],
                         rwkv_ln_w[l], rwkv_ln_b[l], jnp.zeros((BRANCH_W,), F32)], axis=0)

        ya_p, gdn_p = _gdn(p1, p3, None, None, conv_a[l], prm, gnorm,
                           nbatch=bp, nchunk=nchunk_p, nseq=1, null_rows=0)
        ya_s, gdn_s = _gdn(p1s, p3s, state_gdn_conv[l], state_gdn[l], conv_a[l], prm, gnorm,
                           nbatch=nstep_s, nchunk=1, nseq=seq_per_chunk, null_rows=null_rows)
        yb_p, gla_p = _gla(p2, p3, None, up_w, gk_bias, lnorm,
                           nbatch=bp, nchunk=nchunk_p, nseq=1, null_rows=0)
        yb_s, gla_s = _gla(p2s, p3s, state_gla[l], up_w, gk_bias, lnorm,
                           nbatch=nstep_s, nchunk=1, nseq=seq_per_chunk, null_rows=null_rows)
        yc_p, rwkv_p = _rwkv(p3, None, None, mu, wwa, gup, vec,
                             nbatch=bp, nchunk=nchunk_p, nseq=1, null_rows=0)
        yc_s, rwkv_s = _rwkv(p3s, state_rwkv_shift[l],
                             state_rwkv[l].reshape(bs, N_PAIR, 2 * N_C, N_C), mu, wwa, gup, vec,
                             nbatch=nstep_s, nchunk=1, nseq=seq_per_chunk, null_rows=null_rows)

        def join(y_p, y_s):
            y_s = y_s.reshape(bs, SAMPLE_ROWS, BRANCH_W)[:, null_rows:].reshape(n_s, BRANCH_W)
            return jnp.concatenate([y_p, y_s], axis=0)

        h = _merge(join(ya_p, ya_s), join(yb_p, yb_s), join(yc_p, yc_s), pg,
                   w_branch[l].astype(BF16), w_out[l].astype(BF16), h)
        x = _ffn(h, norm_ff2[l], *_prep_ffn(w_ff2_up[l], w_ff2_down[l]))

        outs["gdn_p"].append(gdn_p)
        outs["gdn_s"].append(gdn_s)
        outs["gla_p"].append(gla_p)
        outs["gla_s"].append(gla_s)
        outs["rwkv_p"].append(rwkv_p.reshape(bp, H_C, N_C, N_C))
        outs["rwkv_s"].append(rwkv_s.reshape(bs, H_C, N_C, N_C))
        outs["conv_p"].append(p1[:n_p].reshape(bp, tp, -1)[:, tp - (CONV_W - 1):, :QKV_A])
        outs["conv_s"].append(p1[n_p:].reshape(bs, ts, -1)[:, ts - (CONV_W - 1):, :QKV_A])
        outs["shift_p"].append(p3[:n_p].reshape(bp, tp, -1)[:, tp - 1:, :N_CC])
        outs["shift_s"].append(p3[n_p:].reshape(bs, ts, -1)[:, ts - 1:, :N_CC])

    y = _final_norm(x, norm_final)
    st = {k: jnp.stack(v) for k, v in outs.items()}
    return (y[:n_p].reshape(bp, tp, d), y[n_p:].reshape(bs, ts, d),
            st["gdn_p"], st["gdn_s"], st["conv_p"], st["conv_s"], st["gla_p"], st["gla_s"],
            st["rwkv_p"], st["rwkv_s"], st["shift_p"], st["shift_s"])
```

```python
import functools
import math

import jax
import jax.numpy as jnp
from jax import lax
from jax.experimental import pallas as pl
from jax.experimental.pallas import tpu as pltpu

F32 = jnp.float32
BF16 = jnp.bfloat16
HIGHEST = lax.Precision.HIGHEST

D_MODEL = 2048
N_BRANCH = 3
BRANCH_W = 768
DK_A, DV_A, H_A, CONV_W = 128, 128, 6, 4
DK_B, DV_B, H_B, GLA_RANK, GLA_NORMALIZER = 64, 128, 6, 16, 16.0
N_C, H_C, DECAY_LORA, AAA_LORA, GATE_LORA = 64, 12, 64, 64, 128
GN_EPS = 64e-5
D_FF = 5504
NORM_EPS = 1e-6
QKV_A = 2 * H_A * DK_A + H_A * DV_A
N_A = QKV_A + H_A * DV_A + 2 * H_A
N_B = 2 * H_B * DK_B + 2 * H_B * DV_B + GLA_RANK
N_CC = 3 * BRANCH_W + DECAY_LORA + AAA_LORA + GATE_LORA
N_PAIR = H_C // 2

LANES = 128
SUBLANES = 8
CHUNK = 64
SAMPLE_ROWS = 8
FF_TILE = 512
D_FF_PAD = ((D_FF + FF_TILE - 1) // FF_TILE) * FF_TILE
VMEM_LIMIT = 60 * 1024 * 1024


def _cparams(n_axes):
    return pltpu.CompilerParams(dimension_semantics=("arbitrary",) * n_axes,
                                vmem_limit_bytes=VMEM_LIMIT)


def _pick_tile(n, prefs):
    for t in prefs:
        if n % t == 0:
            return t
    raise ValueError(f"no tile for {n}")


def _bdot(a, b):
    return jnp.dot(a.astype(BF16), b.astype(BF16), preferred_element_type=F32)


def _bdot_nt(a, b):
    return lax.dot_general(a.astype(BF16), b.astype(BF16), (((1,), (1,)), ((), ())),
                           preferred_element_type=F32)


def _bdot_tn(a, b):
    return lax.dot_general(a.astype(BF16), b.astype(BF16), (((0,), (0,)), ((), ())),
                           preferred_element_type=F32)


def _hdot(a, b):
    return jnp.dot(a, b, preferred_element_type=F32, precision=HIGHEST)


def _sigmoid(x):
    return jax.nn.sigmoid(x)


def _silu(x):
    return x * _sigmoid(x)


def _softplus(x):
    return jnp.maximum(x, 0.0) + jnp.log(1.0 + jnp.exp(-jnp.abs(x)))


def _rmsnorm_rows(x, g):
    return x * lax.rsqrt(jnp.mean(x * x, axis=-1, keepdims=True) + NORM_EPS) * g


def _chunk_masks(rows, seq_len):
    sh = int(math.log2(seq_len))
    ri = lax.broadcasted_iota(jnp.int32, (rows, rows), 0)
    ci = lax.broadcasted_iota(jnp.int32, (rows, rows), 1)
    same = (ri >> sh) == (ci >> sh)
    return same & (ri >= ci), same & (ri > ci), same


def _eye(rows):
    return (lax.broadcasted_iota(jnp.int32, (rows, rows), 0)
            == lax.broadcasted_iota(jnp.int32, (rows, rows), 1)).astype(F32)


def _inv_unit_lower(lms, eye, nil):
    xs = [eye - lm for lm in lms]
    ps = list(lms)
    k = 2
    while k < nil:
        ps = [_bdot(p, p) for p in ps]
        xs = [x + _bdot(x, p) for x, p in zip(xs, ps)]
        k *= 2
    return xs


def _row_valid(rows, seq_len, null_rows, width):
    r = lax.broadcasted_iota(jnp.int32, (rows, width), 0)
    return (r & (seq_len - 1)) >= null_rows


def _seq_rows(rows, seq_len, s, width):
    r = lax.broadcasted_iota(jnp.int32, (rows, width), 0) & (CHUNK - 1)
    return (r >> int(math.log2(seq_len))) == s


def _pick(sel, xv):
    return xv if sel is None else jnp.where(sel, xv, 0.0)


def _ffn_body(x_ref, g_ref, wg_ref, wu_ref, wd_ref, o_ref, xn_ref):
    j = pl.program_id(1)

    @pl.when(j == 0)
    def _():
        xn_ref[...] = _rmsnorm_rows(x_ref[...], g_ref[...]).astype(BF16)
        o_ref[...] = jnp.zeros_like(o_ref)

    xn = xn_ref[...]
    gate = jnp.dot(xn, wg_ref[...], preferred_element_type=F32)
    up = jnp.dot(xn, wu_ref[...], preferred_element_type=F32)
    act = (_silu(gate) * up).astype(BF16)
    o_ref[...] += jnp.dot(act, wd_ref[...], preferred_element_type=F32)

    @pl.when(j == pl.num_programs(1) - 1)
    def _():
        o_ref[...] = x_ref[...] + 0.5 * o_ref[...]


def _ffn(x, g, wg, wu, wd):
    n, d = x.shape
    tm = _pick_tile(n, (1088, 544, 512, 256, 128, 64, 32, 16, 8))
    nf = wg.shape[1] // FF_TILE
    return pl.pallas_call(
        _ffn_body,
        grid=(n // tm, nf),
        in_specs=[
            pl.BlockSpec((tm, d), lambda i, j: (i, 0)),
            pl.BlockSpec((1, d), lambda i, j: (0, 0)),
            pl.BlockSpec((d, FF_TILE), lambda i, j: (0, j)),
            pl.BlockSpec((d, FF_TILE), lambda i, j: (0, j)),
            pl.BlockSpec((FF_TILE, d), lambda i, j: (j, 0)),
        ],
        out_specs=pl.BlockSpec((tm, d), lambda i, j: (i, 0)),
        out_shape=jax.ShapeDtypeStruct((n, d), F32),
        scratch_shapes=[pltpu.VMEM((tm, d), BF16)],
        compiler_params=_cparams(2),
        name="ffn",
    )(x, g.reshape(1, d), wg, wu, wd)


def _proj_body(x_ref, g_ref, w_ref, o_ref):
    xn = _rmsnorm_rows(x_ref[...], g_ref[...]).astype(BF16)
    o_ref[...] = jnp.dot(xn, w_ref[...], preferred_element_type=F32)


def _proj(x, g, w, tn):
    n, d = x.shape
    nn = w.shape[1]
    tm = _pick_tile(n, (544, 512, 256, 128, 64, 32, 16, 8))
    return pl.pallas_call(
        _proj_body,
        grid=(nn // tn, n // tm),
        in_specs=[
            pl.BlockSpec((tm, d), lambda j, i: (i, 0)),
            pl.BlockSpec((1, d), lambda j, i: (0, 0)),
            pl.BlockSpec((d, tn), lambda j, i: (0, j)),
        ],
        out_specs=pl.BlockSpec((tm, tn), lambda j, i: (i, j)),
        out_shape=jax.ShapeDtypeStruct((n, nn), F32),
        compiler_params=_cparams(2),
        name="proj",
    )(x, g.reshape(1, d), w)


MERGE_TN = 512


def _merge_body(ya_ref, yb_ref, yc_ref, ga_ref, gb_ref, gc_ref, wb_ref, wo_ref, h_ref, o_ref):
    j = pl.program_id(1)

    @pl.when(j == 0)
    def _():
        o_ref[...] = h_ref[...]

    m = _sigmoid(ga_ref[...]) * jnp.dot(ya_ref[...], wb_ref[0], preferred_element_type=F32)
    m += _sigmoid(gb_ref[...]) * jnp.dot(yb_ref[...], wb_ref[1], preferred_element_type=F32)
    m += _sigmoid(gc_ref[...]) * jnp.dot(yc_ref[...], wb_ref[2], preferred_element_type=F32)
    o_ref[...] += jnp.dot(m.astype(BF16), wo_ref[...], preferred_element_type=F32)


def _merge(ya, yb, yc, pg, wb, wo, h):
    n, d = h.shape
    tm = _pick_tile(n, (544, 512, 256, 128, 64, 32, 16))
    nj = d // MERGE_TN
    yspec = pl.BlockSpec((tm, BRANCH_W), lambda i, j: (i, 0))
    return pl.pallas_call(
        _merge_body,
        grid=(n // tm, nj),
        in_specs=[
            yspec, yspec, yspec,
            pl.BlockSpec((tm, MERGE_TN), lambda i, j: (i, j)),
            pl.BlockSpec((tm, MERGE_TN), lambda i, j: (i, nj + j)),
            pl.BlockSpec((tm, MERGE_TN), lambda i, j: (i, 2 * nj + j)),
            pl.BlockSpec((N_BRANCH, BRANCH_W, MERGE_TN), lambda i, j: (0, 0, j)),
            pl.BlockSpec((MERGE_TN, d), lambda i, j: (j, 0)),
            pl.BlockSpec((tm, d), lambda i, j: (i, 0)),
        ],
        out_specs=pl.BlockSpec((tm, d), lambda i, j: (i, 0)),
        out_shape=jax.ShapeDtypeStruct((n, d), F32),
        compiler_params=_cparams(2),
        name="merge",
    )(ya, yb, yc, pg, pg, pg, wb, wo, h)


def _final_norm_body(x_ref, g_ref, o_ref):
    o_ref[...] = _rmsnorm_rows(x_ref[...], g_ref[...])


def _final_norm(x, g):
    n, d = x.shape
    tm = _pick_tile(n, (544, 512, 256, 128, 64, 32, 16, 8))
    return pl.pallas_call(
        _final_norm_body,
        grid=(n // tm,),
        in_specs=[pl.BlockSpec((tm, d), lambda i: (i, 0)), pl.BlockSpec((1, d), lambda i: (0, 0))],
        out_specs=pl.BlockSpec((tm, d), lambda i: (i, 0)),
        out_shape=jax.ShapeDtypeStruct((n, d), F32),
        compiler_params=_cparams(1),
        name="final_norm",
    )(x, g.reshape(1, d))


def _state_specs(stack, layer, nseq, tail):
    zeros = (0,) * len(tail)
    return (pl.BlockSpec(memory_space=pl.ANY),
            pl.BlockSpec((None, nseq) + tail, lambda b, c: (layer, b) + zeros),
            jax.ShapeDtypeStruct(stack.shape, stack.dtype))


def _gdn_body(*refs, nseq, null_rows, has_state):
    if has_state:
        (qkv_ref, z_ref, ba_ref, cs_ref, cw_ref, prm_ref, nw_ref, s0_ref, _,
         y_ref, so_ref, xbuf) = refs
    else:
        qkv_ref, z_ref, ba_ref, cw_ref, prm_ref, nw_ref, _, y_ref, so_ref, xbuf = refs
        s0_ref = so_ref
    rows = qkv_ref.shape[0]
    seq_len = rows // nseq
    c = pl.program_id(1)
    heads = range(H_A)

    @pl.when(c == 0)
    def _():
        xbuf[0:SUBLANES, :] = jnp.zeros((SUBLANES, QKV_A), F32)
        if not has_state:
            so_ref[...] = jnp.zeros_like(so_ref)

    @pl.when(c > 0)
    def _():
        xbuf[0:SUBLANES, :] = xbuf[rows:rows + SUBLANES, :]

    xbuf[SUBLANES:SUBLANES + rows, :] = qkv_ref[...]
    if has_state:
        for s in range(nseq):
            r0 = SUBLANES + s * seq_len + null_rows - (CONV_W - 1)
            xbuf[r0:r0 + CONV_W - 1, :] = cs_ref[s]

    incl, strict, same = _chunk_masks(rows, seq_len)
    eye = _eye(rows)
    valid = _row_valid(rows, seq_len, null_rows, LANES) if null_rows else None
    sels = [_seq_rows(rows, seq_len, s, LANES) for s in range(nseq)] if nseq > 1 else [None]
    sels2 = [_seq_rows(2 * rows, seq_len, s, LANES) for s in range(nseq)] if nseq > 1 else [None]

    ba = ba_ref[...]
    beta_all = _sigmoid(ba)
    g_all = -jnp.exp(prm_ref[0:1, :]) * _softplus(ba + prm_ref[1:2, :])
    if null_rows:
        g_all = jnp.where(valid, g_all, 0.0)
    lstack = jnp.concatenate([incl.astype(F32), same.astype(F32)], axis=0)
    cs = _hdot(lstack, g_all)
    gc_all, gt_all = cs[0:rows], cs[rows:2 * rows]
    gc_t = gc_all.T

    def conv_cols(c0):
        acc = xbuf[SUBLANES - 3:SUBLANES - 3 + rows, c0:c0 + LANES] * cw_ref[0:1, c0:c0 + LANES]
        for j in range(1, CONV_W):
            acc += (xbuf[SUBLANES - 3 + j:SUBLANES - 3 + j + rows, c0:c0 + LANES]
                    * cw_ref[j:j + 1, c0:c0 + LANES])
        return _silu(acc)

    qs, ks, kbs, rhs, decays, wq_parts, kds, gtcs = [], [], [], [], [], [], [], []
    for h in heads:
        q = conv_cols(h * DK_A)
        k = conv_cols(H_A * DK_A + h * DK_A)
        v = conv_cols(2 * H_A * DK_A + h * DV_A)
        q = q * lax.rsqrt(jnp.sum(q * q, axis=-1, keepdims=True) + 1e-6) * DK_A ** -0.5
        k = k * lax.rsqrt(jnp.sum(k * k, axis=-1, keepdims=True) + 1e-6)
        if null_rows:
            q = jnp.where(valid, q, 0.0)
            k = jnp.where(valid, k, 0.0)
            v = jnp.where(valid, v, 0.0)
        beta = beta_all[:, h:h + 1]
        gcc = gc_all[:, H_A + h:H_A + h + 1]
        gtc = gt_all[:, H_A + h:H_A + h + 1]
        gcr = gc_t[H_A + h:H_A + h + 1, :]
        egc = jnp.exp(gcc)
        kb = k * beta
        qs.append(q)
        ks.append(k)
        kbs.append(kb)
        rhs.append(jnp.concatenate([v * beta, kb * egc], axis=1))
        decays.append(jnp.where(incl, jnp.exp(jnp.where(incl, gcc - gcr, 0.0)), 0.0))
        wq_parts.append(q * egc)
        kds.append(k * jnp.exp(gtc - gcc))
        gtcs.append(gtc)

    lms = [jnp.where(strict, _bdot_nt(kbs[h], ks[h]) * decays[h], 0.0) for h in heads]
    attns = [_bdot_nt(qs[h], ks[h]) * decays[h] for h in heads]
    tmats = _inv_unit_lower(lms, eye, seq_len)
    uws = [_bdot(tmats[h], rhs[h]) for h in heads]
    us, os_ = [], []
    for h in heads:
        u = uws[h][:, 0:DV_A]
        wq = jnp.concatenate([uws[h][:, DV_A:DV_A + DK_A], wq_parts[h]], axis=0)
        o = None
        for s in range(nseq):
            both = _bdot(_pick(sels2[s], wq), s0_ref[s, h])
            u = u - both[0:rows]
            o = both[rows:2 * rows] if o is None else o + both[rows:2 * rows]
        us.append(u)
        os_.append(o)
    os_ = [os_[h] + _bdot(attns[h], us[h]) for h in heads]
    for h in heads:
        for s in range(nseq):
            g_last = jnp.exp(gtcs[h][s * seq_len:s * seq_len + 1, :])
            so_ref[s, h] = g_last * s0_ref[s, h] + _bdot_tn(_pick(sels[s], kds[h]), us[h])
    for h in heads:
        z = z_ref[:, h * DV_A:(h + 1) * DV_A]
        y_ref[:, h * DV_A:(h + 1) * DV_A] = (_rmsnorm_rows(os_[h], nw_ref[...]) * _silu(z)).astype(y_ref.dtype)


def _gdn(p1, p3, conv_state, state, stack, layer, conv_w, prm, norm_w, *, nbatch, nchunk, nseq, null_rows):
    has_state = state is not None
    rows = CHUNK
    in_specs = [
        pl.BlockSpec((rows, QKV_A), lambda b, c: (b * nchunk + c, 0)),
        pl.BlockSpec((rows, BRANCH_W), lambda b, c: (b * nchunk + c, QKV_A // BRANCH_W)),
        pl.BlockSpec((rows, LANES), lambda b, c: (b * nchunk + c, N_CC // LANES)),
    ]
    args = [p1, p1, p3]
    if has_state:
        in_specs.append(pl.BlockSpec((nseq, CONV_W - 1, QKV_A), lambda b, c: (b, 0, 0)))
        args.append(conv_state)
    in_specs += [
        pl.BlockSpec((CONV_W, QKV_A), lambda b, c: (0, 0)),
        pl.BlockSpec((SUBLANES, LANES), lambda b, c: (0, 0)),
        pl.BlockSpec((1, DV_A), lambda b, c: (0, 0)),
    ]
    args += [conv_w, prm, norm_w]
    if has_state:
        in_specs.append(pl.BlockSpec((nseq, H_A, DK_A, DV_A), lambda b, c: (b, 0, 0, 0)))
        args.append(state)
    stack_in, stack_out, stack_shape = _state_specs(stack, layer, nseq, (H_A, DK_A, DV_A))
    in_specs.append(stack_in)
    args.append(stack)
    return pl.pallas_call(
        functools.partial(_gdn_body, nseq=nseq, null_rows=null_rows, has_state=has_state),
        grid=(nbatch, nchunk),
        in_specs=in_specs,
        out_specs=[pl.BlockSpec((rows, BRANCH_W), lambda b, c: (b * nchunk + c, 0)), stack_out],
        out_shape=[jax.ShapeDtypeStruct((nbatch * nchunk * rows, BRANCH_W), BF16), stack_shape],
        input_output_aliases={len(args) - 1: 1},
        scratch_shapes=[pltpu.VMEM((rows + SUBLANES, QKV_A), F32)],
        compiler_params=_cparams(2),
        name="gdn",
    )(*args)


def _gla_body(*refs, nseq, null_rows, has_state):
    if has_state:
        (q_ref, k_ref, v_ref, g_ref, gkl_ref, up_ref, bias_ref, nw_ref, s0_ref, _, y_ref, so_ref) = refs
    else:
        q_ref, k_ref, v_ref, g_ref, gkl_ref, up_ref, bias_ref, nw_ref, _, y_ref, so_ref = refs
        s0_ref = so_ref
    rows = q_ref.shape[0]
    seq_len = rows // nseq
    c = pl.program_id(1)
    heads = range(H_B)
    if not has_state:
        @pl.when(c == 0)
        def _():
            so_ref[...] = jnp.zeros_like(so_ref)

    incl, _, same = _chunk_masks(rows, seq_len)
    ci = lax.broadcasted_iota(jnp.int32, (rows, rows), 1)
    first_half = same & ((ci & (seq_len - 1)) < seq_len // 2)
    wide = H_B * LANES
    x = _bdot(gkl_ref[...], up_ref[...]) + bias_ref[...]
    gk_all = (jnp.minimum(x, 0.0) - jnp.log(1.0 + jnp.exp(-jnp.abs(x)))) / GLA_NORMALIZER
    if null_rows:
        gk_all = jnp.where(_row_valid(rows, seq_len, null_rows, wide), gk_all, 0.0)
    lstack = jnp.concatenate([incl.astype(F32), same.astype(F32), first_half.astype(F32)], axis=0)
    cs = _hdot(lstack, gk_all)
    bc_all, bt_all, an_all = cs[0:rows], cs[rows:2 * rows], cs[2 * rows:3 * rows]
    bt_t = bt_all.T
    valid_k = _row_valid(rows, seq_len, null_rows, DK_B) if null_rows else None
    valid_v = _row_valid(rows, seq_len, null_rows, DV_B) if null_rows else None
    sels = [_seq_rows(rows, seq_len, s, DK_B) for s in range(nseq)] if nseq > 1 else [None]

    qis, kis, vs, q_ins, k_decs = [], [], [], [], []
    for h in heads:
        c0 = h * LANES
        q = q_ref[:, c0:c0 + DK_B] * DK_B ** -0.5
        k = k_ref[:, c0:c0 + DK_B]
        v = v_ref[:, h * DV_B:(h + 1) * DV_B]
        if null_rows:
            q = jnp.where(valid_k, q, 0.0)
            k = jnp.where(valid_k, k, 0.0)
            v = jnp.where(valid_v, v, 0.0)
        bc = bc_all[:, c0:c0 + DK_B]
        bt = bt_all[:, c0:c0 + DK_B]
        an = an_all[:, c0:c0 + DK_B]
        qis.append(q * jnp.exp(bc - an))
        kis.append(k * jnp.exp(an - bc))
        vs.append(v)
        q_ins.append(q * jnp.exp(bc))
        k_decs.append(k * jnp.exp(bt - bc))

    attns = [jnp.where(incl, _bdot_nt(qis[h], kis[h]), 0.0) for h in heads]
    os_ = [_bdot(attns[h], vs[h]) for h in heads]
    for h in heads:
        for s in range(nseq):
            os_[h] = os_[h] + _bdot(_pick(sels[s], q_ins[h]), s0_ref[s, h])
    for h in heads:
        c0 = h * LANES
        for s in range(nseq):
            a_last = jnp.exp(bt_t[c0:c0 + DK_B, s * seq_len:s * seq_len + 1])
            so_ref[s, h] = a_last * s0_ref[s, h] + _bdot_tn(_pick(sels[s], k_decs[h]), vs[h])
    for h in heads:
        g = g_ref[:, h * DV_B:(h + 1) * DV_B]
        y_ref[:, h * DV_B:(h + 1) * DV_B] = (_rmsnorm_rows(os_[h], nw_ref[...]) * _silu(g)).astype(y_ref.dtype)


def _gla(p2, p3, state, stack, layer, up_w, bias, norm_w, *, nbatch, nchunk, nseq, null_rows):
    has_state = state is not None
    rows = CHUNK
    wide = H_B * LANES

    def col(j):
        return pl.BlockSpec((rows, wide), lambda b, c: (b * nchunk + c, j))

    in_specs = [col(0), col(1), col(2), col(3),
                pl.BlockSpec((rows, LANES), lambda b, c: (b * nchunk + c, N_CC // LANES + 1)),
                pl.BlockSpec((LANES, wide), lambda b, c: (0, 0)),
                pl.BlockSpec((1, wide), lambda b, c: (0, 0)),
                pl.BlockSpec((1, DV_B), lambda b, c: (0, 0))]
    args = [p2, p2, p2, p2, p3, up_w, bias, norm_w]
    if has_state:
        in_specs.append(pl.BlockSpec((nseq, H_B, DK_B, DV_B), lambda b, c: (b, 0, 0, 0)))
        args.append(state)
    stack_in, stack_out, stack_shape = _state_specs(stack, layer, nseq, (H_B, DK_B, DV_B))
    in_specs.append(stack_in)
    args.append(stack)
    return pl.pallas_call(
        functools.partial(_gla_body, nseq=nseq, null_rows=null_rows, has_state=has_state),
        grid=(nbatch, nchunk),
        in_specs=in_specs,
        out_specs=[pl.BlockSpec((rows, BRANCH_W), lambda b, c: (b * nchunk + c, 0)), stack_out],
        out_shape=[jax.ShapeDtypeStruct((nbatch * nchunk * rows, BRANCH_W), BF16), stack_shape],
        input_output_aliases={len(args) - 1: 1},
        compiler_params=_cparams(2),
        name="gla",
    )(*args)


def _rwkv_body(*refs, nseq, null_rows, has_state):
    if has_state:
        (pc_ref, sh_ref, mu_ref, wwa_ref, gup_ref, vec_ref, s0_ref, _, y_ref, so_ref, pbuf, sp_ref) = refs
    else:
        pc_ref, mu_ref, wwa_ref, gup_ref, vec_ref, _, y_ref, so_ref, pbuf, sp_ref = refs
    rows = pc_ref.shape[0]
    seq_len = rows // nseq
    c = pl.program_id(1)
    pairs = range(N_PAIR)

    @pl.when(c == 0)
    def _():
        pbuf[0:SUBLANES, :] = jnp.zeros((SUBLANES, N_CC), F32)
        if not has_state:
            sp_ref[...] = jnp.zeros_like(sp_ref)

    @pl.when(c > 0)
    def _():
        pbuf[0:SUBLANES, :] = pbuf[rows:rows + SUBLANES, :]

    pbuf[SUBLANES:SUBLANES + rows, :] = pc_ref[...]
    if has_state:
        for s in range(nseq):
            r0 = SUBLANES + s * seq_len + null_rows - 1
            pbuf[r0:r0 + 1, :] = sh_ref[s]

    def xc_cols(c0, w):
        cur = pbuf[SUBLANES:SUBLANES + rows, c0:c0 + w]
        prev = pbuf[SUBLANES - 1:SUBLANES - 1 + rows, c0:c0 + w]
        return cur + (prev - cur) * mu_ref[:, c0:c0 + w]

    incl, strict, same = _chunk_masks(rows, seq_len)
    eye = _eye(rows)
    lane = lax.broadcasted_iota(jnp.int32, (rows, LANES), 1)
    lo = lane < N_C
    lo2 = lax.broadcasted_iota(jnp.int32, (2 * rows, LANES), 1) < N_C
    top2 = lax.broadcasted_iota(jnp.int32, (2 * rows, LANES), 0) < rows
    valid = _row_valid(rows, seq_len, null_rows, LANES) if null_rows else None
    sels2 = [_seq_rows(2 * rows, seq_len, s, LANES) for s in range(nseq)] if nseq > 1 else [None]

    lora = xc_cols(3 * BRANCH_W, DECAY_LORA + AAA_LORA + GATE_LORA)
    wa_in = lora[:, 0:LANES]
    wa_in = jnp.where(lo, jnp.tanh(wa_in), wa_in)
    wa = _bdot(wa_in, wwa_ref[...])
    g_all = _bdot(_sigmoid(lora[:, LANES:2 * LANES]), gup_ref[...])
    w0 = vec_ref[0:1, :]
    a0 = vec_ref[1:2, :]
    w_log = -_softplus(-(w0 + wa[:, 0:BRANCH_W])) - 0.5
    lw_all = -jnp.exp(w_log)
    a_all = _sigmoid(a0 + wa[:, BRANCH_W:2 * BRANCH_W])
    if null_rows:
        lw_all = jnp.where(_row_valid(rows, seq_len, null_rows, BRANCH_W), lw_all, 0.0)
    lstack = jnp.concatenate([incl.astype(F32), same.astype(F32)], axis=0)
    cs = _hdot(lstack, lw_all)
    cum_all, tot_all = cs[0:rows], cs[rows:2 * rows]

    r2 = lax.broadcasted_iota(jnp.int32, (2 * N_C, LANES), 0)
    c2 = lax.broadcasted_iota(jnp.int32, (2 * N_C, LANES), 1)
    blockdiag = (r2 >= N_C) == (c2 >= N_C)

    def seg_sum(xv):
        s_lo = jnp.sum(jnp.where(lo, xv, 0.0), axis=-1, keepdims=True)
        s_hi = jnp.sum(jnp.where(lo, 0.0, xv), axis=-1, keepdims=True)
        return jnp.where(lo, s_lo, s_hi)

    def by_head(stacked):
        return jnp.where(lo, stacked[0:rows], stacked[rows:2 * rows])

    rs, k2s, vs, ar_stacks, b_ts, k_ts = [], [], [], [], [], []
    for p in pairs:
        c0 = p * LANES
        r = xc_cols(c0, LANES)
        k = xc_cols(BRANCH_W + c0, LANES)
        v = xc_cols(2 * BRANCH_W + c0, LANES)
        a_p = a_all[:, c0:c0 + LANES]
        kkr = k * vec_ref[2:3, c0:c0 + LANES]
        kk = kkr * lax.rsqrt(seg_sum(kkr * kkr) + 1e-6)
        k2 = k * (1.0 + (a_p - 1.0) * vec_ref[3:4, c0:c0 + LANES])
        av = -kk
        bv = kk * a_p
        if null_rows:
            r = jnp.where(valid, r, 0.0)
            k2 = jnp.where(valid, k2, 0.0)
            v = jnp.where(valid, v, 0.0)
            av = jnp.where(valid, av, 0.0)
            bv = jnp.where(valid, bv, 0.0)
        cum = cum_all[:, c0:c0 + LANES]
        lw = lw_all[:, c0:c0 + LANES]
        g_inv = jnp.exp(-cum)
        rs.append(r)
        k2s.append(k2)
        vs.append(v)
        ar_stacks.append(jnp.concatenate([av * jnp.exp(cum - lw), r * jnp.exp(cum)], axis=0))
        b_ts.append(bv * g_inv)
        k_ts.append(k2 * g_inv)

    a_abs, a_rbs, a_aks, a_rks = [], [], [], []
    for p in pairs:
        for hh in range(2):
            lhs = jnp.where(lo2 if hh == 0 else jnp.logical_not(lo2), ar_stacks[p], 0.0)
            mb = _bdot_nt(lhs, b_ts[p])
            mk = _bdot_nt(lhs, k_ts[p])
            a_abs.append(jnp.where(strict, mb[0:rows], 0.0))
            a_rbs.append(jnp.where(incl, mb[rows:2 * rows], 0.0))
            a_aks.append(jnp.where(strict, mk[0:rows], 0.0))
            a_rks.append(jnp.where(incl, mk[rows:2 * rows], 0.0))
    tmats = _inv_unit_lower([-m for m in a_abs], eye, seq_len)

    def pair_stack(mats, p):
        return jnp.concatenate([mats[2 * p], mats[2 * p + 1]], axis=0)

    states, xss = [], []
    for p in pairs:
        sps, xs = [], None
        for s in range(nseq):
            if has_state:
                sv = jnp.concatenate([s0_ref[s, 2 * p], s0_ref[s, 2 * p + 1]], axis=0)
                sp = jnp.where(blockdiag, jnp.concatenate([sv, sv], axis=1), 0.0)
            else:
                sp = sp_ref[p]
            both = _bdot_nt(_pick(sels2[s], ar_stacks[p]), sp)
            xs = both if xs is None else xs + both
            sps.append(sp)
        states.append(sps)
        xss.append(xs)
    yvs = [xss[p][0:rows] + by_head(_bdot(pair_stack(a_aks, p), vs[p])) for p in pairs]
    us = [by_head(_bdot(pair_stack(tmats, p), yvs[p])) for p in pairs]
    os_ = [xss[p][rows:2 * rows]
           + by_head(_bdot(pair_stack(a_rbs, p), us[p]) + _bdot(pair_stack(a_rks, p), vs[p]))
           for p in pairs]
    for p in pairs:
        c0 = p * LANES
        uv = jnp.concatenate([us[p], vs[p]], axis=0)
        bk = jnp.concatenate([b_ts[p], k_ts[p]], axis=0)
        tot = tot_all[:, c0:c0 + LANES]
        for s in range(nseq):
            upd = jnp.where(blockdiag, _bdot_tn(_pick(sels2[s], uv), bk), 0.0)
            sp_new = (states[p][s] + upd) * jnp.exp(tot[s * seq_len:s * seq_len + 1, :])
            if has_state:
                so_ref[s, 2 * p] = sp_new[0:N_C, 0:N_C]
                so_ref[s, 2 * p + 1] = pltpu.roll(sp_new[N_C:2 * N_C, :], N_C, 1)[:, 0:N_C]
            else:
                sp_ref[p] = sp_new

    for p in pairs:
        c0 = p * LANES
        o = os_[p]
        mean = seg_sum(o) * (1.0 / N_C)
        d = o - mean
        var = seg_sum(d * d) * (1.0 / N_C)
        on = d * lax.rsqrt(var + GN_EPS) * vec_ref[5:6, c0:c0 + LANES] + vec_ref[6:7, c0:c0 + LANES]
        bonus = seg_sum(rs[p] * k2s[p] * vec_ref[4:5, c0:c0 + LANES]) * vs[p]
        y_ref[:, c0:c0 + LANES] = ((on + bonus) * g_all[:, c0:c0 + LANES]).astype(y_ref.dtype)

    if not has_state:
        @pl.when(c == pl.num_programs(1) - 1)
        def _():
            for p in pairs:
                sp = sp_ref[p]
                so_ref[0, 2 * p] = sp[0:N_C, 0:N_C]
                so_ref[0, 2 * p + 1] = pltpu.roll(sp[N_C:2 * N_C, :], N_C, 1)[:, 0:N_C]


def _rwkv(p3, shift_state, state, stack, layer, mu, wwa, gup, vec, *, nbatch, nchunk, nseq, null_rows):
    has_state = state is not None
    rows = CHUNK
    in_specs = [pl.BlockSpec((rows, N_CC), lambda b, c: (b * nchunk + c, 0))]
    args = [p3]
    if has_state:
        in_specs.append(pl.BlockSpec((nseq, 1, N_CC), lambda b, c: (b, 0, 0)))
        args.append(shift_state)
    in_specs += [
        pl.BlockSpec((1, N_CC), lambda b, c: (0, 0)),
        pl.BlockSpec((LANES, 2 * BRANCH_W), lambda b, c: (0, 0)),
        pl.BlockSpec((GATE_LORA, BRANCH_W), lambda b, c: (0, 0)),
        pl.BlockSpec((SUBLANES, BRANCH_W), lambda b, c: (0, 0)),
    ]
    args += [mu, wwa, gup, vec]
    if has_state:
        in_specs.append(pl.BlockSpec((nseq, H_C, N_C, N_C), lambda b, c: (b, 0, 0, 0)))
        args.append(state)
    stack_in, stack_out, stack_shape = _state_specs(stack, layer, nseq, (H_C, N_C, N_C))
    in_specs.append(stack_in)
    args.append(stack)
    return pl.pallas_call(
        functools.partial(_rwkv_body, nseq=nseq, null_rows=null_rows, has_state=has_state),
        grid=(nbatch, nchunk),
        in_specs=in_specs,
        out_specs=[pl.BlockSpec((rows, BRANCH_W), lambda b, c: (b * nchunk + c, 0)), stack_out],
        out_shape=[jax.ShapeDtypeStruct((nbatch * nchunk * rows, BRANCH_W), BF16), stack_shape],
        input_output_aliases={len(args) - 1: 1},
        scratch_shapes=[pltpu.VMEM((rows + SUBLANES, N_CC), F32),
                        pltpu.VMEM((N_PAIR, 2 * N_C, 2 * N_C), F32)],
        compiler_params=_cparams(2),
        name="rwkv",
    )(*args)


def _pad_cols(w, n):
    return jnp.pad(w, ((0, 0), (0, n - w.shape[1])))


def _pad_heads(w):
    z = jnp.zeros((w.shape[0], LANES - DK_B), w.dtype)
    parts = []
    for h in range(H_B):
        parts += [w[:, h * DK_B:(h + 1) * DK_B], z]
    return jnp.concatenate(parts, axis=1)


def _prep_in(w_in):
    w = w_in.astype(BF16)
    wa = w[:, :N_A]
    wb = w[:, N_A:N_A + N_B]
    wc = w[:, N_A + N_B:N_A + N_B + N_CC]
    wg = w[:, N_A + N_B + N_CC:]
    w1 = wa[:, :QKV_A + H_A * DV_A]
    kq = H_B * DK_B
    w2 = jnp.concatenate([_pad_heads(wb[:, :kq]), _pad_heads(wb[:, kq:2 * kq]),
                          wb[:, 2 * kq:2 * kq + 2 * H_B * DV_B]], axis=1)
    misc = jnp.concatenate([_pad_cols(wa[:, QKV_A + H_A * DV_A:], LANES),
                            _pad_cols(wb[:, 2 * kq + 2 * H_B * DV_B:], LANES)], axis=1)
    w3 = jnp.concatenate([wc, misc], axis=1)
    return w1, w2, w3, wg


def _prep_ffn(w_up, w_down):
    wg = _pad_cols(w_up[:, :D_FF], D_FF_PAD).astype(BF16)
    wu = _pad_cols(w_up[:, D_FF:], D_FF_PAD).astype(BF16)
    wd = jnp.pad(w_down, ((0, D_FF_PAD - D_FF), (0, 0))).astype(BF16)
    return wg, wu, wd


def _pad_sample_rows(p, nb, ts):
    n = p.shape[-1]
    return jnp.pad(p.reshape(nb, ts, n), ((0, 0), (SAMPLE_ROWS - ts, 0), (0, 0))).reshape(nb * SAMPLE_ROWS, n)


def _tail_rows(p, nb, t, n_tail, width):
    return jnp.stack([p[(b + 1) * t - n_tail:(b + 1) * t, :width] for b in range(nb)])


def kernel(x_prompt, x_sample, state_gdn, state_gdn_conv, state_gla, state_rwkv, state_rwkv_shift, w_in, conv_a, a_log, dt_bias, gdn_norm, gla_gk_up, gla_gk_bias, gla_norm, rwkv_mu, rwkv_w0, rwkv_w_up, rwkv_a0, rwkv_a_up, rwkv_g_up, rwkv_k_k, rwkv_k_a, rwkv_r_k, rwkv_ln_w, rwkv_ln_b, w_branch, w_out, norm_ff1, w_ff1_up, w_ff1_down, norm_mix, norm_ff2, w_ff2_up, w_ff2_down, norm_final):
    bp, tp, d = x_prompt.shape
    bs, ts, _ = x_sample.shape
    depth = w_in.shape[0]
    n_p, n_s = bp * tp, bs * ts
    null_rows = SAMPLE_ROWS - ts
    assert tp % CHUNK == 0 and CONV_W - 1 <= null_rows < SAMPLE_ROWS
    seq_per_chunk = CHUNK // SAMPLE_ROWS
    assert bs % seq_per_chunk == 0
    nchunk_p = tp // CHUNK
    nstep_s = bs // seq_per_chunk
    prompt = dict(nbatch=bp, nchunk=nchunk_p, nseq=1, null_rows=0)
    sample = dict(nbatch=nstep_s, nchunk=1, nseq=seq_per_chunk, null_rows=null_rows)

    x = jnp.concatenate([x_prompt.reshape(n_p, d), x_sample.reshape(n_s, d)], axis=0)
    gdn_p = jnp.zeros((depth, bp, H_A, DK_A, DV_A), F32)
    gdn_s = jnp.zeros((depth, bs, H_A, DK_A, DV_A), F32)
    gla_p = jnp.zeros((depth, bp, H_B, DK_B, DV_B), F32)
    gla_s = jnp.zeros((depth, bs, H_B, DK_B, DV_B), F32)
    rwkv_p = jnp.zeros((depth, bp, H_C, N_C, N_C), F32)
    rwkv_s = jnp.zeros((depth, bs, H_C, N_C, N_C), F32)
    conv_p, conv_s, shift_p, shift_s = [], [], [], []
    for l in range(depth):
        h = _ffn(x, norm_ff1[l], *_prep_ffn(w_ff1_up[l], w_ff1_down[l]))

        w1, w2, w3, wg = _prep_in(w_in[l])
        p1 = _proj(h, norm_mix[l], w1, w1.shape[1] // 2)
        p2 = _proj(h, norm_mix[l], w2, w2.shape[1] // 2)
        p3 = _proj(h, norm_mix[l], w3, w3.shape[1] // 2)
        pg = _proj(h, norm_mix[l], wg, D_MODEL)
        p1s, p2s, p3s = (_pad_sample_rows(p[n_p:], bs, ts) for p in (p1, p2, p3))

        prm = jnp.zeros((SUBLANES, LANES), F32)
        prm = prm.at[0, H_A:2 * H_A].set(a_log[l]).at[1, H_A:2 * H_A].set(dt_bias[l])
        gnorm = gdn_norm[l].reshape(1, DV_A)
        up_w = jnp.pad(gla_gk_up[l].reshape(GLA_RANK, H_B, DK_B),
                       ((0, LANES - GLA_RANK), (0, 0), (0, LANES - DK_B))).reshape(LANES, H_B * LANES).astype(BF16)
        gk_bias = jnp.pad(gla_gk_bias[l].reshape(H_B, DK_B), ((0, 0), (0, LANES - DK_B))).reshape(1, H_B * LANES)
        lnorm = gla_norm[l].reshape(1, DV_B)
        mu = rwkv_mu[l].reshape(1, N_CC)
        wwa = jnp.zeros((LANES, 2 * BRANCH_W), F32)
        wwa = wwa.at[0:DECAY_LORA, 0:BRANCH_W].set(rwkv_w_up[l])
        wwa = wwa.at[DECAY_LORA:DECAY_LORA + AAA_LORA, BRANCH_W:].set(rwkv_a_up[l]).astype(BF16)
        gup = rwkv_g_up[l].astype(BF16)
        vec = jnp.stack([rwkv_w0[l], rwkv_a0[l], rwkv_k_k[l], rwkv_k_a[l], rwkv_r_k[l],
                         rwkv_ln_w[l], rwkv_ln_b[l], jnp.zeros((BRANCH_W,), F32)], axis=0)

        ya_p, gdn_p = _gdn(p1, p3, None, None, gdn_p, l, conv_a[l], prm, gnorm, **prompt)
        ya_s, gdn_s = _gdn(p1s, p3s, state_gdn_conv[l], state_gdn[l], gdn_s, l, conv_a[l], prm, gnorm, **sample)
        yb_p, gla_p = _gla(p2, p3, None, gla_p, l, up_w, gk_bias, lnorm, **prompt)
        yb_s, gla_s = _gla(p2s, p3s, state_gla[l], gla_s, l, up_w, gk_bias, lnorm, **sample)
        yc_p, rwkv_p = _rwkv(p3, None, None, rwkv_p, l, mu, wwa, gup, vec, **prompt)
        yc_s, rwkv_s = _rwkv(p3s, state_rwkv_shift[l], state_rwkv[l], rwkv_s, l, mu, wwa, gup, vec, **sample)

        def join(y_p, y_s):
            y_s = y_s.reshape(bs, SAMPLE_ROWS, BRANCH_W)[:, null_rows:].reshape(n_s, BRANCH_W)
            return jnp.concatenate([y_p, y_s], axis=0)

        h = _merge(join(ya_p, ya_s), join(yb_p, yb_s), join(yc_p, yc_s), pg,
                   w_branch[l].astype(BF16), w_out[l].astype(BF16), h)
        x = _ffn(h, norm_ff2[l], *_prep_ffn(w_ff2_up[l], w_ff2_down[l]))

        conv_p.append(_tail_rows(p1, bp, tp, CONV_W - 1, QKV_A))
        conv_s.append(p1[n_p:].reshape(bs, ts, -1)[:, ts - (CONV_W - 1):, :QKV_A])
        shift_p.append(_tail_rows(p3, bp, tp, 1, N_CC))
        shift_s.append(p3[n_p:].reshape(bs, ts, -1)[:, ts - 1:, :N_CC])

    y = _final_norm(x, norm_final)
    return (y[:n_p].reshape(bp, tp, d), y[n_p:].reshape(bs, ts, d),
            gdn_p, gdn_s, jnp.stack(conv_p), jnp.stack(conv_s), gla_p, gla_s,
            rwkv_p, rwkv_s, jnp.stack(shift_p), jnp.stack(shift_s))
```

```python
import functools
import math

import jax
import jax.numpy as jnp
from jax import lax
from jax.experimental import pallas as pl
from jax.experimental.pallas import tpu as pltpu

F32 = jnp.float32
BF16 = jnp.bfloat16

D_MODEL = 2048
N_BRANCH = 3
BRANCH_W = 768
DK_A, DV_A, H_A, CONV_W = 128, 128, 6, 4
DK_B, DV_B, H_B, GLA_RANK, GLA_NORMALIZER = 64, 128, 6, 16, 16.0
N_C, H_C, DECAY_LORA, AAA_LORA, GATE_LORA = 64, 12, 64, 64, 128
GN_EPS = 64e-5
D_FF = 5504
NORM_EPS = 1e-6
QKV_A = 2 * H_A * DK_A + H_A * DV_A
N_A = QKV_A + H_A * DV_A + 2 * H_A
N_B = 2 * H_B * DK_B + 2 * H_B * DV_B + GLA_RANK
N_CC = 3 * BRANCH_W + DECAY_LORA + AAA_LORA + GATE_LORA
N_PAIR = H_C // 2

LANES = 128
SUBLANES = 8
CHUNK = 64
SAMPLE_ROWS = 8
FF_TILE = 512
D_FF_PAD = ((D_FF + FF_TILE - 1) // FF_TILE) * FF_TILE
VMEM_LIMIT = 60 * 1024 * 1024


def _cparams(n_axes):
    return pltpu.CompilerParams(dimension_semantics=("arbitrary",) * n_axes,
                                vmem_limit_bytes=VMEM_LIMIT)


def _pick_tile(n, prefs):
    for t in prefs:
        if n % t == 0:
            return t
    raise ValueError(f"no tile for {n}")


def _bdot(a, b):
    return jnp.dot(a.astype(BF16), b.astype(BF16), preferred_element_type=F32)


def _bdot_nt(a, b):
    return lax.dot_general(a.astype(BF16), b.astype(BF16), (((1,), (1,)), ((), ())),
                           preferred_element_type=F32)


def _bdot_tn(a, b):
    return lax.dot_general(a.astype(BF16), b.astype(BF16), (((0,), (0,)), ((), ())),
                           preferred_element_type=F32)


def _sigmoid(x):
    return jax.nn.sigmoid(x)


def _silu(x):
    return x * _sigmoid(x)


def _softplus(x):
    return jnp.maximum(x, 0.0) + jnp.log(1.0 + jnp.exp(-jnp.abs(x)))


def _rmsnorm_rows(x, g):
    return x * lax.rsqrt(jnp.mean(x * x, axis=-1, keepdims=True) + NORM_EPS) * g


def _chunk_masks(rows, seq_len):
    sh = int(math.log2(seq_len))
    ri = lax.broadcasted_iota(jnp.int32, (rows, rows), 0)
    ci = lax.broadcasted_iota(jnp.int32, (rows, rows), 1)
    same = (ri >> sh) == (ci >> sh)
    return same & (ri >= ci), same & (ri > ci), same


def _eye(rows):
    return (lax.broadcasted_iota(jnp.int32, (rows, rows), 0)
            == lax.broadcasted_iota(jnp.int32, (rows, rows), 1)).astype(F32)


def _inv_unit_lower(lms, eye, nil):
    xs = [eye - lm for lm in lms]
    ps = list(lms)
    k = 2
    while k < nil:
        ps = [_bdot(p, p) for p in ps]
        xs = [x + _bdot(x, p) for x, p in zip(xs, ps)]
        k *= 2
    return xs


def _row_valid(rows, seq_len, null_rows, width):
    r = lax.broadcasted_iota(jnp.int32, (rows, width), 0)
    return (r & (seq_len - 1)) >= null_rows


def _seq_rows(rows, seq_len, s, width):
    r = lax.broadcasted_iota(jnp.int32, (rows, width), 0) & (CHUNK - 1)
    return (r >> int(math.log2(seq_len))) == s


def _pick(sel, xv):
    return xv if sel is None else jnp.where(sel, xv, 0.0)


def _ffn_body(x_ref, g_ref, wg_ref, wu_ref, wd_ref, o_ref, xn_ref):
    j = pl.program_id(1)

    @pl.when(j == 0)
    def _():
        xn_ref[...] = _rmsnorm_rows(x_ref[...], g_ref[...]).astype(BF16)
        o_ref[...] = jnp.zeros_like(o_ref)

    xn = xn_ref[...]
    gate = jnp.dot(xn, wg_ref[...], preferred_element_type=F32)
    up = jnp.dot(xn, wu_ref[...], preferred_element_type=F32)
    act = (_silu(gate) * up).astype(BF16)
    o_ref[...] += jnp.dot(act, wd_ref[...], preferred_element_type=F32)

    @pl.when(j == pl.num_programs(1) - 1)
    def _():
        o_ref[...] = x_ref[...] + 0.5 * o_ref[...]


def _ffn(x, g, wg, wu, wd, layer):
    n, d = x.shape
    tm = _pick_tile(n, (1088, 544, 512, 256, 128, 64, 32, 16, 8))
    nf = wg.shape[2] // FF_TILE
    return pl.pallas_call(
        _ffn_body,
        grid=(n // tm, nf),
        in_specs=[
            pl.BlockSpec((tm, d), lambda i, j: (i, 0)),
            pl.BlockSpec((1, d), lambda i, j: (0, 0)),
            pl.BlockSpec((None, d, FF_TILE), lambda i, j: (layer, 0, j)),
            pl.BlockSpec((None, d, FF_TILE), lambda i, j: (layer, 0, j)),
            pl.BlockSpec((None, FF_TILE, d), lambda i, j: (layer, j, 0)),
        ],
        out_specs=pl.BlockSpec((tm, d), lambda i, j: (i, 0)),
        out_shape=jax.ShapeDtypeStruct((n, d), F32),
        scratch_shapes=[pltpu.VMEM((tm, d), BF16)],
        compiler_params=_cparams(2),
        name="ffn",
    )(x, g.reshape(1, d), wg, wu, wd)


def _norm_cast_body(x_ref, g_ref, o_ref):
    o_ref[...] = _rmsnorm_rows(x_ref[...], g_ref[...]).astype(o_ref.dtype)


def _norm_cast(x, g, dtype):
    n, d = x.shape
    tm = _pick_tile(n, (544, 512, 256, 128, 64, 32, 16, 8))
    return pl.pallas_call(
        _norm_cast_body,
        grid=(n // tm,),
        in_specs=[pl.BlockSpec((tm, d), lambda i: (i, 0)), pl.BlockSpec((1, d), lambda i: (0, 0))],
        out_specs=pl.BlockSpec((tm, d), lambda i: (i, 0)),
        out_shape=jax.ShapeDtypeStruct((n, d), dtype),
        compiler_params=_cparams(1),
        name="norm",
    )(x, g.reshape(1, d))


def _proj_body(x_ref, w_ref, o_ref):
    o_ref[...] = jnp.dot(x_ref[...], w_ref[...], preferred_element_type=F32)


def _proj(xn, w, layer, tn):
    n, d = xn.shape
    nn = w.shape[2]
    tm = _pick_tile(n, (1088, 544, 512, 256, 128, 64, 32, 16))
    return pl.pallas_call(
        _proj_body,
        grid=(nn // tn, n // tm),
        in_specs=[
            pl.BlockSpec((tm, d), lambda j, i: (i, 0)),
            pl.BlockSpec((None, d, tn), lambda j, i: (layer, 0, j)),
        ],
        out_specs=pl.BlockSpec((tm, tn), lambda j, i: (i, j)),
        out_shape=jax.ShapeDtypeStruct((n, nn), F32),
        compiler_params=_cparams(2),
        name="proj",
    )(xn, w)


MERGE_TN = 1024


def _merge_body(ya_ref, yb_ref, yc_ref, ga_ref, gb_ref, gc_ref, wb_ref, wo_ref, h_ref, o_ref):
    j = pl.program_id(1)

    @pl.when(j == 0)
    def _():
        o_ref[...] = h_ref[...]

    m = _sigmoid(ga_ref[...]) * jnp.dot(ya_ref[...], wb_ref[0], preferred_element_type=F32)
    m += _sigmoid(gb_ref[...]) * jnp.dot(yb_ref[...], wb_ref[1], preferred_element_type=F32)
    m += _sigmoid(gc_ref[...]) * jnp.dot(yc_ref[...], wb_ref[2], preferred_element_type=F32)
    o_ref[...] += jnp.dot(m.astype(BF16), wo_ref[...], preferred_element_type=F32)


def _merge(ya, yb, yc, pg, wb, wo, h, layer):
    n, d = h.shape
    tm = _pick_tile(n, (544, 512, 256, 128, 64, 32, 16))
    nj = d // MERGE_TN
    yspec = pl.BlockSpec((tm, BRANCH_W), lambda i, j: (i, 0))
    return pl.pallas_call(
        _merge_body,
        grid=(n // tm, nj),
        in_specs=[
            yspec, yspec, yspec,
            pl.BlockSpec((tm, MERGE_TN), lambda i, j: (i, j)),
            pl.BlockSpec((tm, MERGE_TN), lambda i, j: (i, nj + j)),
            pl.BlockSpec((tm, MERGE_TN), lambda i, j: (i, 2 * nj + j)),
            pl.BlockSpec((None, N_BRANCH, BRANCH_W, MERGE_TN), lambda i, j: (layer, 0, 0, j)),
            pl.BlockSpec((None, MERGE_TN, d), lambda i, j: (layer, j, 0)),
            pl.BlockSpec((tm, d), lambda i, j: (i, 0)),
        ],
        out_specs=pl.BlockSpec((tm, d), lambda i, j: (i, 0)),
        out_shape=jax.ShapeDtypeStruct((n, d), F32),
        compiler_params=_cparams(2),
        name="merge",
    )(ya, yb, yc, pg, pg, pg, wb, wo, h)


def _layer_state_spec(layer, nseq, tail):
    zeros = (0,) * len(tail)
    return pl.BlockSpec((None, nseq) + tail, lambda b, c: (layer, b) + zeros)


def _stack_io(stack, depth, nstate, tail, n_args):
    shape = jax.ShapeDtypeStruct((depth, nstate) + tail, F32)
    if stack is None:
        return [], [], shape, {}
    return [pl.BlockSpec(memory_space=pl.ANY)], [stack], shape, {n_args: 1}


def _split3_dot(lmat, x):
    lb = lmat.astype(BF16)
    hi = x.astype(BF16)
    r1 = x - hi.astype(F32)
    mid = r1.astype(BF16)
    lo = (r1 - mid.astype(F32)).astype(BF16)
    return (jnp.dot(lb, hi, preferred_element_type=F32) + jnp.dot(lb, mid, preferred_element_type=F32)
            + jnp.dot(lb, lo, preferred_element_type=F32))


def _gdn_body(*refs, nseq, null_rows, has_state, has_stack):
    refs = list(refs)
    xbuf = refs.pop()
    so_ref = refs.pop()
    y_ref = refs.pop()
    if has_stack:
        refs.pop()
    if has_state:
        qkv_ref, z_ref, ba_ref, cs_ref, cw_ref, prm_ref, nw_ref, s0_ref = refs
    else:
        qkv_ref, z_ref, ba_ref, cw_ref, prm_ref, nw_ref = refs
        s0_ref = so_ref
    rows = qkv_ref.shape[0]
    seq_len = rows // nseq
    c = pl.program_id(1)
    heads = range(H_A)

    @pl.when(c == 0)
    def _():
        xbuf[0:SUBLANES, :] = jnp.zeros((SUBLANES, QKV_A), F32)
        if not has_state:
            so_ref[...] = jnp.zeros_like(so_ref)

    @pl.when(c > 0)
    def _():
        xbuf[0:SUBLANES, :] = xbuf[rows:rows + SUBLANES, :]

    xbuf[SUBLANES:SUBLANES + rows, :] = qkv_ref[...]
    if has_state:
        for s in range(nseq):
            r0 = SUBLANES + s * seq_len + null_rows - (CONV_W - 1)
            xbuf[r0:r0 + CONV_W - 1, :] = cs_ref[s]

    incl, strict, same = _chunk_masks(rows, seq_len)
    eye = _eye(rows)
    valid = _row_valid(rows, seq_len, null_rows, LANES) if null_rows else None
    sels = [_seq_rows(rows, seq_len, s, LANES) for s in range(nseq)] if nseq > 1 else [None]
    sels2 = [_seq_rows(2 * rows, seq_len, s, LANES) for s in range(nseq)] if nseq > 1 else [None]

    ba = ba_ref[...]
    beta_all = _sigmoid(ba)
    g_all = -jnp.exp(prm_ref[0:1, :]) * _softplus(ba + prm_ref[1:2, :])
    if null_rows:
        g_all = jnp.where(valid, g_all, 0.0)
    lstack = jnp.concatenate([incl.astype(F32), same.astype(F32)], axis=0)
    cs = _split3_dot(lstack, g_all)
    gc_all, gt_all = cs[0:rows], cs[rows:2 * rows]
    gc_t = gc_all.T

    def conv_cols(c0):
        acc = xbuf[SUBLANES - 3:SUBLANES - 3 + rows, c0:c0 + LANES] * cw_ref[0:1, c0:c0 + LANES]
        for j in range(1, CONV_W):
            acc += (xbuf[SUBLANES - 3 + j:SUBLANES - 3 + j + rows, c0:c0 + LANES]
                    * cw_ref[j:j + 1, c0:c0 + LANES])
        return _silu(acc)

    qs, ks, kbs, rhs, decays, wq_parts, kds, gtcs = [], [], [], [], [], [], [], []
    for h in heads:
        q = conv_cols(h * DK_A)
        k = conv_cols(H_A * DK_A + h * DK_A)
        v = conv_cols(2 * H_A * DK_A + h * DV_A)
        q = q * lax.rsqrt(jnp.sum(q * q, axis=-1, keepdims=True) + 1e-6) * DK_A ** -0.5
        k = k * lax.rsqrt(jnp.sum(k * k, axis=-1, keepdims=True) + 1e-6)
        if null_rows:
            q = jnp.where(valid, q, 0.0)
            k = jnp.where(valid, k, 0.0)
            v = jnp.where(valid, v, 0.0)
        beta = beta_all[:, h:h + 1]
        gcc = gc_all[:, H_A + h:H_A + h + 1]
        gtc = gt_all[:, H_A + h:H_A + h + 1]
        gcr = gc_t[H_A + h:H_A + h + 1, :]
        egc = jnp.exp(gcc)
        kb = k * beta
        qs.append(q)
        ks.append(k)
        kbs.append(kb)
        rhs.append(jnp.concatenate([v * beta, kb * egc], axis=1))
        decays.append(jnp.where(incl, jnp.exp(jnp.where(incl, gcc - gcr, 0.0)), 0.0))
        wq_parts.append(q * egc)
        kds.append(k * jnp.exp(gtc - gcc))
        gtcs.append(gtc)

    lms = [jnp.where(strict, _bdot_nt(kbs[h], ks[h]) * decays[h], 0.0) for h in heads]
    attns = [_bdot_nt(qs[h], ks[h]) * decays[h] for h in heads]
    tmats = _inv_unit_lower(lms, eye, seq_len)
    uws = [_bdot(tmats[h], rhs[h]) for h in heads]
    us, os_ = [], []
    for h in heads:
        u = uws[h][:, 0:DV_A]
        wq = jnp.concatenate([uws[h][:, DV_A:DV_A + DK_A], wq_parts[h]], axis=0)
        o = None
        for s in range(nseq):
            both = _bdot(_pick(sels2[s], wq), s0_ref[s, h])
            u = u - both[0:rows]
            o = both[rows:2 * rows] if o is None else o + both[rows:2 * rows]
        us.append(u)
        os_.append(o)
    os_ = [os_[h] + _bdot(attns[h], us[h]) for h in heads]
    for h in heads:
        for s in range(nseq):
            g_last = jnp.exp(gtcs[h][s * seq_len:s * seq_len + 1, :])
            so_ref[s, h] = g_last * s0_ref[s, h] + _bdot_tn(_pick(sels[s], kds[h]), us[h])
    for h in heads:
        z = z_ref[:, h * DV_A:(h + 1) * DV_A]
        y_ref[:, h * DV_A:(h + 1) * DV_A] = (_rmsnorm_rows(os_[h], nw_ref[...]) * _silu(z)).astype(y_ref.dtype)


def _gdn(p1, p3, conv_state, state, stack, layer, depth, conv_w, prm, norm_w, *,
         nbatch, nchunk, nseq, null_rows, row0):
    has_state = state is not None
    rows = CHUNK
    tail = (H_A, DK_A, DV_A)
    in_specs = [
        pl.BlockSpec((rows, QKV_A), lambda b, c: (row0 + b * nchunk + c, 0)),
        pl.BlockSpec((rows, BRANCH_W), lambda b, c: (row0 + b * nchunk + c, QKV_A // BRANCH_W)),
        pl.BlockSpec((rows, LANES), lambda b, c: (row0 + b * nchunk + c, N_CC // LANES)),
    ]
    args = [p1, p1, p3]
    if has_state:
        in_specs.append(_layer_state_spec(layer, nseq, (CONV_W - 1, QKV_A)))
        args.append(conv_state)
    in_specs += [
        pl.BlockSpec((CONV_W, QKV_A), lambda b, c: (0, 0)),
        pl.BlockSpec((SUBLANES, LANES), lambda b, c: (0, 0)),
        pl.BlockSpec((1, DV_A), lambda b, c: (0, 0)),
    ]
    args += [conv_w, prm, norm_w]
    if has_state:
        in_specs.append(_layer_state_spec(layer, nseq, tail))
        args.append(state)
    stack_specs, stack_args, stack_shape, aliases = _stack_io(stack, depth, nbatch * nseq, tail, len(args))
    return pl.pallas_call(
        functools.partial(_gdn_body, nseq=nseq, null_rows=null_rows, has_state=has_state,
                          has_stack=stack is not None),
        grid=(nbatch, nchunk),
        in_specs=in_specs + stack_specs,
        out_specs=[pl.BlockSpec((rows, BRANCH_W), lambda b, c: (b * nchunk + c, 0)),
                   _layer_state_spec(layer, nseq, tail)],
        out_shape=[jax.ShapeDtypeStruct((nbatch * nchunk * rows, BRANCH_W), BF16), stack_shape],
        input_output_aliases=aliases,
        scratch_shapes=[pltpu.VMEM((rows + SUBLANES, QKV_A), F32)],
        compiler_params=_cparams(2),
        name="gdn",
    )(*args, *stack_args)


def _gla_body(*refs, nseq, null_rows, has_state, has_stack):
    refs = list(refs)
    so_ref = refs.pop()
    y_ref = refs.pop()
    if has_stack:
        refs.pop()
    if has_state:
        q_ref, k_ref, v_ref, g_ref, gkl_ref, up_ref, bias_ref, nw_ref, s0_ref = refs
    else:
        q_ref, k_ref, v_ref, g_ref, gkl_ref, up_ref, bias_ref, nw_ref = refs
        s0_ref = so_ref
    rows = q_ref.shape[0]
    seq_len = rows // nseq
    c = pl.program_id(1)
    heads = range(H_B)
    if not has_state:
        @pl.when(c == 0)
        def _():
            so_ref[...] = jnp.zeros_like(so_ref)

    incl, _, same = _chunk_masks(rows, seq_len)
    ci = lax.broadcasted_iota(jnp.int32, (rows, rows), 1)
    first_half = same & ((ci & (seq_len - 1)) < seq_len // 2)
    wide = H_B * LANES
    x = _bdot(gkl_ref[...], up_ref[...]) + bias_ref[...]
    gk_all = (jnp.minimum(x, 0.0) - jnp.log(1.0 + jnp.exp(-jnp.abs(x)))) / GLA_NORMALIZER
    if null_rows:
        gk_all = jnp.where(_row_valid(rows, seq_len, null_rows, wide), gk_all, 0.0)
    lstack = jnp.concatenate([incl.astype(F32), same.astype(F32), first_half.astype(F32)], axis=0)
    cs = _split3_dot(lstack, gk_all)
    bc_all, bt_all, an_all = cs[0:rows], cs[rows:2 * rows], cs[2 * rows:3 * rows]
    bt_t = bt_all.T
    valid_k = _row_valid(rows, seq_len, null_rows, DK_B) if null_rows else None
    valid_v = _row_valid(rows, seq_len, null_rows, DV_B) if null_rows else None
    sels = [_seq_rows(rows, seq_len, s, DK_B) for s in range(nseq)] if nseq > 1 else [None]

    qis, kis, vs, q_ins, k_decs = [], [], [], [], []
    for h in heads:
        c0 = h * LANES
        q = q_ref[:, c0:c0 + DK_B] * DK_B ** -0.5
        k = k_ref[:, c0:c0 + DK_B]
        v = v_ref[:, h * DV_B:(h + 1) * DV_B]
        if null_rows:
            q = jnp.where(valid_k, q, 0.0)
            k = jnp.where(valid_k, k, 0.0)
            v = jnp.where(valid_v, v, 0.0)
        bc = bc_all[:, c0:c0 + DK_B]
        bt = bt_all[:, c0:c0 + DK_B]
        an = an_all[:, c0:c0 + DK_B]
        qis.append(q * jnp.exp(bc - an))
        kis.append(k * jnp.exp(an - bc))
        vs.append(v)
        q_ins.append(q * jnp.exp(bc))
        k_decs.append(k * jnp.exp(bt - bc))

    attns = [jnp.where(incl, _bdot_nt(qis[h], kis[h]), 0.0) for h in heads]
    os_ = [_bdot(attns[h], vs[h]) for h in heads]
    for h in heads:
        for s in range(nseq):
            os_[h] = os_[h] + _bdot(_pick(sels[s], q_ins[h]), s0_ref[s, h])
    for h in heads:
        c0 = h * LANES
        for s in range(nseq):
            a_last = jnp.exp(bt_t[c0:c0 + DK_B, s * seq_len:s * seq_len + 1])
            so_ref[s, h] = a_last * s0_ref[s, h] + _bdot_tn(_pick(sels[s], k_decs[h]), vs[h])
    for h in heads:
        g = g_ref[:, h * DV_B:(h + 1) * DV_B]
        y_ref[:, h * DV_B:(h + 1) * DV_B] = (_rmsnorm_rows(os_[h], nw_ref[...]) * _silu(g)).astype(y_ref.dtype)


def _gla(p2, p3, state, stack, layer, depth, up_w, bias, norm_w, *, nbatch, nchunk, nseq, null_rows, row0):
    has_state = state is not None
    rows = CHUNK
    wide = H_B * LANES
    tail = (H_B, DK_B, DV_B)

    def col(j):
        return pl.BlockSpec((rows, wide), lambda b, c: (row0 + b * nchunk + c, j))

    in_specs = [col(0), col(1), col(2), col(3),
                pl.BlockSpec((rows, LANES), lambda b, c: (row0 + b * nchunk + c, N_CC // LANES + 1)),
                pl.BlockSpec((LANES, wide), lambda b, c: (0, 0)),
                pl.BlockSpec((1, wide), lambda b, c: (0, 0)),
                pl.BlockSpec((1, DV_B), lambda b, c: (0, 0))]
    args = [p2, p2, p2, p2, p3, up_w, bias, norm_w]
    if has_state:
        in_specs.append(_layer_state_spec(layer, nseq, tail))
        args.append(state)
    stack_specs, stack_args, stack_shape, aliases = _stack_io(stack, depth, nbatch * nseq, tail, len(args))
    return pl.pallas_call(
        functools.partial(_gla_body, nseq=nseq, null_rows=null_rows, has_state=has_state,
                          has_stack=stack is not None),
        grid=(nbatch, nchunk),
        in_specs=in_specs + stack_specs,
        out_specs=[pl.BlockSpec((rows, BRANCH_W), lambda b, c: (b * nchunk + c, 0)),
                   _layer_state_spec(layer, nseq, tail)],
        out_shape=[jax.ShapeDtypeStruct((nbatch * nchunk * rows, BRANCH_W), BF16), stack_shape],
        input_output_aliases=aliases,
        compiler_params=_cparams(2),
        name="gla",
    )(*args, *stack_args)


def _rwkv_body(*refs, nseq, null_rows, has_state, has_stack):
    refs = list(refs)
    sp_ref = refs.pop()
    pbuf = refs.pop()
    so_ref = refs.pop()
    y_ref = refs.pop()
    if has_stack:
        refs.pop()
    if has_state:
        pc_ref, sh_ref, mu_ref, wwa_ref, gup_ref, vec_ref, s0_ref = refs
    else:
        pc_ref, mu_ref, wwa_ref, gup_ref, vec_ref = refs
    rows = pc_ref.shape[0]
    seq_len = rows // nseq
    c = pl.program_id(1)
    pairs = range(N_PAIR)

    @pl.when(c == 0)
    def _():
        pbuf[0:SUBLANES, :] = jnp.zeros((SUBLANES, N_CC), F32)
        if not has_state:
            sp_ref[...] = jnp.zeros_like(sp_ref)

    @pl.when(c > 0)
    def _():
        pbuf[0:SUBLANES, :] = pbuf[rows:rows + SUBLANES, :]

    pbuf[SUBLANES:SUBLANES + rows, :] = pc_ref[...]
    if has_state:
        for s in range(nseq):
            r0 = SUBLANES + s * seq_len + null_rows - 1
            pbuf[r0:r0 + 1, :] = sh_ref[s]

    def xc_cols(c0, w):
        cur = pbuf[SUBLANES:SUBLANES + rows, c0:c0 + w]
        prev = pbuf[SUBLANES - 1:SUBLANES - 1 + rows, c0:c0 + w]
        return cur + (prev - cur) * mu_ref[:, c0:c0 + w]

    incl, strict, same = _chunk_masks(rows, seq_len)
    eye = _eye(rows)
    lane = lax.broadcasted_iota(jnp.int32, (rows, LANES), 1)
    lo = lane < N_C
    lo2 = lax.broadcasted_iota(jnp.int32, (2 * rows, LANES), 1) < N_C
    top2 = lax.broadcasted_iota(jnp.int32, (2 * rows, LANES), 0) < rows
    valid = _row_valid(rows, seq_len, null_rows, LANES) if null_rows else None
    sels2 = [_seq_rows(2 * rows, seq_len, s, LANES) for s in range(nseq)] if nseq > 1 else [None]

    lora = xc_cols(3 * BRANCH_W, DECAY_LORA + AAA_LORA + GATE_LORA)
    wa_in = lora[:, 0:LANES]
    wa_in = jnp.where(lo, jnp.tanh(wa_in), wa_in)
    wa = _bdot(wa_in, wwa_ref[...])
    g_all = _bdot(_sigmoid(lora[:, LANES:2 * LANES]), gup_ref[...])
    w0 = vec_ref[0:1, :]
    a0 = vec_ref[1:2, :]
    w_log = -_softplus(-(w0 + wa[:, 0:BRANCH_W])) - 0.5
    lw_all = -jnp.exp(w_log)
    a_all = _sigmoid(a0 + wa[:, BRANCH_W:2 * BRANCH_W])
    if null_rows:
        lw_all = jnp.where(_row_valid(rows, seq_len, null_rows, BRANCH_W), lw_all, 0.0)
    lstack = jnp.concatenate([incl.astype(F32), same.astype(F32)], axis=0)
    cs = _split3_dot(lstack, lw_all)
    cum_all, tot_all = cs[0:rows], cs[rows:2 * rows]

    r2 = lax.broadcasted_iota(jnp.int32, (2 * N_C, LANES), 0)
    c2 = lax.broadcasted_iota(jnp.int32, (2 * N_C, LANES), 1)
    blockdiag = (r2 >= N_C) == (c2 >= N_C)

    def seg_sum(xv):
        s_lo = jnp.sum(jnp.where(lo, xv, 0.0), axis=-1, keepdims=True)
        s_hi = jnp.sum(jnp.where(lo, 0.0, xv), axis=-1, keepdims=True)
        return jnp.where(lo, s_lo, s_hi)

    def by_head(stacked):
        return jnp.where(lo, stacked[0:rows], stacked[rows:2 * rows])

    rs, k2s, vs, ar_stacks, b_ts, k_ts = [], [], [], [], [], []
    for p in pairs:
        c0 = p * LANES
        r = xc_cols(c0, LANES)
        k = xc_cols(BRANCH_W + c0, LANES)
        v = xc_cols(2 * BRANCH_W + c0, LANES)
        a_p = a_all[:, c0:c0 + LANES]
        kkr = k * vec_ref[2:3, c0:c0 + LANES]
        kk = kkr * lax.rsqrt(seg_sum(kkr * kkr) + 1e-6)
        k2 = k * (1.0 + (a_p - 1.0) * vec_ref[3:4, c0:c0 + LANES])
        av = -kk
        bv = kk * a_p
        if null_rows:
            r = jnp.where(valid, r, 0.0)
            k2 = jnp.where(valid, k2, 0.0)
            v = jnp.where(valid, v, 0.0)
            av = jnp.where(valid, av, 0.0)
            bv = jnp.where(valid, bv, 0.0)
        cum = cum_all[:, c0:c0 + LANES]
        lw = lw_all[:, c0:c0 + LANES]
        g_inv = jnp.exp(-cum)
        rs.append(r)
        k2s.append(k2)
        vs.append(v)
        ar_stacks.append(jnp.concatenate([av * jnp.exp(cum - lw), r * jnp.exp(cum)], axis=0))
        b_ts.append(bv * g_inv)
        k_ts.append(k2 * g_inv)

    a_abs, a_rbs, a_aks, a_rks = [], [], [], []
    for p in pairs:
        for hh in range(2):
            lhs = jnp.where(lo2 if hh == 0 else jnp.logical_not(lo2), ar_stacks[p], 0.0)
            mb = _bdot_nt(lhs, b_ts[p])
            mk = _bdot_nt(lhs, k_ts[p])
            a_abs.append(jnp.where(strict, mb[0:rows], 0.0))
            a_rbs.append(jnp.where(incl, mb[rows:2 * rows], 0.0))
            a_aks.append(jnp.where(strict, mk[0:rows], 0.0))
            a_rks.append(jnp.where(incl, mk[rows:2 * rows], 0.0))
    tmats = _inv_unit_lower([-m for m in a_abs], eye, seq_len)

    def pair_stack(mats, p):
        return jnp.concatenate([mats[2 * p], mats[2 * p + 1]], axis=0)

    states, xss = [], []
    for p in pairs:
        sps, xs = [], None
        for s in range(nseq):
            if has_state:
                sv = jnp.concatenate([s0_ref[s, 2 * p], s0_ref[s, 2 * p + 1]], axis=0)
                sp = jnp.where(blockdiag, jnp.concatenate([sv, sv], axis=1), 0.0)
            else:
                sp = sp_ref[p]
            both = _bdot_nt(_pick(sels2[s], ar_stacks[p]), sp)
            xs = both if xs is None else xs + both
            sps.append(sp)
        states.append(sps)
        xss.append(xs)
    yvs = [xss[p][0:rows] + by_head(_bdot(pair_stack(a_aks, p), vs[p])) for p in pairs]
    us = [by_head(_bdot(pair_stack(tmats, p), yvs[p])) for p in pairs]
    os_ = [xss[p][rows:2 * rows]
           + by_head(_bdot(pair_stack(a_rbs, p), us[p]) + _bdot(pair_stack(a_rks, p), vs[p]))
           for p in pairs]
    for p in pairs:
        c0 = p * LANES
        uv = jnp.concatenate([us[p], vs[p]], axis=0)
        bk = jnp.concatenate([b_ts[p], k_ts[p]], axis=0)
        tot = tot_all[:, c0:c0 + LANES]
        for s in range(nseq):
            upd = jnp.where(blockdiag, _bdot_tn(_pick(sels2[s], uv), bk), 0.0)
            sp_new = (states[p][s] + upd) * jnp.exp(tot[s * seq_len:s * seq_len + 1, :])
            if has_state:
                so_ref[s, 2 * p] = sp_new[0:N_C, 0:N_C]
                so_ref[s, 2 * p + 1] = pltpu.roll(sp_new[N_C:2 * N_C, :], N_C, 1)[:, 0:N_C]
            else:
                sp_ref[p] = sp_new

    for p in pairs:
        c0 = p * LANES
        o = os_[p]
        mean = seg_sum(o) * (1.0 / N_C)
        d = o - mean
        var = seg_sum(d * d) * (1.0 / N_C)
        on = d * lax.rsqrt(var + GN_EPS) * vec_ref[5:6, c0:c0 + LANES] + vec_ref[6:7, c0:c0 + LANES]
        bonus = seg_sum(rs[p] * k2s[p] * vec_ref[4:5, c0:c0 + LANES]) * vs[p]
        y_ref[:, c0:c0 + LANES] = ((on + bonus) * g_all[:, c0:c0 + LANES]).astype(y_ref.dtype)

    if not has_state:
        @pl.when(c == pl.num_programs(1) - 1)
        def _():
            for p in pairs:
                sp = sp_ref[p]
                so_ref[0, 2 * p] = sp[0:N_C, 0:N_C]
                so_ref[0, 2 * p + 1] = pltpu.roll(sp[N_C:2 * N_C, :], N_C, 1)[:, 0:N_C]


def _rwkv(p3, shift_state, state, stack, layer, depth, mu, wwa, gup, vec, *,
          nbatch, nchunk, nseq, null_rows, row0):
    has_state = state is not None
    rows = CHUNK
    tail = (H_C, N_C, N_C)
    in_specs = [pl.BlockSpec((rows, N_CC), lambda b, c: (row0 + b * nchunk + c, 0))]
    args = [p3]
    if has_state:
        in_specs.append(_layer_state_spec(layer, nseq, (1, N_CC)))
        args.append(shift_state)
    in_specs += [
        pl.BlockSpec((1, N_CC), lambda b, c: (0, 0)),
        pl.BlockSpec((LANES, 2 * BRANCH_W), lambda b, c: (0, 0)),
        pl.BlockSpec((GATE_LORA, BRANCH_W), lambda b, c: (0, 0)),
        pl.BlockSpec((SUBLANES, BRANCH_W), lambda b, c: (0, 0)),
    ]
    args += [mu, wwa, gup, vec]
    if has_state:
        in_specs.append(_layer_state_spec(layer, nseq, tail))
        args.append(state)
    stack_specs, stack_args, stack_shape, aliases = _stack_io(stack, depth, nbatch * nseq, tail, len(args))
    return pl.pallas_call(
        functools.partial(_rwkv_body, nseq=nseq, null_rows=null_rows, has_state=has_state,
                          has_stack=stack is not None),
        grid=(nbatch, nchunk),
        in_specs=in_specs + stack_specs,
        out_specs=[pl.BlockSpec((rows, BRANCH_W), lambda b, c: (b * nchunk + c, 0)),
                   _layer_state_spec(layer, nseq, tail)],
        out_shape=[jax.ShapeDtypeStruct((nbatch * nchunk * rows, BRANCH_W), BF16), stack_shape],
        input_output_aliases=aliases,
        scratch_shapes=[pltpu.VMEM((rows + SUBLANES, N_CC), F32),
                        pltpu.VMEM((N_PAIR, 2 * N_C, 2 * N_C), F32)],
        compiler_params=_cparams(2),
        name="rwkv",
    )(*args, *stack_args)


def _pad_cols(w, n):
    return jnp.pad(w, ((0, 0),) * (w.ndim - 1) + ((0, n - w.shape[-1]),))


def _pad_heads(w):
    z = jnp.zeros(w.shape[:-1] + (LANES - DK_B,), w.dtype)
    parts = []
    for h in range(H_B):
        parts += [w[..., h * DK_B:(h + 1) * DK_B], z]
    return jnp.concatenate(parts, axis=-1)


def _prep_in(w_in):
    w = w_in.astype(BF16)
    wa = w[..., :N_A]
    wb = w[..., N_A:N_A + N_B]
    wc = w[..., N_A + N_B:N_A + N_B + N_CC]
    wg = w[..., N_A + N_B + N_CC:]
    w1 = wa[..., :QKV_A + H_A * DV_A]
    kq = H_B * DK_B
    w2 = jnp.concatenate([_pad_heads(wb[..., :kq]), _pad_heads(wb[..., kq:2 * kq]),
                          wb[..., 2 * kq:2 * kq + 2 * H_B * DV_B]], axis=-1)
    misc = jnp.concatenate([_pad_cols(wa[..., QKV_A + H_A * DV_A:], LANES),
                            _pad_cols(wb[..., 2 * kq + 2 * H_B * DV_B:], LANES)], axis=-1)
    w3 = jnp.concatenate([wc, misc], axis=-1)
    return w1, w2, w3, wg


def _prep_ffn(w_up, w_down):
    wg = _pad_cols(w_up[..., :D_FF], D_FF_PAD).astype(BF16)
    wu = _pad_cols(w_up[..., D_FF:], D_FF_PAD).astype(BF16)
    wd = jnp.pad(w_down, ((0, 0), (0, D_FF_PAD - D_FF), (0, 0))).astype(BF16)
    return wg, wu, wd


def _pad_sample_rows(p, nb, ts):
    n = p.shape[-1]
    return jnp.pad(p.reshape(nb, ts, n), ((0, 0), (SAMPLE_ROWS - ts, 0), (0, 0))).reshape(nb * SAMPLE_ROWS, n)


def _tail_rows(p, nb, t, n_tail, width):
    return jnp.stack([p[(b + 1) * t - n_tail:(b + 1) * t, :width] for b in range(nb)])


def kernel(x_prompt, x_sample, state_gdn, state_gdn_conv, state_gla, state_rwkv, state_rwkv_shift, w_in, conv_a, a_log, dt_bias, gdn_norm, gla_gk_up, gla_gk_bias, gla_norm, rwkv_mu, rwkv_w0, rwkv_w_up, rwkv_a0, rwkv_a_up, rwkv_g_up, rwkv_k_k, rwkv_k_a, rwkv_r_k, rwkv_ln_w, rwkv_ln_b, w_branch, w_out, norm_ff1, w_ff1_up, w_ff1_down, norm_mix, norm_ff2, w_ff2_up, w_ff2_down, norm_final):
    bp, tp, d = x_prompt.shape
    bs, ts, _ = x_sample.shape
    depth = w_in.shape[0]
    n_p, n_s = bp * tp, bs * ts
    null_rows = SAMPLE_ROWS - ts
    assert tp % CHUNK == 0 and CONV_W - 1 <= null_rows < SAMPLE_ROWS
    seq_per_chunk = CHUNK // SAMPLE_ROWS
    assert bs % seq_per_chunk == 0
    nchunk_p = tp // CHUNK
    nstep_s = bs // seq_per_chunk
    prompt = dict(nbatch=bp, nchunk=nchunk_p, nseq=1, null_rows=0, row0=0)
    sample = dict(nbatch=nstep_s, nchunk=1, nseq=seq_per_chunk, null_rows=null_rows, row0=0)

    x = jnp.concatenate([x_prompt.reshape(n_p, d), x_sample.reshape(n_s, d)], axis=0)
    gdn_p = gdn_s = gla_p = gla_s = rwkv_p = rwkv_s = None
    conv_p, conv_s, shift_p, shift_s = [], [], [], []
    ff1 = _prep_ffn(w_ff1_up, w_ff1_down)
    ff2 = _prep_ffn(w_ff2_up, w_ff2_down)
    w1, w2, w3, wg = _prep_in(w_in)
    wbr = w_branch.astype(BF16)
    wout = w_out.astype(BF16)
    for l in range(depth):
        h = _ffn(x, norm_ff1[l], *ff1, l)

        hn = _norm_cast(h, norm_mix[l], BF16)
        p1 = _proj(hn, w1, l, w1.shape[2] // 2)
        p2 = _proj(hn, w2, l, w2.shape[2] // 2)
        p3 = _proj(hn, w3, l, w3.shape[2] // 2)
        pg = _proj(hn, wg, l, wg.shape[2] // 4)
        p1s, p2s, p3s = (_pad_sample_rows(p[n_p:], bs, ts) for p in (p1, p2, p3))

        prm = jnp.zeros((SUBLANES, LANES), F32)
        prm = prm.at[0, H_A:2 * H_A].set(a_log[l]).at[1, H_A:2 * H_A].set(dt_bias[l])
        gnorm = gdn_norm[l].reshape(1, DV_A)
        up_w = jnp.pad(gla_gk_up[l].reshape(GLA_RANK, H_B, DK_B),
                       ((0, LANES - GLA_RANK), (0, 0), (0, LANES - DK_B))).reshape(LANES, H_B * LANES).astype(BF16)
        gk_bias = jnp.pad(gla_gk_bias[l].reshape(H_B, DK_B), ((0, 0), (0, LANES - DK_B))).reshape(1, H_B * LANES)
        lnorm = gla_norm[l].reshape(1, DV_B)
        mu = rwkv_mu[l].reshape(1, N_CC)
        wwa = jnp.zeros((LANES, 2 * BRANCH_W), F32)
        wwa = wwa.at[0:DECAY_LORA, 0:BRANCH_W].set(rwkv_w_up[l])
        wwa = wwa.at[DECAY_LORA:DECAY_LORA + AAA_LORA, BRANCH_W:].set(rwkv_a_up[l]).astype(BF16)
        gup = rwkv_g_up[l].astype(BF16)
        vec = jnp.stack([rwkv_w0[l], rwkv_a0[l], rwkv_k_k[l], rwkv_k_a[l], rwkv_r_k[l],
                         rwkv_ln_w[l], rwkv_ln_b[l], jnp.zeros((BRANCH_W,), F32)], axis=0)

        ya_p, gdn_p = _gdn(p1, p3, None, None, gdn_p, l, depth, conv_a[l], prm, gnorm, **prompt)
        ya_s, gdn_s = _gdn(p1s, p3s, state_gdn_conv, state_gdn, gdn_s, l, depth, conv_a[l], prm, gnorm, **sample)
        yb_p, gla_p = _gla(p2, p3, None, gla_p, l, depth, up_w, gk_bias, lnorm, **prompt)
        yb_s, gla_s = _gla(p2s, p3s, state_gla, gla_s, l, depth, up_w, gk_bias, lnorm, **sample)
        yc_p, rwkv_p = _rwkv(p3, None, None, rwkv_p, l, depth, mu, wwa, gup, vec, **prompt)
        yc_s, rwkv_s = _rwkv(p3s, state_rwkv_shift, state_rwkv, rwkv_s, l, depth, mu, wwa, gup, vec, **sample)

        def join(y_p, y_s):
            y_s = y_s.reshape(bs, SAMPLE_ROWS, BRANCH_W)[:, null_rows:].reshape(n_s, BRANCH_W)
            return jnp.concatenate([y_p, y_s], axis=0)

        h = _merge(join(ya_p, ya_s), join(yb_p, yb_s), join(yc_p, yc_s), pg, wbr, wout, h, l)
        x = _ffn(h, norm_ff2[l], *ff2, l)

        conv_p.append(_tail_rows(p1, bp, tp, CONV_W - 1, QKV_A))
        conv_s.append(p1[n_p:].reshape(bs, ts, -1)[:, ts - (CONV_W - 1):, :QKV_A])
        shift_p.append(_tail_rows(p3, bp, tp, 1, N_CC))
        shift_s.append(p3[n_p:].reshape(bs, ts, -1)[:, ts - 1:, :N_CC])

    y = _norm_cast(x, norm_final, F32)
    return (y[:n_p].reshape(bp, tp, d), y[n_p:].reshape(bs, ts, d),
            gdn_p, gdn_s, jnp.stack(conv_p), jnp.stack(conv_s), gla_p, gla_s,
            rwkv_p, rwkv_s, jnp.stack(shift_p), jnp.stack(shift_s))
```

```python
import functools
import math

import jax
import jax.numpy as jnp
from jax import lax
from jax.experimental import pallas as pl
from jax.experimental.pallas import tpu as pltpu

F32 = jnp.float32
BF16 = jnp.bfloat16

D_MODEL = 2048
N_BRANCH = 3
BRANCH_W = 768
DK_A, DV_A, H_A, CONV_W = 128, 128, 6, 4
DK_B, DV_B, H_B, GLA_RANK, GLA_NORMALIZER = 64, 128, 6, 16, 16.0
N_C, H_C, DECAY_LORA, AAA_LORA, GATE_LORA = 64, 12, 64, 64, 128
GN_EPS = 64e-5
D_FF = 5504
NORM_EPS = 1e-6
QKV_A = 2 * H_A * DK_A + H_A * DV_A
N_A = QKV_A + H_A * DV_A + 2 * H_A
N_B = 2 * H_B * DK_B + 2 * H_B * DV_B + GLA_RANK
N_CC = 3 * BRANCH_W + DECAY_LORA + AAA_LORA + GATE_LORA
N_PAIR = H_C // 2

LANES = 128
SUBLANES = 8
CHUNK = 64
PROMPT_GROUPS = 4
SAMPLE_GROUPS = 2
SAMPLE_ROWS = 8
FF_TILE = 512
D_FF_PAD = ((D_FF + FF_TILE - 1) // FF_TILE) * FF_TILE
VMEM_LIMIT = 60 * 1024 * 1024


def _cparams(n_axes):
    return pltpu.CompilerParams(dimension_semantics=("arbitrary",) * n_axes,
                                vmem_limit_bytes=VMEM_LIMIT)


def _pick_tile(n, prefs):
    for t in prefs:
        if n % t == 0:
            return t
    raise ValueError(f"no tile for {n}")


def _bdot(a, b):
    return jnp.dot(a.astype(BF16), b.astype(BF16), preferred_element_type=F32)


def _bdot_nt(a, b):
    return lax.dot_general(a.astype(BF16), b.astype(BF16), (((1,), (1,)), ((), ())),
                           preferred_element_type=F32)


def _bdot_tn(a, b):
    return lax.dot_general(a.astype(BF16), b.astype(BF16), (((0,), (0,)), ((), ())),
                           preferred_element_type=F32)


def _sigmoid(x):
    return jax.nn.sigmoid(x)


def _silu(x):
    return x * _sigmoid(x)


def _softplus(x):
    return jnp.maximum(x, 0.0) + jnp.log(1.0 + jnp.exp(-jnp.abs(x)))


def _rmsnorm_rows(x, g):
    return x * lax.rsqrt(jnp.mean(x * x, axis=-1, keepdims=True) + NORM_EPS) * g


def _chunk_masks(rows, seq_len):
    sh = int(math.log2(seq_len))
    ri = lax.broadcasted_iota(jnp.int32, (rows, rows), 0)
    ci = lax.broadcasted_iota(jnp.int32, (rows, rows), 1)
    same = (ri >> sh) == (ci >> sh)
    return same & (ri >= ci), same & (ri > ci), same


def _eye(rows):
    return (lax.broadcasted_iota(jnp.int32, (rows, rows), 0)
            == lax.broadcasted_iota(jnp.int32, (rows, rows), 1)).astype(F32)


def _inv_unit_lower(lms, eye, nil):
    xs = [eye - lm for lm in lms]
    ps = list(lms)
    k = 2
    while k < nil:
        ps = [_bdot(p, p) for p in ps]
        xs = [x + _bdot(x, p) for x, p in zip(xs, ps)]
        k *= 2
    return xs


def _row_valid(rows, seq_len, null_rows, width):
    r = lax.broadcasted_iota(jnp.int32, (rows, width), 0)
    return (r & (seq_len - 1)) >= null_rows


def _seq_rows(rows, seq_len, s, width):
    r = lax.broadcasted_iota(jnp.int32, (rows, width), 0) & (CHUNK - 1)
    return (r >> int(math.log2(seq_len))) == s


def _pick(sel, xv):
    return xv if sel is None else jnp.where(sel, xv, 0.0)


def _ffn_body(x_ref, g_ref, wg_ref, wu_ref, wd_ref, o_ref, xn_ref):
    j = pl.program_id(1)

    @pl.when(j == 0)
    def _():
        xn_ref[...] = _rmsnorm_rows(x_ref[...], g_ref[...]).astype(BF16)
        o_ref[...] = jnp.zeros_like(o_ref)

    xn = xn_ref[...]
    gate = jnp.dot(xn, wg_ref[...], preferred_element_type=F32)
    up = jnp.dot(xn, wu_ref[...], preferred_element_type=F32)
    act = (_silu(gate) * up).astype(BF16)
    o_ref[...] += jnp.dot(act, wd_ref[...], preferred_element_type=F32)

    @pl.when(j == pl.num_programs(1) - 1)
    def _():
        o_ref[...] = x_ref[...] + 0.5 * o_ref[...]


def _ffn(x, g, wg, wu, wd, layer):
    n, d = x.shape
    tm = _pick_tile(n, (1088, 544, 512, 256, 128, 64, 32, 16, 8))
    nf = wg.shape[2] // FF_TILE
    return pl.pallas_call(
        _ffn_body,
        grid=(n // tm, nf),
        in_specs=[
            pl.BlockSpec((tm, d), lambda i, j: (i, 0)),
            pl.BlockSpec((1, d), lambda i, j: (0, 0)),
            pl.BlockSpec((None, d, FF_TILE), lambda i, j: (layer, 0, j)),
            pl.BlockSpec((None, d, FF_TILE), lambda i, j: (layer, 0, j)),
            pl.BlockSpec((None, FF_TILE, d), lambda i, j: (layer, j, 0)),
        ],
        out_specs=pl.BlockSpec((tm, d), lambda i, j: (i, 0)),
        out_shape=jax.ShapeDtypeStruct((n, d), F32),
        scratch_shapes=[pltpu.VMEM((tm, d), BF16)],
        compiler_params=_cparams(2),
        name="ffn",
    )(x, g.reshape(1, d), wg, wu, wd)


def _norm_cast_body(x_ref, g_ref, o_ref):
    o_ref[...] = _rmsnorm_rows(x_ref[...], g_ref[...]).astype(o_ref.dtype)


def _norm_cast(x, g, dtype, start=0, count=None):
    n, d = x.shape
    count = n - start if count is None else count
    tm = _pick_tile(math.gcd(start, count), (544, 512, 256, 128, 64, 32, 16, 8))
    first = start // tm
    return pl.pallas_call(
        _norm_cast_body,
        grid=(count // tm,),
        in_specs=[pl.BlockSpec((tm, d), lambda i: (first + i, 0)), pl.BlockSpec((1, d), lambda i: (0, 0))],
        out_specs=pl.BlockSpec((tm, d), lambda i: (i, 0)),
        out_shape=jax.ShapeDtypeStruct((count, d), dtype),
        compiler_params=_cparams(1),
        name="norm",
    )(x, g.reshape(1, d))


def _proj_body(x_ref, w_ref, o_ref):
    o_ref[...] = jnp.dot(x_ref[...], w_ref[...], preferred_element_type=F32).astype(o_ref.dtype)


def _proj(xn, w, layer, tn, out_dtype=F32):
    n, d = xn.shape
    nn = w.shape[2]
    tm = _pick_tile(n, (1088, 544, 512, 256, 128, 64, 32, 16))
    return pl.pallas_call(
        _proj_body,
        grid=(nn // tn, n // tm),
        in_specs=[
            pl.BlockSpec((tm, d), lambda j, i: (i, 0)),
            pl.BlockSpec((None, d, tn), lambda j, i: (layer, 0, j)),
        ],
        out_specs=pl.BlockSpec((tm, tn), lambda j, i: (i, j)),
        out_shape=jax.ShapeDtypeStruct((n, nn), out_dtype),
        compiler_params=_cparams(2),
        name="proj",
    )(xn, w)


MERGE_TN = 1024


def _merge_body(ya_ref, yb_ref, yc_ref, ga_ref, gb_ref, gc_ref, wb_ref, wo_ref, h_ref, o_ref):
    j = pl.program_id(1)

    @pl.when(j == 0)
    def _():
        o_ref[...] = h_ref[...]

    m = _sigmoid(ga_ref[...].astype(F32)) * jnp.dot(ya_ref[...], wb_ref[0], preferred_element_type=F32)
    m += _sigmoid(gb_ref[...].astype(F32)) * jnp.dot(yb_ref[...], wb_ref[1], preferred_element_type=F32)
    m += _sigmoid(gc_ref[...].astype(F32)) * jnp.dot(yc_ref[...], wb_ref[2], preferred_element_type=F32)
    o_ref[...] += jnp.dot(m.astype(BF16), wo_ref[...], preferred_element_type=F32)


def _merge(ya, yb, yc, pg, wb, wo, h, layer):
    n, d = h.shape
    tm = _pick_tile(n, (544, 512, 256, 128, 64, 32, 16))
    nj = d // MERGE_TN
    yspec = pl.BlockSpec((tm, BRANCH_W), lambda i, j: (i, 0))
    return pl.pallas_call(
        _merge_body,
        grid=(n // tm, nj),
        in_specs=[
            yspec, yspec, yspec,
            pl.BlockSpec((tm, MERGE_TN), lambda i, j: (i, j)),
            pl.BlockSpec((tm, MERGE_TN), lambda i, j: (i, nj + j)),
            pl.BlockSpec((tm, MERGE_TN), lambda i, j: (i, 2 * nj + j)),
            pl.BlockSpec((None, N_BRANCH, BRANCH_W, MERGE_TN), lambda i, j: (layer, 0, 0, j)),
            pl.BlockSpec((None, MERGE_TN, d), lambda i, j: (layer, j, 0)),
            pl.BlockSpec((tm, d), lambda i, j: (i, 0)),
        ],
        out_specs=pl.BlockSpec((tm, d), lambda i, j: (i, 0)),
        out_shape=jax.ShapeDtypeStruct((n, d), F32),
        compiler_params=_cparams(2),
        name="merge",
    )(ya, yb, yc, pg, pg, pg, wb, wo, h)


def _layer_state_spec(layer, nseq, tail):
    zeros = (0,) * len(tail)
    return pl.BlockSpec((None, nseq) + tail, lambda b, c: (layer, b) + zeros)


def _stack_io(stack, depth, nstate, tail, n_args):
    shape = jax.ShapeDtypeStruct((depth, nstate) + tail, F32)
    if stack is None:
        return [], [], shape, {}
    return [pl.BlockSpec(memory_space=pl.ANY)], [stack], shape, {n_args: 1}


def _split3_dot(lmat, x):
    lb = lmat.astype(BF16)
    hi = x.astype(BF16)
    r1 = x - hi.astype(F32)
    mid = r1.astype(BF16)
    lo = (r1 - mid.astype(F32)).astype(BF16)
    return (jnp.dot(lb, hi, preferred_element_type=F32) + jnp.dot(lb, mid, preferred_element_type=F32)
            + jnp.dot(lb, lo, preferred_element_type=F32))


def _gdn_body(*refs, ngroup, nseq, null_rows, has_state, has_stack):
    refs = list(refs)
    xbuf = refs.pop()
    so_ref = refs.pop()
    y_ref = refs.pop()
    if has_stack:
        refs.pop()
    row_refs = [refs[3 * g:3 * g + 3] for g in range(ngroup)]
    refs = refs[3 * ngroup:]
    if has_state:
        cs_ref, cw_ref, prm_ref, nw_ref, s0_ref = refs
    else:
        cw_ref, prm_ref, nw_ref = refs
        s0_ref = so_ref
    rows = CHUNK
    seq_len = rows // nseq
    c = pl.program_id(1)
    chains = [(g, h) for g in range(ngroup) for h in range(H_A)]

    @pl.when(c == 0)
    def _():
        for g in range(ngroup):
            xbuf[g, 0:SUBLANES, :] = jnp.zeros((SUBLANES, QKV_A), F32)
        if not has_state:
            so_ref[...] = jnp.zeros_like(so_ref)

    @pl.when(c > 0)
    def _():
        for g in range(ngroup):
            xbuf[g, 0:SUBLANES, :] = xbuf[g, rows:rows + SUBLANES, :]

    for g in range(ngroup):
        xbuf[g, SUBLANES:SUBLANES + rows, :] = row_refs[g][0][...]
        if has_state:
            for s in range(nseq):
                r0 = SUBLANES + s * seq_len + null_rows - (CONV_W - 1)
                xbuf[g, r0:r0 + CONV_W - 1, :] = cs_ref[g * nseq + s]

    incl, strict, same = _chunk_masks(rows, seq_len)
    eye = _eye(rows)
    valid = _row_valid(rows, seq_len, null_rows, LANES) if null_rows else None
    sels = [_seq_rows(rows, seq_len, s, LANES) for s in range(nseq)] if nseq > 1 else [None]
    sels2 = [_seq_rows(2 * rows, seq_len, s, LANES) for s in range(nseq)] if nseq > 1 else [None]
    lstack = jnp.concatenate([incl.astype(F32), same.astype(F32)], axis=0)

    def conv_cols(g, c0):
        acc = xbuf[g, SUBLANES - 3:SUBLANES - 3 + rows, c0:c0 + LANES] * cw_ref[0:1, c0:c0 + LANES]
        for j in range(1, CONV_W):
            acc += (xbuf[g, SUBLANES - 3 + j:SUBLANES - 3 + j + rows, c0:c0 + LANES]
                    * cw_ref[j:j + 1, c0:c0 + LANES])
        return _silu(acc)

    qs, ks, kbs, rhs, decays, wq_parts, kds, gtcs = [], [], [], [], [], [], [], []
    for g in range(ngroup):
        ba = row_refs[g][2][...]
        beta_all = _sigmoid(ba)
        g_all = -jnp.exp(prm_ref[0:1, :]) * _softplus(ba + prm_ref[1:2, :])
        if null_rows:
            g_all = jnp.where(valid, g_all, 0.0)
        cs = _split3_dot(lstack, g_all)
        gc_all, gt_all = cs[0:rows], cs[rows:2 * rows]
        gc_t = gc_all.T
        for h in range(H_A):
            q = conv_cols(g, h * DK_A)
            k = conv_cols(g, H_A * DK_A + h * DK_A)
            v = conv_cols(g, 2 * H_A * DK_A + h * DV_A)
            q = q * lax.rsqrt(jnp.sum(q * q, axis=-1, keepdims=True) + 1e-6) * DK_A ** -0.5
            k = k * lax.rsqrt(jnp.sum(k * k, axis=-1, keepdims=True) + 1e-6)
            if null_rows:
                q = jnp.where(valid, q, 0.0)
                k = jnp.where(valid, k, 0.0)
                v = jnp.where(valid, v, 0.0)
            beta = beta_all[:, h:h + 1]
            gcc = gc_all[:, H_A + h:H_A + h + 1]
            gtc = gt_all[:, H_A + h:H_A + h + 1]
            gcr = gc_t[H_A + h:H_A + h + 1, :]
            egc = jnp.exp(gcc)
            kb = k * beta
            qs.append(q)
            ks.append(k)
            kbs.append(kb)
            rhs.append(jnp.concatenate([v * beta, kb * egc], axis=1))
            decays.append(jnp.where(incl, jnp.exp(jnp.where(incl, gcc - gcr, 0.0)), 0.0))
            wq_parts.append(q * egc)
            kds.append(k * jnp.exp(gtc - gcc))
            gtcs.append(gtc)

    n = range(len(chains))
    lms = [jnp.where(strict, _bdot_nt(kbs[i], ks[i]) * decays[i], 0.0) for i in n]
    attns = [_bdot_nt(qs[i], ks[i]) * decays[i] for i in n]
    tmats = _inv_unit_lower(lms, eye, seq_len)
    uws = [_bdot(tmats[i], rhs[i]) for i in n]
    us, os_ = [], []
    for i, (g, h) in enumerate(chains):
        u = uws[i][:, 0:DV_A]
        wq = jnp.concatenate([uws[i][:, DV_A:DV_A + DK_A], wq_parts[i]], axis=0)
        o = None
        for s in range(nseq):
            both = _bdot(_pick(sels2[s], wq), s0_ref[g * nseq + s, h])
            u = u - both[0:rows]
            o = both[rows:2 * rows] if o is None else o + both[rows:2 * rows]
        us.append(u)
        os_.append(o)
    os_ = [os_[i] + _bdot(attns[i], us[i]) for i in n]
    for i, (g, h) in enumerate(chains):
        for s in range(nseq):
            g_last = jnp.exp(gtcs[i][s * seq_len:s * seq_len + 1, :])
            so_ref[g * nseq + s, h] = (g_last * s0_ref[g * nseq + s, h]
                                       + _bdot_tn(_pick(sels[s], kds[i]), us[i]))
    for i, (g, h) in enumerate(chains):
        z = row_refs[g][1][:, h * DV_A:(h + 1) * DV_A]
        y_ref[g, :, h * DV_A:(h + 1) * DV_A] = (_rmsnorm_rows(os_[i], nw_ref[...]) * _silu(z)).astype(y_ref.dtype)


def _row_spec(width, col, g, ngroup, nchunk):
    return pl.BlockSpec((CHUNK, width), lambda b, c: ((b * ngroup + g) * nchunk + c, col))


def _y_io(nbatch, nchunk, ngroup):
    return (pl.BlockSpec((ngroup, CHUNK, BRANCH_W), lambda b, c: (b, c, 0)),
            jax.ShapeDtypeStruct((nbatch, nchunk * CHUNK, BRANCH_W), BF16))


def _gdn(p1, p3, conv_state, state, stack, layer, depth, conv_w, prm, norm_w, *,
         nbatch, nchunk, nseq, null_rows, ngroup):
    has_state = state is not None
    tail = (H_A, DK_A, DV_A)
    in_specs, args = [], []
    for g in range(ngroup):
        in_specs += [_row_spec(QKV_A, 0, g, ngroup, nchunk),
                     _row_spec(BRANCH_W, QKV_A // BRANCH_W, g, ngroup, nchunk),
                     _row_spec(LANES, N_CC // LANES, g, ngroup, nchunk)]
        args += [p1, p1, p3]
    if has_state:
        in_specs.append(_layer_state_spec(layer, ngroup * nseq, (CONV_W - 1, QKV_A)))
        args.append(conv_state)
    in_specs += [
        pl.BlockSpec((CONV_W, QKV_A), lambda b, c: (0, 0)),
        pl.BlockSpec((SUBLANES, LANES), lambda b, c: (0, 0)),
        pl.BlockSpec((1, DV_A), lambda b, c: (0, 0)),
    ]
    args += [conv_w, prm, norm_w]
    if has_state:
        in_specs.append(_layer_state_spec(layer, ngroup * nseq, tail))
        args.append(state)
    stack_specs, stack_args, stack_shape, aliases = _stack_io(stack, depth, nbatch * nseq, tail, len(args))
    y_spec, y_shape = _y_io(nbatch, nchunk, ngroup)
    return pl.pallas_call(
        functools.partial(_gdn_body, ngroup=ngroup, nseq=nseq, null_rows=null_rows, has_state=has_state,
                          has_stack=stack is not None),
        grid=(nbatch // ngroup, nchunk),
        in_specs=in_specs + stack_specs,
        out_specs=[y_spec, _layer_state_spec(layer, ngroup * nseq, tail)],
        out_shape=[y_shape, stack_shape],
        input_output_aliases=aliases,
        scratch_shapes=[pltpu.VMEM((ngroup, CHUNK + SUBLANES, QKV_A), F32)],
        compiler_params=_cparams(2),
        name="gdn",
    )(*args, *stack_args)


def _gla_body(*refs, ngroup, nseq, null_rows, has_state, has_stack):
    refs = list(refs)
    so_ref = refs.pop()
    y_ref = refs.pop()
    if has_stack:
        refs.pop()
    row_refs = [refs[5 * g:5 * g + 5] for g in range(ngroup)]
    refs = refs[5 * ngroup:]
    if has_state:
        up_ref, bias_ref, nw_ref, s0_ref = refs
    else:
        up_ref, bias_ref, nw_ref = refs
        s0_ref = so_ref
    rows = CHUNK
    seq_len = rows // nseq
    c = pl.program_id(1)
    chains = [(g, h) for g in range(ngroup) for h in range(H_B)]
    if not has_state:
        @pl.when(c == 0)
        def _():
            so_ref[...] = jnp.zeros_like(so_ref)

    incl, _, same = _chunk_masks(rows, seq_len)
    ci = lax.broadcasted_iota(jnp.int32, (rows, rows), 1)
    first_half = same & ((ci & (seq_len - 1)) < seq_len // 2)
    wide = H_B * LANES
    lstack = jnp.concatenate([incl.astype(F32), same.astype(F32), first_half.astype(F32)], axis=0)
    valid_w = _row_valid(rows, seq_len, null_rows, wide) if null_rows else None
    valid_k = _row_valid(rows, seq_len, null_rows, DK_B) if null_rows else None
    valid_v = _row_valid(rows, seq_len, null_rows, DV_B) if null_rows else None
    sels = [_seq_rows(rows, seq_len, s, DK_B) for s in range(nseq)] if nseq > 1 else [None]

    qis, kis, vs, q_ins, k_decs, a_lasts = [], [], [], [], [], []
    for g in range(ngroup):
        q_ref, k_ref, v_ref, _, gkl_ref = row_refs[g]
        x = _bdot(gkl_ref[...], up_ref[...]) + bias_ref[...]
        gk_all = (jnp.minimum(x, 0.0) - jnp.log(1.0 + jnp.exp(-jnp.abs(x)))) / GLA_NORMALIZER
        if null_rows:
            gk_all = jnp.where(valid_w, gk_all, 0.0)
        cs = _split3_dot(lstack, gk_all)
        bc_all, bt_all, an_all = cs[0:rows], cs[rows:2 * rows], cs[2 * rows:3 * rows]
        bt_t = bt_all.T
        for h in range(H_B):
            c0 = h * LANES
            q = q_ref[:, c0:c0 + DK_B] * DK_B ** -0.5
            k = k_ref[:, c0:c0 + DK_B]
            v = v_ref[:, h * DV_B:(h + 1) * DV_B]
            if null_rows:
                q = jnp.where(valid_k, q, 0.0)
                k = jnp.where(valid_k, k, 0.0)
                v = jnp.where(valid_v, v, 0.0)
            bc = bc_all[:, c0:c0 + DK_B]
            bt = bt_all[:, c0:c0 + DK_B]
            an = an_all[:, c0:c0 + DK_B]
            qis.append(q * jnp.exp(bc - an))
            kis.append(k * jnp.exp(an - bc))
            vs.append(v)
            q_ins.append(q * jnp.exp(bc))
            k_decs.append(k * jnp.exp(bt - bc))
            a_lasts.append([jnp.exp(bt_t[c0:c0 + DK_B, s * seq_len:s * seq_len + 1])
                            for s in range(nseq)])

    n = range(len(chains))
    attns = [jnp.where(incl, _bdot_nt(qis[i], kis[i]), 0.0) for i in n]
    os_ = [_bdot(attns[i], vs[i]) for i in n]
    for i, (g, h) in enumerate(chains):
        for s in range(nseq):
            os_[i] = os_[i] + _bdot(_pick(sels[s], q_ins[i]), s0_ref[g * nseq + s, h])
    for i, (g, h) in enumerate(chains):
        for s in range(nseq):
            so_ref[g * nseq + s, h] = (a_lasts[i][s] * s0_ref[g * nseq + s, h]
                                       + _bdot_tn(_pick(sels[s], k_decs[i]), vs[i]))
    for i, (g, h) in enumerate(chains):
        gate = row_refs[g][3][:, h * DV_B:(h + 1) * DV_B]
        y_ref[g, :, h * DV_B:(h + 1) * DV_B] = (_rmsnorm_rows(os_[i], nw_ref[...]) * _silu(gate)).astype(y_ref.dtype)


def _gla(p2, p3, state, stack, layer, depth, up_w, bias, norm_w, *, nbatch, nchunk, nseq, null_rows, ngroup):
    has_state = state is not None
    wide = H_B * LANES
    tail = (H_B, DK_B, DV_B)
    in_specs, args = [], []
    for g in range(ngroup):
        in_specs += [_row_spec(wide, j, g, ngroup, nchunk) for j in range(4)]
        in_specs.append(_row_spec(LANES, N_CC // LANES + 1, g, ngroup, nchunk))
        args += [p2, p2, p2, p2, p3]
    in_specs += [pl.BlockSpec((LANES, wide), lambda b, c: (0, 0)),
                 pl.BlockSpec((1, wide), lambda b, c: (0, 0)),
                 pl.BlockSpec((1, DV_B), lambda b, c: (0, 0))]
    args += [up_w, bias, norm_w]
    if has_state:
        in_specs.append(_layer_state_spec(layer, ngroup * nseq, tail))
        args.append(state)
    stack_specs, stack_args, stack_shape, aliases = _stack_io(stack, depth, nbatch * nseq, tail, len(args))
    y_spec, y_shape = _y_io(nbatch, nchunk, ngroup)
    return pl.pallas_call(
        functools.partial(_gla_body, ngroup=ngroup, nseq=nseq, null_rows=null_rows, has_state=has_state,
                          has_stack=stack is not None),
        grid=(nbatch // ngroup, nchunk),
        in_specs=in_specs + stack_specs,
        out_specs=[y_spec, _layer_state_spec(layer, ngroup * nseq, tail)],
        out_shape=[y_shape, stack_shape],
        input_output_aliases=aliases,
        compiler_params=_cparams(2),
        name="gla",
    )(*args, *stack_args)


def _rwkv_body(*refs, ngroup, nseq, null_rows, has_state, has_stack):
    refs = list(refs)
    sp_ref = refs.pop()
    pbuf = refs.pop()
    so_ref = refs.pop()
    y_ref = refs.pop()
    if has_stack:
        refs.pop()
    pc_refs = refs[:ngroup]
    refs = refs[ngroup:]
    if has_state:
        sh_ref, mu_ref, wwa_ref, gup_ref, vec_ref, s0_ref = refs
    else:
        mu_ref, wwa_ref, gup_ref, vec_ref = refs
    rows = CHUNK
    seq_len = rows // nseq
    c = pl.program_id(1)
    chains = [(g, p) for g in range(ngroup) for p in range(N_PAIR)]
    pairs = range(len(chains))

    @pl.when(c == 0)
    def _():
        for g in range(ngroup):
            pbuf[g, 0:SUBLANES, :] = jnp.zeros((SUBLANES, N_CC), F32)
        if not has_state:
            sp_ref[...] = jnp.zeros_like(sp_ref)

    @pl.when(c > 0)
    def _():
        for g in range(ngroup):
            pbuf[g, 0:SUBLANES, :] = pbuf[g, rows:rows + SUBLANES, :]

    for g in range(ngroup):
        pbuf[g, SUBLANES:SUBLANES + rows, :] = pc_refs[g][...]
        if has_state:
            for s in range(nseq):
                r0 = SUBLANES + s * seq_len + null_rows - 1
                pbuf[g, r0:r0 + 1, :] = sh_ref[g * nseq + s]

    def xc_cols(g, c0, w):
        cur = pbuf[g, SUBLANES:SUBLANES + rows, c0:c0 + w]
        prev = pbuf[g, SUBLANES - 1:SUBLANES - 1 + rows, c0:c0 + w]
        return cur + (prev - cur) * mu_ref[:, c0:c0 + w]

    incl, strict, same = _chunk_masks(rows, seq_len)
    eye = _eye(rows)
    lane = lax.broadcasted_iota(jnp.int32, (rows, LANES), 1)
    lo = lane < N_C
    lo2 = lax.broadcasted_iota(jnp.int32, (2 * rows, LANES), 1) < N_C
    valid = _row_valid(rows, seq_len, null_rows, LANES) if null_rows else None
    valid_w = _row_valid(rows, seq_len, null_rows, BRANCH_W) if null_rows else None
    sels2 = [_seq_rows(2 * rows, seq_len, s, LANES) for s in range(nseq)] if nseq > 1 else [None]
    lstack = jnp.concatenate([incl.astype(F32), same.astype(F32)], axis=0)
    w0 = vec_ref[0:1, :]
    a0 = vec_ref[1:2, :]

    r2 = lax.broadcasted_iota(jnp.int32, (2 * N_C, LANES), 0)
    c2 = lax.broadcasted_iota(jnp.int32, (2 * N_C, LANES), 1)
    blockdiag = (r2 >= N_C) == (c2 >= N_C)

    def seg_sum(xv):
        s_lo = jnp.sum(jnp.where(lo, xv, 0.0), axis=-1, keepdims=True)
        s_hi = jnp.sum(jnp.where(lo, 0.0, xv), axis=-1, keepdims=True)
        return jnp.where(lo, s_lo, s_hi)

    def by_head(stacked):
        return jnp.where(lo, stacked[0:rows], stacked[rows:2 * rows])

    rs, k2s, vs, ar_stacks, b_ts, k_ts, tots, gates = [], [], [], [], [], [], [], []
    for g in range(ngroup):
        lora = xc_cols(g, 3 * BRANCH_W, DECAY_LORA + AAA_LORA + GATE_LORA)
        wa_in = lora[:, 0:LANES]
        wa_in = jnp.where(lo, jnp.tanh(wa_in), wa_in)
        wa = _bdot(wa_in, wwa_ref[...])
        g_all = _bdot(_sigmoid(lora[:, LANES:2 * LANES]), gup_ref[...])
        w_log = -_softplus(-(w0 + wa[:, 0:BRANCH_W])) - 0.5
        lw_all = -jnp.exp(w_log)
        a_all = _sigmoid(a0 + wa[:, BRANCH_W:2 * BRANCH_W])
        if null_rows:
            lw_all = jnp.where(valid_w, lw_all, 0.0)
        cs = _split3_dot(lstack, lw_all)
        cum_all, tot_all = cs[0:rows], cs[rows:2 * rows]
        for p in range(N_PAIR):
            c0 = p * LANES
            r = xc_cols(g, c0, LANES)
            k = xc_cols(g, BRANCH_W + c0, LANES)
            v = xc_cols(g, 2 * BRANCH_W + c0, LANES)
            a_p = a_all[:, c0:c0 + LANES]
            kkr = k * vec_ref[2:3, c0:c0 + LANES]
            kk = kkr * lax.rsqrt(seg_sum(kkr * kkr) + 1e-6)
            k2 = k * (1.0 + (a_p - 1.0) * vec_ref[3:4, c0:c0 + LANES])
            av = -kk
            bv = kk * a_p
            if null_rows:
                r = jnp.where(valid, r, 0.0)
                k2 = jnp.where(valid, k2, 0.0)
                v = jnp.where(valid, v, 0.0)
                av = jnp.where(valid, av, 0.0)
                bv = jnp.where(valid, bv, 0.0)
            cum = cum_all[:, c0:c0 + LANES]
            lw = lw_all[:, c0:c0 + LANES]
            g_inv = jnp.exp(-cum)
            rs.append(r)
            k2s.append(k2)
            vs.append(v)
            ar_stacks.append(jnp.concatenate([av * jnp.exp(cum - lw), r * jnp.exp(cum)], axis=0))
            b_ts.append(bv * g_inv)
            k_ts.append(k2 * g_inv)
            tots.append(tot_all[:, c0:c0 + LANES])
            gates.append(g_all[:, c0:c0 + LANES])

    a_abs, a_rbs, a_aks, a_rks = [], [], [], []
    for p in pairs:
        for hh in range(2):
            lhs = jnp.where(lo2 if hh == 0 else jnp.logical_not(lo2), ar_stacks[p], 0.0)
            mb = _bdot_nt(lhs, b_ts[p])
            mk = _bdot_nt(lhs, k_ts[p])
            a_abs.append(jnp.where(strict, mb[0:rows], 0.0))
            a_rbs.append(jnp.where(incl, mb[rows:2 * rows], 0.0))
            a_aks.append(jnp.where(strict, mk[0:rows], 0.0))
            a_rks.append(jnp.where(incl, mk[rows:2 * rows], 0.0))
    tmats = _inv_unit_lower([-m for m in a_abs], eye, seq_len)

    def pair_stack(mats, p):
        return jnp.concatenate([mats[2 * p], mats[2 * p + 1]], axis=0)

    states, xss = [], []
    for i, (g, p) in enumerate(chains):
        sps, xs = [], None
        for s in range(nseq):
            if has_state:
                sq = g * nseq + s
                sv = jnp.concatenate([s0_ref[sq, 2 * p], s0_ref[sq, 2 * p + 1]], axis=0)
                sp = jnp.where(blockdiag, jnp.concatenate([sv, sv], axis=1), 0.0)
            else:
                sp = sp_ref[i]
            both = _bdot_nt(_pick(sels2[s], ar_stacks[i]), sp)
            xs = both if xs is None else xs + both
            sps.append(sp)
        states.append(sps)
        xss.append(xs)
    yvs = [xss[i][0:rows] + by_head(_bdot(pair_stack(a_aks, i), vs[i])) for i in pairs]
    us = [by_head(_bdot(pair_stack(tmats, i), yvs[i])) for i in pairs]
    os_ = [xss[i][rows:2 * rows]
           + by_head(_bdot(pair_stack(a_rbs, i), us[i]) + _bdot(pair_stack(a_rks, i), vs[i]))
           for i in pairs]
    for i, (g, p) in enumerate(chains):
        uv = jnp.concatenate([us[i], vs[i]], axis=0)
        bk = jnp.concatenate([b_ts[i], k_ts[i]], axis=0)
        for s in range(nseq):
            upd = jnp.where(blockdiag, _bdot_tn(_pick(sels2[s], uv), bk), 0.0)
            sp_new = (states[i][s] + upd) * jnp.exp(tots[i][s * seq_len:s * seq_len + 1, :])
            if has_state:
                sq = g * nseq + s
                so_ref[sq, 2 * p] = sp_new[0:N_C, 0:N_C]
                so_ref[sq, 2 * p + 1] = pltpu.roll(sp_new[N_C:2 * N_C, :], N_C, 1)[:, 0:N_C]
            else:
                sp_ref[i] = sp_new

    for i, (g, p) in enumerate(chains):
        c0 = p * LANES
        o = os_[i]
        mean = seg_sum(o) * (1.0 / N_C)
        d = o - mean
        var = seg_sum(d * d) * (1.0 / N_C)
        on = d * lax.rsqrt(var + GN_EPS) * vec_ref[5:6, c0:c0 + LANES] + vec_ref[6:7, c0:c0 + LANES]
        bonus = seg_sum(rs[i] * k2s[i] * vec_ref[4:5, c0:c0 + LANES]) * vs[i]
        y_ref[g, :, c0:c0 + LANES] = ((on + bonus) * gates[i]).astype(y_ref.dtype)

    if not has_state:
        @pl.when(c == pl.num_programs(1) - 1)
        def _():
            for i, (g, p) in enumerate(chains):
                sp = sp_ref[i]
                so_ref[g, 2 * p] = sp[0:N_C, 0:N_C]
                so_ref[g, 2 * p + 1] = pltpu.roll(sp[N_C:2 * N_C, :], N_C, 1)[:, 0:N_C]


def _rwkv(p3, shift_state, state, stack, layer, depth, mu, wwa, gup, vec, *,
          nbatch, nchunk, nseq, null_rows, ngroup):
    has_state = state is not None
    tail = (H_C, N_C, N_C)
    in_specs = [_row_spec(N_CC, 0, g, ngroup, nchunk) for g in range(ngroup)]
    args = [p3] * ngroup
    if has_state:
        in_specs.append(_layer_state_spec(layer, ngroup * nseq, (1, N_CC)))
        args.append(shift_state)
    in_specs += [
        pl.BlockSpec((1, N_CC), lambda b, c: (0, 0)),
        pl.BlockSpec((LANES, 2 * BRANCH_W), lambda b, c: (0, 0)),
        pl.BlockSpec((GATE_LORA, BRANCH_W), lambda b, c: (0, 0)),
        pl.BlockSpec((SUBLANES, BRANCH_W), lambda b, c: (0, 0)),
    ]
    args += [mu, wwa, gup, vec]
    if has_state:
        in_specs.append(_layer_state_spec(layer, ngroup * nseq, tail))
        args.append(state)
    stack_specs, stack_args, stack_shape, aliases = _stack_io(stack, depth, nbatch * nseq, tail, len(args))
    y_spec, y_shape = _y_io(nbatch, nchunk, ngroup)
    return pl.pallas_call(
        functools.partial(_rwkv_body, ngroup=ngroup, nseq=nseq, null_rows=null_rows, has_state=has_state,
                          has_stack=stack is not None),
        grid=(nbatch // ngroup, nchunk),
        in_specs=in_specs + stack_specs,
        out_specs=[y_spec, _layer_state_spec(layer, ngroup * nseq, tail)],
        out_shape=[y_shape, stack_shape],
        input_output_aliases=aliases,
        scratch_shapes=[pltpu.VMEM((ngroup, CHUNK + SUBLANES, N_CC), F32),
                        pltpu.VMEM((ngroup * N_PAIR, 2 * N_C, 2 * N_C), F32)],
        compiler_params=_cparams(2),
        name="rwkv",
    )(*args, *stack_args)


def _pad_cols(w, n):
    return jnp.pad(w, ((0, 0),) * (w.ndim - 1) + ((0, n - w.shape[-1]),))


def _pad_heads(w):
    z = jnp.zeros(w.shape[:-1] + (LANES - DK_B,), w.dtype)
    parts = []
    for h in range(H_B):
        parts += [w[..., h * DK_B:(h + 1) * DK_B], z]
    return jnp.concatenate(parts, axis=-1)


def _prep_in(w_in):
    w = w_in.astype(BF16)
    wa = w[..., :N_A]
    wb = w[..., N_A:N_A + N_B]
    wc = w[..., N_A + N_B:N_A + N_B + N_CC]
    wg = w[..., N_A + N_B + N_CC:]
    w1 = wa[..., :QKV_A + H_A * DV_A]
    kq = H_B * DK_B
    w2 = jnp.concatenate([_pad_heads(wb[..., :kq]), _pad_heads(wb[..., kq:2 * kq]),
                          wb[..., 2 * kq:2 * kq + 2 * H_B * DV_B]], axis=-1)
    misc = jnp.concatenate([_pad_cols(wa[..., QKV_A + H_A * DV_A:], LANES),
                            _pad_cols(wb[..., 2 * kq + 2 * H_B * DV_B:], LANES)], axis=-1)
    w3 = jnp.concatenate([wc, misc], axis=-1)
    return w1, w2, w3, wg


def _prep_ffn(w_up, w_down):
    wg = _pad_cols(w_up[..., :D_FF], D_FF_PAD).astype(BF16)
    wu = _pad_cols(w_up[..., D_FF:], D_FF_PAD).astype(BF16)
    wd = jnp.pad(w_down, ((0, 0), (0, D_FF_PAD - D_FF), (0, 0))).astype(BF16)
    return wg, wu, wd


def _pad_sample_rows(p, nb, ts):
    n = p.shape[-1]
    return jnp.pad(p.reshape(nb, ts, n), ((0, 0), (SAMPLE_ROWS - ts, 0), (0, 0))).reshape(nb * SAMPLE_ROWS, n)


def _tail_rows(p, nb, t, n_tail, width):
    return jnp.stack([p[(b + 1) * t - n_tail:(b + 1) * t, :width] for b in range(nb)])


def kernel(x_prompt, x_sample, state_gdn, state_gdn_conv, state_gla, state_rwkv, state_rwkv_shift, w_in, conv_a, a_log, dt_bias, gdn_norm, gla_gk_up, gla_gk_bias, gla_norm, rwkv_mu, rwkv_w0, rwkv_w_up, rwkv_a0, rwkv_a_up, rwkv_g_up, rwkv_k_k, rwkv_k_a, rwkv_r_k, rwkv_ln_w, rwkv_ln_b, w_branch, w_out, norm_ff1, w_ff1_up, w_ff1_down, norm_mix, norm_ff2, w_ff2_up, w_ff2_down, norm_final):
    bp, tp, d = x_prompt.shape
    bs, ts, _ = x_sample.shape
    depth = w_in.shape[0]
    n_p, n_s = bp * tp, bs * ts
    null_rows = SAMPLE_ROWS - ts
    assert tp % CHUNK == 0 and CONV_W - 1 <= null_rows < SAMPLE_ROWS
    seq_per_chunk = CHUNK // SAMPLE_ROWS
    assert bs % seq_per_chunk == 0
    nchunk_p = tp // CHUNK
    nstep_s = bs // seq_per_chunk
    prompt = dict(nbatch=bp, nchunk=nchunk_p, nseq=1, null_rows=0, ngroup=math.gcd(bp, PROMPT_GROUPS))
    sample = dict(nbatch=nstep_s, nchunk=1, nseq=seq_per_chunk, null_rows=null_rows,
                  ngroup=math.gcd(nstep_s, SAMPLE_GROUPS))

    x = jnp.concatenate([x_prompt.reshape(n_p, d), x_sample.reshape(n_s, d)], axis=0)
    gdn_p = gdn_s = gla_p = gla_s = rwkv_p = rwkv_s = None
    conv_p, conv_s, shift_p, shift_s = [], [], [], []
    ff1 = _prep_ffn(w_ff1_up, w_ff1_down)
    ff2 = _prep_ffn(w_ff2_up, w_ff2_down)
    w1, w2, w3, wg = _prep_in(w_in)
    wbr = w_branch.astype(BF16)
    wout = w_out.astype(BF16)
    for l in range(depth):
        h = _ffn(x, norm_ff1[l], *ff1, l)

        hn = _norm_cast(h, norm_mix[l], BF16)
        p1 = _proj(hn, w1, l, w1.shape[2] // 2)
        p2 = _proj(hn, w2, l, w2.shape[2] // 2)
        p3 = _proj(hn, w3, l, w3.shape[2] // 2)
        pg = _proj(hn, wg, l, wg.shape[2] // 4, BF16)
        p1s, p2s, p3s = (_pad_sample_rows(p[n_p:], bs, ts) for p in (p1, p2, p3))

        prm = jnp.zeros((SUBLANES, LANES), F32)
        prm = prm.at[0, H_A:2 * H_A].set(a_log[l]).at[1, H_A:2 * H_A].set(dt_bias[l])
        gnorm = gdn_norm[l].reshape(1, DV_A)
        up_w = jnp.pad(gla_gk_up[l].reshape(GLA_RANK, H_B, DK_B),
                       ((0, LANES - GLA_RANK), (0, 0), (0, LANES - DK_B))).reshape(LANES, H_B * LANES).astype(BF16)
        gk_bias = jnp.pad(gla_gk_bias[l].reshape(H_B, DK_B), ((0, 0), (0, LANES - DK_B))).reshape(1, H_B * LANES)
        lnorm = gla_norm[l].reshape(1, DV_B)
        mu = rwkv_mu[l].reshape(1, N_CC)
        wwa = jnp.zeros((LANES, 2 * BRANCH_W), F32)
        wwa = wwa.at[0:DECAY_LORA, 0:BRANCH_W].set(rwkv_w_up[l])
        wwa = wwa.at[DECAY_LORA:DECAY_LORA + AAA_LORA, BRANCH_W:].set(rwkv_a_up[l]).astype(BF16)
        gup = rwkv_g_up[l].astype(BF16)
        vec = jnp.stack([rwkv_w0[l], rwkv_a0[l], rwkv_k_k[l], rwkv_k_a[l], rwkv_r_k[l],
                         rwkv_ln_w[l], rwkv_ln_b[l], jnp.zeros((BRANCH_W,), F32)], axis=0)

        ya_p, gdn_p = _gdn(p1, p3, None, None, gdn_p, l, depth, conv_a[l], prm, gnorm, **prompt)
        ya_s, gdn_s = _gdn(p1s, p3s, state_gdn_conv, state_gdn, gdn_s, l, depth, conv_a[l], prm, gnorm, **sample)
        yb_p, gla_p = _gla(p2, p3, None, gla_p, l, depth, up_w, gk_bias, lnorm, **prompt)
        yb_s, gla_s = _gla(p2s, p3s, state_gla, gla_s, l, depth, up_w, gk_bias, lnorm, **sample)
        yc_p, rwkv_p = _rwkv(p3, None, None, rwkv_p, l, depth, mu, wwa, gup, vec, **prompt)
        yc_s, rwkv_s = _rwkv(p3s, state_rwkv_shift, state_rwkv, rwkv_s, l, depth, mu, wwa, gup, vec, **sample)

        def join(y_p, y_s):
            y_s = y_s.reshape(bs, SAMPLE_ROWS, BRANCH_W)[:, null_rows:].reshape(n_s, BRANCH_W)
            return jnp.concatenate([y_p.reshape(n_p, BRANCH_W), y_s], axis=0)

        h = _merge(join(ya_p, ya_s), join(yb_p, yb_s), join(yc_p, yc_s), pg, wbr, wout, h, l)
        x = _ffn(h, norm_ff2[l], *ff2, l)

        conv_p.append(_tail_rows(p1, bp, tp, CONV_W - 1, QKV_A))
        conv_s.append(p1[n_p:].reshape(bs, ts, -1)[:, ts - (CONV_W - 1):, :QKV_A])
        shift_p.append(_tail_rows(p3, bp, tp, 1, N_CC))
        shift_s.append(p3[n_p:].reshape(bs, ts, -1)[:, ts - 1:, :N_CC])

    y_p = _norm_cast(x, norm_final, F32, 0, n_p)
    y_s = _norm_cast(x, norm_final, F32, n_p, n_s)
    return (y_p.reshape(bp, tp, d), y_s.reshape(bs, ts, d),
            gdn_p, gdn_s, jnp.stack(conv_p), jnp.stack(conv_s), gla_p, gla_s,
            rwkv_p, rwkv_s, jnp.stack(shift_p), jnp.stack(shift_s))
```

```python
import functools
import math

import jax
import jax.numpy as jnp
from jax import lax
from jax.experimental import pallas as pl
from jax.experimental.pallas import tpu as pltpu

F32 = jnp.float32
BF16 = jnp.bfloat16

D_MODEL = 2048
N_BRANCH = 3
BRANCH_W = 768
DK_A, DV_A, H_A, CONV_W = 128, 128, 6, 4
DK_B, DV_B, H_B, GLA_RANK, GLA_NORMALIZER = 64, 128, 6, 16, 16.0
N_C, H_C, DECAY_LORA, AAA_LORA, GATE_LORA = 64, 12, 64, 64, 128
GN_EPS = 64e-5
D_FF = 5504
NORM_EPS = 1e-6
QKV_A = 2 * H_A * DK_A + H_A * DV_A
N_A = QKV_A + H_A * DV_A + 2 * H_A
N_B = 2 * H_B * DK_B + 2 * H_B * DV_B + GLA_RANK
N_CC = 3 * BRANCH_W + DECAY_LORA + AAA_LORA + GATE_LORA
N_PAIR = H_C // 2

LANES = 128
SUBLANES = 8
CHUNK = 64
PROMPT_GROUPS = 4
SAMPLE_GROUPS = 2
SAMPLE_ROWS = 8
FF_TILE = 512
VMEM_LIMIT = 60 * 1024 * 1024


def _cparams(n_axes):
    return pltpu.CompilerParams(dimension_semantics=("arbitrary",) * n_axes,
                                vmem_limit_bytes=VMEM_LIMIT)


def _pick_tile(n, prefs):
    for t in prefs:
        if n % t == 0:
            return t
    raise ValueError(f"no tile for {n}")


def _bdot(a, b):
    return jnp.dot(a.astype(BF16), b.astype(BF16), preferred_element_type=F32)


def _bdot_nt(a, b):
    return lax.dot_general(a.astype(BF16), b.astype(BF16), (((1,), (1,)), ((), ())),
                           preferred_element_type=F32)


def _bdot_tn(a, b):
    return lax.dot_general(a.astype(BF16), b.astype(BF16), (((0,), (0,)), ((), ())),
                           preferred_element_type=F32)


def _sigmoid(x):
    return jax.nn.sigmoid(x)


def _silu(x):
    return x * _sigmoid(x)


def _softplus(x):
    return jnp.maximum(x, 0.0) + jnp.log(1.0 + jnp.exp(-jnp.abs(x)))


def _rmsnorm_rows(x, g):
    return x * lax.rsqrt(jnp.mean(x * x, axis=-1, keepdims=True) + NORM_EPS) * g


def _chunk_masks(rows, seq_len):
    sh = int(math.log2(seq_len))
    ri = lax.broadcasted_iota(jnp.int32, (rows, rows), 0)
    ci = lax.broadcasted_iota(jnp.int32, (rows, rows), 1)
    same = (ri >> sh) == (ci >> sh)
    return same & (ri >= ci), same & (ri > ci), same


def _eye(rows):
    return (lax.broadcasted_iota(jnp.int32, (rows, rows), 0)
            == lax.broadcasted_iota(jnp.int32, (rows, rows), 1)).astype(F32)


def _inv_unit_lower(lms, eye, nil):
    xs = [eye - lm for lm in lms]
    ps = list(lms)
    k = 2
    while k < nil:
        ps = [_bdot(p, p) for p in ps]
        xs = [x + _bdot(x, p) for x, p in zip(xs, ps)]
        k *= 2
    return xs


def _row_valid(rows, seq_len, null_rows, width):
    r = lax.broadcasted_iota(jnp.int32, (rows, width), 0)
    return (r & (seq_len - 1)) >= null_rows


def _seq_rows(rows, seq_len, s, width):
    r = lax.broadcasted_iota(jnp.int32, (rows, width), 0) & (CHUNK - 1)
    return (r >> int(math.log2(seq_len))) == s


def _pick(sel, xv):
    return xv if sel is None else jnp.where(sel, xv, 0.0)


def _ffn_body(x_ref, g_ref, wg_ref, wu_ref, wd_ref, o_ref, xn_ref, *, d_ff):
    j = pl.program_id(1)

    @pl.when(j == 0)
    def _():
        xn_ref[...] = _rmsnorm_rows(x_ref[...], g_ref[...]).astype(BF16)
        o_ref[...] = jnp.zeros_like(o_ref)

    valid = d_ff - j * FF_TILE
    col_ok = lax.broadcasted_iota(jnp.int32, (1, FF_TILE), 1) < valid
    row_ok = lax.broadcasted_iota(jnp.int32, (FF_TILE, 1), 0) < valid
    xn = xn_ref[...]
    gate = jnp.dot(xn, wg_ref[...], preferred_element_type=F32)
    up = jnp.dot(xn, wu_ref[...], preferred_element_type=F32)
    act = jnp.where(col_ok, _silu(gate) * up, 0.0).astype(BF16)
    wd = jnp.where(row_ok, wd_ref[...], jnp.zeros((), BF16))
    o_ref[...] += jnp.dot(act, wd, preferred_element_type=F32)

    @pl.when(j == pl.num_programs(1) - 1)
    def _():
        o_ref[...] = x_ref[...] + 0.5 * o_ref[...]


def _ffn(x, g, wg, wu, wd, layer):
    n, d = x.shape
    tm = _pick_tile(n, (1088, 544, 512, 256, 128, 64, 32, 16, 8))
    d_ff = wg.shape[2]
    nf = pl.cdiv(d_ff, FF_TILE)
    return pl.pallas_call(
        functools.partial(_ffn_body, d_ff=d_ff),
        grid=(n // tm, nf),
        in_specs=[
            pl.BlockSpec((tm, d), lambda i, j: (i, 0)),
            pl.BlockSpec((1, d), lambda i, j: (0, 0)),
            pl.BlockSpec((None, d, FF_TILE), lambda i, j: (layer, 0, j)),
            pl.BlockSpec((None, d, FF_TILE), lambda i, j: (layer, 0, j)),
            pl.BlockSpec((None, FF_TILE, d), lambda i, j: (layer, j, 0)),
        ],
        out_specs=pl.BlockSpec((tm, d), lambda i, j: (i, 0)),
        out_shape=jax.ShapeDtypeStruct((n, d), F32),
        scratch_shapes=[pltpu.VMEM((tm, d), BF16)],
        compiler_params=_cparams(2),
        name="ffn",
    )(x, g.reshape(1, d), wg, wu, wd)


def _norm_cast_body(x_ref, g_ref, o_ref):
    o_ref[...] = _rmsnorm_rows(x_ref[...], g_ref[...]).astype(o_ref.dtype)


def _norm_cast(x, g, dtype, start=0, count=None):
    n, d = x.shape
    count = n - start if count is None else count
    tm = _pick_tile(math.gcd(start, count), (544, 512, 256, 128, 64, 32, 16, 8))
    first = start // tm
    return pl.pallas_call(
        _norm_cast_body,
        grid=(count // tm,),
        in_specs=[pl.BlockSpec((tm, d), lambda i: (first + i, 0)), pl.BlockSpec((1, d), lambda i: (0, 0))],
        out_specs=pl.BlockSpec((tm, d), lambda i: (i, 0)),
        out_shape=jax.ShapeDtypeStruct((count, d), dtype),
        compiler_params=_cparams(1),
        name="norm",
    )(x, g.reshape(1, d))


def _pad_rows(x):
    n, w = x.shape
    x3 = x.reshape(n // SUBLANES, SUBLANES, w)
    tail = lax.broadcasted_iota(jnp.int32, x3.shape, 1) >= SUBLANES // 2
    first = jnp.where(tail, pltpu.roll(x3, SUBLANES // 2, 1), 0.0)
    second = jnp.where(tail, x3, 0.0)
    return jnp.stack([first, second], axis=1).reshape(2 * n, w)


def _compact_rows(x):
    n, w = x.shape
    x4 = x.reshape(n // (2 * SUBLANES), 2, SUBLANES, w)
    head = lax.broadcasted_iota(jnp.int32, (n // (2 * SUBLANES), SUBLANES, w), 1) < SUBLANES // 2
    return jnp.where(head, pltpu.roll(x4[:, 0], SUBLANES // 2, 1), x4[:, 1]).reshape(n // 2, w)


def _proj_body(x_ref, w_ref, o_ref, *pad_ref, sample_off, n_sample):
    res = jnp.dot(x_ref[...], w_ref[...], preferred_element_type=F32)
    o_ref[...] = res.astype(o_ref.dtype)
    if pad_ref:
        @pl.when(pl.program_id(1) == pl.num_programs(1) - 1)
        def _():
            pad_ref[0][...] = _pad_rows(res[sample_off:sample_off + n_sample])


def _proj(xn, w, layer, tn, out_dtype=F32, n_sample=0):
    n, d = xn.shape
    nn = w.shape[2]
    tm = _pick_tile(n, (1088, 544, 512, 256, 128, 64, 32, 16))
    out_specs = [pl.BlockSpec((tm, tn), lambda j, i: (i, j))]
    out_shape = [jax.ShapeDtypeStruct((n, nn), out_dtype)]
    sample_off = tm - n_sample
    if n_sample:
        assert 0 <= sample_off and sample_off % SUBLANES == 0 and n_sample % SUBLANES == 0
        out_specs.append(pl.BlockSpec((2 * n_sample, tn), lambda j, i: (0, j)))
        out_shape.append(jax.ShapeDtypeStruct((2 * n_sample, nn), F32))
    out = pl.pallas_call(
        functools.partial(_proj_body, sample_off=sample_off, n_sample=n_sample),
        grid=(nn // tn, n // tm),
        in_specs=[
            pl.BlockSpec((tm, d), lambda j, i: (i, 0)),
            pl.BlockSpec((None, d, tn), lambda j, i: (layer, 0, j)),
        ],
        out_specs=out_specs,
        out_shape=out_shape,
        compiler_params=_cparams(2),
        name="proj",
    )(xn, w)
    return out if n_sample else out[0]


MERGE_TN = 1024


def _merge_body(ya_ref, yb_ref, yc_ref, ga_ref, gb_ref, gc_ref, wb_ref, wo_ref, h_ref, o_ref):
    j = pl.program_id(1)

    @pl.when(j == 0)
    def _():
        o_ref[...] = h_ref[...]

    m = _sigmoid(ga_ref[...].astype(F32)) * jnp.dot(ya_ref[...], wb_ref[0], preferred_element_type=F32)
    m += _sigmoid(gb_ref[...].astype(F32)) * jnp.dot(yb_ref[...], wb_ref[1], preferred_element_type=F32)
    m += _sigmoid(gc_ref[...].astype(F32)) * jnp.dot(yc_ref[...], wb_ref[2], preferred_element_type=F32)
    o_ref[...] += jnp.dot(m.astype(BF16), wo_ref[...], preferred_element_type=F32)


def _merge(ya, yb, yc, pg, wb, wo, h, layer):
    n, d = h.shape
    tm = _pick_tile(n, (544, 512, 256, 128, 64, 32, 16))
    nj = d // MERGE_TN
    yspec = pl.BlockSpec((tm, BRANCH_W), lambda i, j: (i, 0))
    return pl.pallas_call(
        _merge_body,
        grid=(n // tm, nj),
        in_specs=[
            yspec, yspec, yspec,
            pl.BlockSpec((tm, MERGE_TN), lambda i, j: (i, j)),
            pl.BlockSpec((tm, MERGE_TN), lambda i, j: (i, nj + j)),
            pl.BlockSpec((tm, MERGE_TN), lambda i, j: (i, 2 * nj + j)),
            pl.BlockSpec((None, N_BRANCH, BRANCH_W, MERGE_TN), lambda i, j: (layer, 0, 0, j)),
            pl.BlockSpec((None, MERGE_TN, d), lambda i, j: (layer, j, 0)),
            pl.BlockSpec((tm, d), lambda i, j: (i, 0)),
        ],
        out_specs=pl.BlockSpec((tm, d), lambda i, j: (i, 0)),
        out_shape=jax.ShapeDtypeStruct((n, d), F32),
        compiler_params=_cparams(2),
        name="merge",
    )(ya, yb, yc, pg, pg, pg, wb, wo, h)


def _layer_state_spec(layer, nseq, tail):
    zeros = (0,) * len(tail)
    return pl.BlockSpec((None, nseq) + tail, lambda b, c: (layer, b) + zeros)


def _stack_io(stack, depth, nstate, tail, n_args):
    shape = jax.ShapeDtypeStruct((depth, nstate) + tail, F32)
    if stack is None:
        return [], [], shape, {}
    return [pl.BlockSpec(memory_space=pl.ANY)], [stack], shape, {n_args: 1}


def _split3_dot(lmat, x):
    lb = lmat.astype(BF16)
    hi = x.astype(BF16)
    r1 = x - hi.astype(F32)
    mid = r1.astype(BF16)
    lo = (r1 - mid.astype(F32)).astype(BF16)
    return (jnp.dot(lb, hi, preferred_element_type=F32) + jnp.dot(lb, mid, preferred_element_type=F32)
            + jnp.dot(lb, lo, preferred_element_type=F32))


def _gdn_body(*refs, ngroup, nseq, null_rows, has_state, has_stack):
    refs = list(refs)
    xbuf = refs.pop()
    so_ref = refs.pop()
    y_ref = refs.pop()
    if has_stack:
        refs.pop()
    row_refs = [refs[3 * g:3 * g + 3] for g in range(ngroup)]
    refs = refs[3 * ngroup:]
    if has_state:
        cs_ref, cw_ref, prm_ref, nw_ref, s0_ref = refs
    else:
        cw_ref, prm_ref, nw_ref = refs
        s0_ref = so_ref
    rows = CHUNK
    seq_len = rows // nseq
    c = pl.program_id(1)
    chains = [(g, h) for g in range(ngroup) for h in range(H_A)]

    @pl.when(c == 0)
    def _():
        for g in range(ngroup):
            xbuf[g, 0:SUBLANES, :] = jnp.zeros((SUBLANES, QKV_A), F32)
        if not has_state:
            so_ref[...] = jnp.zeros_like(so_ref)

    @pl.when(c > 0)
    def _():
        for g in range(ngroup):
            xbuf[g, 0:SUBLANES, :] = xbuf[g, rows:rows + SUBLANES, :]

    for g in range(ngroup):
        xbuf[g, SUBLANES:SUBLANES + rows, :] = row_refs[g][0][...]
        if has_state:
            for s in range(nseq):
                r0 = SUBLANES + s * seq_len + null_rows - (CONV_W - 1)
                xbuf[g, r0:r0 + CONV_W - 1, :] = cs_ref[g * nseq + s]

    incl, strict, same = _chunk_masks(rows, seq_len)
    eye = _eye(rows)
    valid = _row_valid(rows, seq_len, null_rows, LANES) if null_rows else None
    sels = [_seq_rows(rows, seq_len, s, LANES) for s in range(nseq)] if nseq > 1 else [None]
    sels2 = [_seq_rows(2 * rows, seq_len, s, LANES) for s in range(nseq)] if nseq > 1 else [None]
    lstack = jnp.concatenate([incl.astype(F32), same.astype(F32)], axis=0)

    def conv_cols(g, c0):
        acc = xbuf[g, SUBLANES - 3:SUBLANES - 3 + rows, c0:c0 + LANES] * cw_ref[0:1, c0:c0 + LANES]
        for j in range(1, CONV_W):
            acc += (xbuf[g, SUBLANES - 3 + j:SUBLANES - 3 + j + rows, c0:c0 + LANES]
                    * cw_ref[j:j + 1, c0:c0 + LANES])
        return _silu(acc)

    qs, ks, kbs, rhs, decays, wq_parts, kds, gtcs = [], [], [], [], [], [], [], []
    for g in range(ngroup):
        ba = row_refs[g][2][...]
        beta_all = _sigmoid(ba)
        g_all = -jnp.exp(prm_ref[0:1, :]) * _softplus(ba + prm_ref[1:2, :])
        if null_rows:
            g_all = jnp.where(valid, g_all, 0.0)
        cs = _split3_dot(lstack, g_all)
        gc_all, gt_all = cs[0:rows], cs[rows:2 * rows]
        gc_t = gc_all.T
        for h in range(H_A):
            q = conv_cols(g, h * DK_A)
            k = conv_cols(g, H_A * DK_A + h * DK_A)
            v = conv_cols(g, 2 * H_A * DK_A + h * DV_A)
            q = q * lax.rsqrt(jnp.sum(q * q, axis=-1, keepdims=True) + 1e-6) * DK_A ** -0.5
            k = k * lax.rsqrt(jnp.sum(k * k, axis=-1, keepdims=True) + 1e-6)
            if null_rows:
                q = jnp.where(valid, q, 0.0)
                k = jnp.where(valid, k, 0.0)
                v = jnp.where(valid, v, 0.0)
            beta = beta_all[:, h:h + 1]
            gcc = gc_all[:, H_A + h:H_A + h + 1]
            gtc = gt_all[:, H_A + h:H_A + h + 1]
            gcr = gc_t[H_A + h:H_A + h + 1, :]
            egc = jnp.exp(gcc)
            kb = k * beta
            qs.append(q)
            ks.append(k)
            kbs.append(kb)
            rhs.append(jnp.concatenate([v * beta, kb * egc], axis=1))
            decays.append(jnp.where(incl, jnp.exp(jnp.where(incl, gcc - gcr, 0.0)), 0.0))
            wq_parts.append(q * egc)
            kds.append(k * jnp.exp(gtc - gcc))
            gtcs.append(gtc)

    n = range(len(chains))
    lms = [jnp.where(strict, _bdot_nt(kbs[i], ks[i]) * decays[i], 0.0) for i in n]
    attns = [_bdot_nt(qs[i], ks[i]) * decays[i] for i in n]
    tmats = _inv_unit_lower(lms, eye, seq_len)
    uws = [_bdot(tmats[i], rhs[i]) for i in n]
    us, os_ = [], []
    for i, (g, h) in enumerate(chains):
        u = uws[i][:, 0:DV_A]
        wq = jnp.concatenate([uws[i][:, DV_A:DV_A + DK_A], wq_parts[i]], axis=0)
        o = None
        for s in range(nseq):
            both = _bdot(_pick(sels2[s], wq), s0_ref[g * nseq + s, h])
            u = u - both[0:rows]
            o = both[rows:2 * rows] if o is None else o + both[rows:2 * rows]
        us.append(u)
        os_.append(o)
    os_ = [os_[i] + _bdot(attns[i], us[i]) for i in n]
    for i, (g, h) in enumerate(chains):
        for s in range(nseq):
            g_last = jnp.exp(gtcs[i][s * seq_len:s * seq_len + 1, :])
            so_ref[g * nseq + s, h] = (g_last * s0_ref[g * nseq + s, h]
                                       + _bdot_tn(_pick(sels[s], kds[i]), us[i]))
    for i, (g, h) in enumerate(chains):
        z = row_refs[g][1][:, h * DV_A:(h + 1) * DV_A]
        _store_y(y_ref, g, h * DV_A, _rmsnorm_rows(os_[i], nw_ref[...]) * _silu(z), null_rows)


def _row_spec(width, col, g, ngroup, nchunk):
    return pl.BlockSpec((CHUNK, width), lambda b, c: ((b * ngroup + g) * nchunk + c, col))


def _y_io(nbatch, nchunk, ngroup, null_rows):
    out_rows = CHUNK // 2 if null_rows else CHUNK
    return (pl.BlockSpec((ngroup, out_rows, BRANCH_W), lambda b, c: (b, c, 0)),
            jax.ShapeDtypeStruct((nbatch, nchunk * out_rows, BRANCH_W), BF16))


def _store_y(y_ref, g, c0, yv, null_rows):
    if null_rows:
        yv = _compact_rows(yv)
    y_ref[g, :, c0:c0 + yv.shape[1]] = yv.astype(y_ref.dtype)


def _gdn(p1, p3, conv_state, state, stack, layer, depth, conv_w, prm, norm_w, *,
         nbatch, nchunk, nseq, null_rows, ngroup):
    has_state = state is not None
    tail = (H_A, DK_A, DV_A)
    in_specs, args = [], []
    for g in range(ngroup):
        in_specs += [_row_spec(QKV_A, 0, g, ngroup, nchunk),
                     _row_spec(BRANCH_W, QKV_A // BRANCH_W, g, ngroup, nchunk),
                     _row_spec(LANES, N_CC // LANES, g, ngroup, nchunk)]
        args += [p1, p1, p3]
    if has_state:
        in_specs.append(_layer_state_spec(layer, ngroup * nseq, (CONV_W - 1, QKV_A)))
        args.append(conv_state)
    in_specs += [
        pl.BlockSpec((CONV_W, QKV_A), lambda b, c: (0, 0)),
        pl.BlockSpec((SUBLANES, LANES), lambda b, c: (0, 0)),
        pl.BlockSpec((1, DV_A), lambda b, c: (0, 0)),
    ]
    args += [conv_w, prm, norm_w]
    if has_state:
        in_specs.append(_layer_state_spec(layer, ngroup * nseq, tail))
        args.append(state)
    stack_specs, stack_args, stack_shape, aliases = _stack_io(stack, depth, nbatch * nseq, tail, len(args))
    y_spec, y_shape = _y_io(nbatch, nchunk, ngroup, null_rows)
    return pl.pallas_call(
        functools.partial(_gdn_body, ngroup=ngroup, nseq=nseq, null_rows=null_rows, has_state=has_state,
                          has_stack=stack is not None),
        grid=(nbatch // ngroup, nchunk),
        in_specs=in_specs + stack_specs,
        out_specs=[y_spec, _layer_state_spec(layer, ngroup * nseq, tail)],
        out_shape=[y_shape, stack_shape],
        input_output_aliases=aliases,
        scratch_shapes=[pltpu.VMEM((ngroup, CHUNK + SUBLANES, QKV_A), F32)],
        compiler_params=_cparams(2),
        name="gdn",
    )(*args, *stack_args)


def _gla_body(*refs, ngroup, nseq, null_rows, has_state, has_stack):
    refs = list(refs)
    so_ref = refs.pop()
    y_ref = refs.pop()
    if has_stack:
        refs.pop()
    row_refs = [refs[5 * g:5 * g + 5] for g in range(ngroup)]
    refs = refs[5 * ngroup:]
    if has_state:
        up_ref, bias_ref, nw_ref, s0_ref = refs
    else:
        up_ref, bias_ref, nw_ref = refs
        s0_ref = so_ref
    rows = CHUNK
    seq_len = rows // nseq
    c = pl.program_id(1)
    chains = [(g, h) for g in range(ngroup) for h in range(H_B)]
    if not has_state:
        @pl.when(c == 0)
        def _():
            so_ref[...] = jnp.zeros_like(so_ref)

    incl, _, same = _chunk_masks(rows, seq_len)
    ci = lax.broadcasted_iota(jnp.int32, (rows, rows), 1)
    first_half = same & ((ci & (seq_len - 1)) < seq_len // 2)
    wide = H_B * LANES
    lstack = jnp.concatenate([incl.astype(F32), same.astype(F32), first_half.astype(F32)], axis=0)
    valid_w = _row_valid(rows, seq_len, null_rows, wide) if null_rows else None
    valid_k = _row_valid(rows, seq_len, null_rows, DK_B) if null_rows else None
    valid_v = _row_valid(rows, seq_len, null_rows, DV_B) if null_rows else None
    sels = [_seq_rows(rows, seq_len, s, DK_B) for s in range(nseq)] if nseq > 1 else [None]

    qis, kis, vs, q_ins, k_decs, a_lasts = [], [], [], [], [], []
    for g in range(ngroup):
        q_ref, k_ref, v_ref, _, gkl_ref = row_refs[g]
        x = _bdot(gkl_ref[...], up_ref[...]) + bias_ref[...]
        gk_all = (jnp.minimum(x, 0.0) - jnp.log(1.0 + jnp.exp(-jnp.abs(x)))) / GLA_NORMALIZER
        if null_rows:
            gk_all = jnp.where(valid_w, gk_all, 0.0)
        cs = _split3_dot(lstack, gk_all)
        bc_all, bt_all, an_all = cs[0:rows], cs[rows:2 * rows], cs[2 * rows:3 * rows]
        bt_t = bt_all.T
        for h in range(H_B):
            c0 = h * LANES
            q = q_ref[:, c0:c0 + DK_B] * DK_B ** -0.5
            k = k_ref[:, c0:c0 + DK_B]
            v = v_ref[:, h * DV_B:(h + 1) * DV_B]
            if null_rows:
                q = jnp.where(valid_k, q, 0.0)
                k = jnp.where(valid_k, k, 0.0)
                v = jnp.where(valid_v, v, 0.0)
            bc = bc_all[:, c0:c0 + DK_B]
            bt = bt_all[:, c0:c0 + DK_B]
            an = an_all[:, c0:c0 + DK_B]
            qis.append(q * jnp.exp(bc - an))
            kis.append(k * jnp.exp(an - bc))
            vs.append(v)
            q_ins.append(q * jnp.exp(bc))
            k_decs.append(k * jnp.exp(bt - bc))
            a_lasts.append([jnp.exp(bt_t[c0:c0 + DK_B, s * seq_len:s * seq_len + 1])
                            for s in range(nseq)])

    n = range(len(chains))
    attns = [jnp.where(incl, _bdot_nt(qis[i], kis[i]), 0.0) for i in n]
    os_ = [_bdot(attns[i], vs[i]) for i in n]
    for i, (g, h) in enumerate(chains):
        for s in range(nseq):
            os_[i] = os_[i] + _bdot(_pick(sels[s], q_ins[i]), s0_ref[g * nseq + s, h])
    for i, (g, h) in enumerate(chains):
        for s in range(nseq):
            so_ref[g * nseq + s, h] = (a_lasts[i][s] * s0_ref[g * nseq + s, h]
                                       + _bdot_tn(_pick(sels[s], k_decs[i]), vs[i]))
    for i, (g, h) in enumerate(chains):
        gate = row_refs[g][3][:, h * DV_B:(h + 1) * DV_B]
        _store_y(y_ref, g, h * DV_B, _rmsnorm_rows(os_[i], nw_ref[...]) * _silu(gate), null_rows)


def _gla(p2, p3, state, stack, layer, depth, up_w, bias, norm_w, *, nbatch, nchunk, nseq, null_rows, ngroup):
    has_state = state is not None
    wide = H_B * LANES
    tail = (H_B, DK_B, DV_B)
    in_specs, args = [], []
    for g in range(ngroup):
        in_specs += [_row_spec(wide, j, g, ngroup, nchunk) for j in range(4)]
        in_specs.append(_row_spec(LANES, N_CC // LANES + 1, g, ngroup, nchunk))
        args += [p2, p2, p2, p2, p3]
    in_specs += [pl.BlockSpec((LANES, wide), lambda b, c: (0, 0)),
                 pl.BlockSpec((1, wide), lambda b, c: (0, 0)),
                 pl.BlockSpec((1, DV_B), lambda b, c: (0, 0))]
    args += [up_w, bias, norm_w]
    if has_state:
        in_specs.append(_layer_state_spec(layer, ngroup * nseq, tail))
        args.append(state)
    stack_specs, stack_args, stack_shape, aliases = _stack_io(stack, depth, nbatch * nseq, tail, len(args))
    y_spec, y_shape = _y_io(nbatch, nchunk, ngroup, null_rows)
    return pl.pallas_call(
        functools.partial(_gla_body, ngroup=ngroup, nseq=nseq, null_rows=null_rows, has_state=has_state,
                          has_stack=stack is not None),
        grid=(nbatch // ngroup, nchunk),
        in_specs=in_specs + stack_specs,
        out_specs=[y_spec, _layer_state_spec(layer, ngroup * nseq, tail)],
        out_shape=[y_shape, stack_shape],
        input_output_aliases=aliases,
        compiler_params=_cparams(2),
        name="gla",
    )(*args, *stack_args)


def _rwkv_body(*refs, ngroup, nseq, null_rows, has_state, has_stack):
    refs = list(refs)
    sp_ref = refs.pop()
    pbuf = refs.pop()
    so_ref = refs.pop()
    y_ref = refs.pop()
    if has_stack:
        refs.pop()
    pc_refs = refs[:ngroup]
    refs = refs[ngroup:]
    if has_state:
        sh_ref, mu_ref, wwa_ref, gup_ref, vec_ref, s0_ref = refs
    else:
        mu_ref, wwa_ref, gup_ref, vec_ref = refs
    rows = CHUNK
    seq_len = rows // nseq
    c = pl.program_id(1)
    chains = [(g, p) for g in range(ngroup) for p in range(N_PAIR)]
    pairs = range(len(chains))

    @pl.when(c == 0)
    def _():
        for g in range(ngroup):
            pbuf[g, 0:SUBLANES, :] = jnp.zeros((SUBLANES, N_CC), F32)
        if not has_state:
            sp_ref[...] = jnp.zeros_like(sp_ref)

    @pl.when(c > 0)
    def _():
        for g in range(ngroup):
            pbuf[g, 0:SUBLANES, :] = pbuf[g, rows:rows + SUBLANES, :]

    for g in range(ngroup):
        pbuf[g, SUBLANES:SUBLANES + rows, :] = pc_refs[g][...]
        if has_state:
            for s in range(nseq):
                r0 = SUBLANES + s * seq_len + null_rows - 1
                pbuf[g, r0:r0 + 1, :] = sh_ref[g * nseq + s]

    def xc_cols(g, c0, w):
        cur = pbuf[g, SUBLANES:SUBLANES + rows, c0:c0 + w]
        prev = pbuf[g, SUBLANES - 1:SUBLANES - 1 + rows, c0:c0 + w]
        return cur + (prev - cur) * mu_ref[:, c0:c0 + w]

    incl, strict, same = _chunk_masks(rows, seq_len)
    eye = _eye(rows)
    lane = lax.broadcasted_iota(jnp.int32, (rows, LANES), 1)
    lo = lane < N_C
    lo2 = lax.broadcasted_iota(jnp.int32, (2 * rows, LANES), 1) < N_C
    valid = _row_valid(rows, seq_len, null_rows, LANES) if null_rows else None
    valid_w = _row_valid(rows, seq_len, null_rows, BRANCH_W) if null_rows else None
    sels2 = [_seq_rows(2 * rows, seq_len, s, LANES) for s in range(nseq)] if nseq > 1 else [None]
    lstack = jnp.concatenate([incl.astype(F32), same.astype(F32)], axis=0)
    w0 = vec_ref[0:1, :]
    a0 = vec_ref[1:2, :]

    r2 = lax.broadcasted_iota(jnp.int32, (2 * N_C, LANES), 0)
    c2 = lax.broadcasted_iota(jnp.int32, (2 * N_C, LANES), 1)
    blockdiag = (r2 >= N_C) == (c2 >= N_C)

    def seg_sum(xv):
        s_lo = jnp.sum(jnp.where(lo, xv, 0.0), axis=-1, keepdims=True)
        s_hi = jnp.sum(jnp.where(lo, 0.0, xv), axis=-1, keepdims=True)
        return jnp.where(lo, s_lo, s_hi)

    def by_head(stacked):
        return jnp.where(lo, stacked[0:rows], stacked[rows:2 * rows])

    rs, k2s, vs, ar_stacks, b_ts, k_ts, tots, gates = [], [], [], [], [], [], [], []
    for g in range(ngroup):
        lora = xc_cols(g, 3 * BRANCH_W, DECAY_LORA + AAA_LORA + GATE_LORA)
        wa_in = lora[:, 0:LANES]
        wa_in = jnp.where(lo, jnp.tanh(wa_in), wa_in)
        wa = _bdot(wa_in, wwa_ref[...])
        g_all = _bdot(_sigmoid(lora[:, LANES:2 * LANES]), gup_ref[...])
        w_log = -_softplus(-(w0 + wa[:, 0:BRANCH_W])) - 0.5
        lw_all = -jnp.exp(w_log)
        a_all = _sigmoid(a0 + wa[:, BRANCH_W:2 * BRANCH_W])
        if null_rows:
            lw_all = jnp.where(valid_w, lw_all, 0.0)
        cs = _split3_dot(lstack, lw_all)
        cum_all, tot_all = cs[0:rows], cs[rows:2 * rows]
        for p in range(N_PAIR):
            c0 = p * LANES
            r = xc_cols(g, c0, LANES)
            k = xc_cols(g, BRANCH_W + c0, LANES)
            v = xc_cols(g, 2 * BRANCH_W + c0, LANES)
            a_p = a_all[:, c0:c0 + LANES]
            kkr = k * vec_ref[2:3, c0:c0 + LANES]
            kk = kkr * lax.rsqrt(seg_sum(kkr * kkr) + 1e-6)
            k2 = k * (1.0 + (a_p - 1.0) * vec_ref[3:4, c0:c0 + LANES])
            av = -kk
            bv = kk * a_p
            if null_rows:
                r = jnp.where(valid, r, 0.0)
                k2 = jnp.where(valid, k2, 0.0)
                v = jnp.where(valid, v, 0.0)
                av = jnp.where(valid, av, 0.0)
                bv = jnp.where(valid, bv, 0.0)
            cum = cum_all[:, c0:c0 + LANES]
            lw = lw_all[:, c0:c0 + LANES]
            g_inv = jnp.exp(-cum)
            rs.append(r)
            k2s.append(k2)
            vs.append(v)
            ar_stacks.append(jnp.concatenate([av * jnp.exp(cum - lw), r * jnp.exp(cum)], axis=0))
            b_ts.append(bv * g_inv)
            k_ts.append(k2 * g_inv)
            tots.append(tot_all[:, c0:c0 + LANES])
            gates.append(g_all[:, c0:c0 + LANES])

    a_abs, a_rbs, a_aks, a_rks = [], [], [], []
    for p in pairs:
        for hh in range(2):
            lhs = jnp.where(lo2 if hh == 0 else jnp.logical_not(lo2), ar_stacks[p], 0.0)
            mb = _bdot_nt(lhs, b_ts[p])
            mk = _bdot_nt(lhs, k_ts[p])
            a_abs.append(jnp.where(strict, mb[0:rows], 0.0))
            a_rbs.append(jnp.where(incl, mb[rows:2 * rows], 0.0))
            a_aks.append(jnp.where(strict, mk[0:rows], 0.0))
            a_rks.append(jnp.where(incl, mk[rows:2 * rows], 0.0))
    tmats = _inv_unit_lower([-m for m in a_abs], eye, seq_len)

    def pair_stack(mats, p):
        return jnp.concatenate([mats[2 * p], mats[2 * p + 1]], axis=0)

    states, xss = [], []
    for i, (g, p) in enumerate(chains):
        sps, xs = [], None
        for s in range(nseq):
            if has_state:
                sq = g * nseq + s
                sv = jnp.concatenate([s0_ref[sq, 2 * p], s0_ref[sq, 2 * p + 1]], axis=0)
                sp = jnp.where(blockdiag, jnp.concatenate([sv, sv], axis=1), 0.0)
            else:
                sp = sp_ref[i]
            both = _bdot_nt(_pick(sels2[s], ar_stacks[i]), sp)
            xs = both if xs is None else xs + both
            sps.append(sp)
        states.append(sps)
        xss.append(xs)
    yvs = [xss[i][0:rows] + by_head(_bdot(pair_stack(a_aks, i), vs[i])) for i in pairs]
    us = [by_head(_bdot(pair_stack(tmats, i), yvs[i])) for i in pairs]
    os_ = [xss[i][rows:2 * rows]
           + by_head(_bdot(pair_stack(a_rbs, i), us[i]) + _bdot(pair_stack(a_rks, i), vs[i]))
           for i in pairs]
    for i, (g, p) in enumerate(chains):
        uv = jnp.concatenate([us[i], vs[i]], axis=0)
        bk = jnp.concatenate([b_ts[i], k_ts[i]], axis=0)
        for s in range(nseq):
            upd = jnp.where(blockdiag, _bdot_tn(_pick(sels2[s], uv), bk), 0.0)
            sp_new = (states[i][s] + upd) * jnp.exp(tots[i][s * seq_len:s * seq_len + 1, :])
            if has_state:
                sq = g * nseq + s
                so_ref[sq, 2 * p] = sp_new[0:N_C, 0:N_C]
                so_ref[sq, 2 * p + 1] = pltpu.roll(sp_new[N_C:2 * N_C, :], N_C, 1)[:, 0:N_C]
            else:
                sp_ref[i] = sp_new

    for i, (g, p) in enumerate(chains):
        c0 = p * LANES
        o = os_[i]
        mean = seg_sum(o) * (1.0 / N_C)
        d = o - mean
        var = seg_sum(d * d) * (1.0 / N_C)
        on = d * lax.rsqrt(var + GN_EPS) * vec_ref[5:6, c0:c0 + LANES] + vec_ref[6:7, c0:c0 + LANES]
        bonus = seg_sum(rs[i] * k2s[i] * vec_ref[4:5, c0:c0 + LANES]) * vs[i]
        _store_y(y_ref, g, c0, (on + bonus) * gates[i], null_rows)

    if not has_state:
        @pl.when(c == pl.num_programs(1) - 1)
        def _():
            for i, (g, p) in enumerate(chains):
                sp = sp_ref[i]
                so_ref[g, 2 * p] = sp[0:N_C, 0:N_C]
                so_ref[g, 2 * p + 1] = pltpu.roll(sp[N_C:2 * N_C, :], N_C, 1)[:, 0:N_C]


def _rwkv(p3, shift_state, state, stack, layer, depth, mu, wwa, gup, vec, *,
          nbatch, nchunk, nseq, null_rows, ngroup):
    has_state = state is not None
    tail = (H_C, N_C, N_C)
    in_specs = [_row_spec(N_CC, 0, g, ngroup, nchunk) for g in range(ngroup)]
    args = [p3] * ngroup
    if has_state:
        in_specs.append(_layer_state_spec(layer, ngroup * nseq, (1, N_CC)))
        args.append(shift_state)
    in_specs += [
        pl.BlockSpec((1, N_CC), lambda b, c: (0, 0)),
        pl.BlockSpec((LANES, 2 * BRANCH_W), lambda b, c: (0, 0)),
        pl.BlockSpec((GATE_LORA, BRANCH_W), lambda b, c: (0, 0)),
        pl.BlockSpec((SUBLANES, BRANCH_W), lambda b, c: (0, 0)),
    ]
    args += [mu, wwa, gup, vec]
    if has_state:
        in_specs.append(_layer_state_spec(layer, ngroup * nseq, tail))
        args.append(state)
    stack_specs, stack_args, stack_shape, aliases = _stack_io(stack, depth, nbatch * nseq, tail, len(args))
    y_spec, y_shape = _y_io(nbatch, nchunk, ngroup, null_rows)
    return pl.pallas_call(
        functools.partial(_rwkv_body, ngroup=ngroup, nseq=nseq, null_rows=null_rows, has_state=has_state,
                          has_stack=stack is not None),
        grid=(nbatch // ngroup, nchunk),
        in_specs=in_specs + stack_specs,
        out_specs=[y_spec, _layer_state_spec(layer, ngroup * nseq, tail)],
        out_shape=[y_shape, stack_shape],
        input_output_aliases=aliases,
        scratch_shapes=[pltpu.VMEM((ngroup, CHUNK + SUBLANES, N_CC), F32),
                        pltpu.VMEM((ngroup * N_PAIR, 2 * N_C, 2 * N_C), F32)],
        compiler_params=_cparams(2),
        name="rwkv",
    )(*args, *stack_args)


def _pad_cols(w, n):
    return jnp.pad(w, ((0, 0),) * (w.ndim - 1) + ((0, n - w.shape[-1]),))


def _pad_heads(w):
    z = jnp.zeros(w.shape[:-1] + (LANES - DK_B,), w.dtype)
    parts = []
    for h in range(H_B):
        parts += [w[..., h * DK_B:(h + 1) * DK_B], z]
    return jnp.concatenate(parts, axis=-1)


def _prep_in(w_in):
    w = w_in.astype(BF16)
    wa = w[..., :N_A]
    wb = w[..., N_A:N_A + N_B]
    wc = w[..., N_A + N_B:N_A + N_B + N_CC]
    wg = w[..., N_A + N_B + N_CC:]
    w1 = wa[..., :QKV_A + H_A * DV_A]
    kq = H_B * DK_B
    w2 = jnp.concatenate([_pad_heads(wb[..., :kq]), _pad_heads(wb[..., kq:2 * kq]),
                          wb[..., 2 * kq:2 * kq + 2 * H_B * DV_B]], axis=-1)
    misc = jnp.concatenate([_pad_cols(wa[..., QKV_A + H_A * DV_A:], LANES),
                            _pad_cols(wb[..., 2 * kq + 2 * H_B * DV_B:], LANES)], axis=-1)
    w3 = jnp.concatenate([wc, misc], axis=-1)
    return w1, w2, w3, wg


def _prep_ffn(w_up, w_down):
    return w_up[..., :D_FF].astype(BF16), w_up[..., D_FF:].astype(BF16), w_down.astype(BF16)


def _tail_rows(p, nb, t, n_tail, width):
    return jnp.stack([p[(b + 1) * t - n_tail:(b + 1) * t, :width] for b in range(nb)])


def kernel(x_prompt, x_sample, state_gdn, state_gdn_conv, state_gla, state_rwkv, state_rwkv_shift, w_in, conv_a, a_log, dt_bias, gdn_norm, gla_gk_up, gla_gk_bias, gla_norm, rwkv_mu, rwkv_w0, rwkv_w_up, rwkv_a0, rwkv_a_up, rwkv_g_up, rwkv_k_k, rwkv_k_a, rwkv_r_k, rwkv_ln_w, rwkv_ln_b, w_branch, w_out, norm_ff1, w_ff1_up, w_ff1_down, norm_mix, norm_ff2, w_ff2_up, w_ff2_down, norm_final):
    bp, tp, d = x_prompt.shape
    bs, ts, _ = x_sample.shape
    depth = w_in.shape[0]
    n_p, n_s = bp * tp, bs * ts
    null_rows = SAMPLE_ROWS - ts
    assert tp % CHUNK == 0 and null_rows == SAMPLE_ROWS // 2 and CONV_W - 1 <= null_rows
    seq_per_chunk = CHUNK // SAMPLE_ROWS
    assert bs % seq_per_chunk == 0
    nchunk_p = tp // CHUNK
    nstep_s = bs // seq_per_chunk
    prompt = dict(nbatch=bp, nchunk=nchunk_p, nseq=1, null_rows=0, ngroup=math.gcd(bp, PROMPT_GROUPS))
    sample = dict(nbatch=nstep_s, nchunk=1, nseq=seq_per_chunk, null_rows=null_rows,
                  ngroup=math.gcd(nstep_s, SAMPLE_GROUPS))

    x = jnp.concatenate([x_prompt.reshape(n_p, d), x_sample.reshape(n_s, d)], axis=0)
    gdn_p = gdn_s = gla_p = gla_s = rwkv_p = rwkv_s = None
    conv_p, conv_s, shift_p, shift_s = [], [], [], []
    ff1 = _prep_ffn(w_ff1_up, w_ff1_down)
    ff2 = _prep_ffn(w_ff2_up, w_ff2_down)
    w1, w2, w3, wg = _prep_in(w_in)
    wbr = w_branch.astype(BF16)
    wout = w_out.astype(BF16)
    for l in range(depth):
        h = _ffn(x, norm_ff1[l], *ff1, l)

        hn = _norm_cast(h, norm_mix[l], BF16)
        p1, p1s = _proj(hn, w1, l, w1.shape[2] // 2, n_sample=n_s)
        p2, p2s = _proj(hn, w2, l, w2.shape[2] // 2, n_sample=n_s)
        p3, p3s = _proj(hn, w3, l, w3.shape[2] // 2, n_sample=n_s)
        pg = _proj(hn, wg, l, wg.shape[2] // 4, BF16)

        prm = jnp.zeros((SUBLANES, LANES), F32)
        prm = prm.at[0, H_A:2 * H_A].set(a_log[l]).at[1, H_A:2 * H_A].set(dt_bias[l])
        gnorm = gdn_norm[l].reshape(1, DV_A)
        up_w = jnp.pad(gla_gk_up[l].reshape(GLA_RANK, H_B, DK_B),
                       ((0, LANES - GLA_RANK), (0, 0), (0, LANES - DK_B))).reshape(LANES, H_B * LANES).astype(BF16)
        gk_bias = jnp.pad(gla_gk_bias[l].reshape(H_B, DK_B), ((0, 0), (0, LANES - DK_B))).reshape(1, H_B * LANES)
        lnorm = gla_norm[l].reshape(1, DV_B)
        mu = rwkv_mu[l].reshape(1, N_CC)
        wwa = jnp.zeros((LANES, 2 * BRANCH_W), F32)
        wwa = wwa.at[0:DECAY_LORA, 0:BRANCH_W].set(rwkv_w_up[l])
        wwa = wwa.at[DECAY_LORA:DECAY_LORA + AAA_LORA, BRANCH_W:].set(rwkv_a_up[l]).astype(BF16)
        gup = rwkv_g_up[l].astype(BF16)
        vec = jnp.stack([rwkv_w0[l], rwkv_a0[l], rwkv_k_k[l], rwkv_k_a[l], rwkv_r_k[l],
                         rwkv_ln_w[l], rwkv_ln_b[l], jnp.zeros((BRANCH_W,), F32)], axis=0)

        ya_p, gdn_p = _gdn(p1, p3, None, None, gdn_p, l, depth, conv_a[l], prm, gnorm, **prompt)
        ya_s, gdn_s = _gdn(p1s, p3s, state_gdn_conv, state_gdn, gdn_s, l, depth, conv_a[l], prm, gnorm, **sample)
        yb_p, gla_p = _gla(p2, p3, None, gla_p, l, depth, up_w, gk_bias, lnorm, **prompt)
        yb_s, gla_s = _gla(p2s, p3s, state_gla, gla_s, l, depth, up_w, gk_bias, lnorm, **sample)
        yc_p, rwkv_p = _rwkv(p3, None, None, rwkv_p, l, depth, mu, wwa, gup, vec, **prompt)
        yc_s, rwkv_s = _rwkv(p3s, state_rwkv_shift, state_rwkv, rwkv_s, l, depth, mu, wwa, gup, vec, **sample)

        def join(y_p, y_s):
            return jnp.concatenate([y_p.reshape(n_p, BRANCH_W), y_s.reshape(n_s, BRANCH_W)], axis=0)

        h = _merge(join(ya_p, ya_s), join(yb_p, yb_s), join(yc_p, yc_s), pg, wbr, wout, h, l)
        x = _ffn(h, norm_ff2[l], *ff2, l)

        conv_p.append(_tail_rows(p1, bp, tp, CONV_W - 1, QKV_A))
        conv_s.append(p1[n_p:].reshape(bs, ts, -1)[:, ts - (CONV_W - 1):, :QKV_A])
        shift_p.append(_tail_rows(p3, bp, tp, 1, N_CC))
        shift_s.append(p3[n_p:].reshape(bs, ts, -1)[:, ts - 1:, :N_CC])

    y_p = _norm_cast(x, norm_final, F32, 0, n_p)
    y_s = _norm_cast(x, norm_final, F32, n_p, n_s)
    return (y_p.reshape(bp, tp, d), y_s.reshape(bs, ts, d),
            gdn_p, gdn_s, jnp.stack(conv_p), jnp.stack(conv_s), gla_p, gla_s,
            rwkv_p, rwkv_s, jnp.stack(shift_p), jnp.stack(shift_s))
```

```python
import functools
import math

import jax
import jax.numpy as jnp
from jax import lax
from jax.experimental import pallas as pl
from jax.experimental.pallas import tpu as pltpu

F32 = jnp.float32
BF16 = jnp.bfloat16

D_MODEL = 2048
N_BRANCH = 3
BRANCH_W = 768
DK_A, DV_A, H_A, CONV_W = 128, 128, 6, 4
DK_B, DV_B, H_B, GLA_RANK, GLA_NORMALIZER = 64, 128, 6, 16, 16.0
N_C, H_C, DECAY_LORA, AAA_LORA, GATE_LORA = 64, 12, 64, 64, 128
GN_EPS = 64e-5
D_FF = 5504
NORM_EPS = 1e-6
QKV_A = 2 * H_A * DK_A + H_A * DV_A
N_A = QKV_A + H_A * DV_A + 2 * H_A
N_B = 2 * H_B * DK_B + 2 * H_B * DV_B + GLA_RANK
N_CC = 3 * BRANCH_W + DECAY_LORA + AAA_LORA + GATE_LORA
N_PAIR = H_C // 2

LANES = 128
SUBLANES = 8
CHUNK = 64
PROMPT_GROUPS = 4
SAMPLE_GROUPS = 2
SAMPLE_ROWS = 8
FF_TILE = 512
VMEM_LIMIT = 60 * 1024 * 1024


def _cparams(n_axes):
    return pltpu.CompilerParams(dimension_semantics=("arbitrary",) * n_axes,
                                vmem_limit_bytes=VMEM_LIMIT)


def _pick_tile(n, prefs):
    for t in prefs:
        if n % t == 0:
            return t
    raise ValueError(f"no tile for {n}")


def _bdot(a, b):
    return jnp.dot(a.astype(BF16), b.astype(BF16), preferred_element_type=F32)


def _bdot_nt(a, b):
    return lax.dot_general(a.astype(BF16), b.astype(BF16), (((1,), (1,)), ((), ())),
                           preferred_element_type=F32)


def _bdot_tn(a, b):
    return lax.dot_general(a.astype(BF16), b.astype(BF16), (((0,), (0,)), ((), ())),
                           preferred_element_type=F32)


def _sigmoid(x):
    return jax.nn.sigmoid(x)


def _silu(x):
    return x * _sigmoid(x)


def _softplus(x):
    return jnp.maximum(x, 0.0) + jnp.log(1.0 + jnp.exp(-jnp.abs(x)))


def _rmsnorm_rows(x, g):
    return x * lax.rsqrt(jnp.mean(x * x, axis=-1, keepdims=True) + NORM_EPS) * g


def _chunk_masks(rows, seq_len):
    sh = int(math.log2(seq_len))
    ri = lax.broadcasted_iota(jnp.int32, (rows, rows), 0)
    ci = lax.broadcasted_iota(jnp.int32, (rows, rows), 1)
    same = (ri >> sh) == (ci >> sh)
    return same & (ri >= ci), same & (ri > ci), same


def _eye(rows):
    return (lax.broadcasted_iota(jnp.int32, (rows, rows), 0)
            == lax.broadcasted_iota(jnp.int32, (rows, rows), 1)).astype(F32)


def _inv_unit_lower(lms, eye, nil):
    xs = [eye - lm for lm in lms]
    ps = list(lms)
    k = 2
    while k < nil:
        ps = [_bdot(p, p) for p in ps]
        xs = [x + _bdot(x, p) for x, p in zip(xs, ps)]
        k *= 2
    return xs


def _row_valid(rows, seq_len, null_rows, width):
    r = lax.broadcasted_iota(jnp.int32, (rows, width), 0)
    return (r & (seq_len - 1)) >= null_rows


def _seq_rows(rows, seq_len, s, width):
    r = lax.broadcasted_iota(jnp.int32, (rows, width), 0) & (CHUNK - 1)
    return (r >> int(math.log2(seq_len))) == s


def _pick(sel, xv):
    return xv if sel is None else jnp.where(sel, xv, 0.0)


def _ffn_body(x_ref, g_ref, wg_ref, wu_ref, wd_ref, o_ref, xn_ref, *, d_ff):
    j = pl.program_id(1)

    @pl.when(j == 0)
    def _():
        xn_ref[...] = _rmsnorm_rows(x_ref[...], g_ref[...]).astype(BF16)
        o_ref[...] = jnp.zeros_like(o_ref)

    valid = d_ff - j * FF_TILE
    col_ok = lax.broadcasted_iota(jnp.int32, (1, FF_TILE), 1) < valid
    row_ok = lax.broadcasted_iota(jnp.int32, (FF_TILE, 1), 0) < valid
    xn = xn_ref[...]
    gate = jnp.dot(xn, wg_ref[...], preferred_element_type=F32)
    up = jnp.dot(xn, wu_ref[...], preferred_element_type=F32)
    act = jnp.where(col_ok, _silu(gate) * up, 0.0).astype(BF16)
    wd = jnp.where(row_ok, wd_ref[...], jnp.zeros((), BF16))
    o_ref[...] += jnp.dot(act, wd, preferred_element_type=F32)

    @pl.when(j == pl.num_programs(1) - 1)
    def _():
        o_ref[...] = x_ref[...] + 0.5 * o_ref[...]


def _ffn(x, g, wg, wu, wd, layer):
    n, d = x.shape
    tm = _pick_tile(n, (1088, 544, 512, 256, 128, 64, 32, 16, 8))
    d_ff = wg.shape[2]
    nf = pl.cdiv(d_ff, FF_TILE)
    return pl.pallas_call(
        functools.partial(_ffn_body, d_ff=d_ff),
        grid=(n // tm, nf),
        in_specs=[
            pl.BlockSpec((tm, d), lambda i, j: (i, 0)),
            pl.BlockSpec((1, d), lambda i, j: (0, 0)),
            pl.BlockSpec((None, d, FF_TILE), lambda i, j: (layer, 0, j)),
            pl.BlockSpec((None, d, FF_TILE), lambda i, j: (layer, 0, j)),
            pl.BlockSpec((None, FF_TILE, d), lambda i, j: (layer, j, 0)),
        ],
        out_specs=pl.BlockSpec((tm, d), lambda i, j: (i, 0)),
        out_shape=jax.ShapeDtypeStruct((n, d), F32),
        scratch_shapes=[pltpu.VMEM((tm, d), BF16)],
        compiler_params=_cparams(2),
        name="ffn",
    )(x, g.reshape(1, d), wg, wu, wd)


def _norm_cast_body(x_ref, g_ref, o_ref):
    o_ref[...] = _rmsnorm_rows(x_ref[...], g_ref[...]).astype(o_ref.dtype)


def _norm_cast(x, g, dtype, start=0, count=None):
    n, d = x.shape
    count = n - start if count is None else count
    tm = _pick_tile(math.gcd(start, count), (544, 512, 256, 128, 64, 32, 16, 8))
    first = start // tm
    return pl.pallas_call(
        _norm_cast_body,
        grid=(count // tm,),
        in_specs=[pl.BlockSpec((tm, d), lambda i: (first + i, 0)), pl.BlockSpec((1, d), lambda i: (0, 0))],
        out_specs=pl.BlockSpec((tm, d), lambda i: (i, 0)),
        out_shape=jax.ShapeDtypeStruct((count, d), dtype),
        compiler_params=_cparams(1),
        name="norm",
    )(x, g.reshape(1, d))


def _pad_rows(x):
    n, w = x.shape
    x3 = x.reshape(n // SUBLANES, SUBLANES, w)
    tail = lax.broadcasted_iota(jnp.int32, x3.shape, 1) >= SUBLANES // 2
    first = jnp.where(tail, pltpu.roll(x3, SUBLANES // 2, 1), 0.0)
    second = jnp.where(tail, x3, 0.0)
    return jnp.stack([first, second], axis=1).reshape(2 * n, w)


def _compact_rows(x):
    n, w = x.shape
    x4 = x.reshape(n // (2 * SUBLANES), 2, SUBLANES, w)
    head = lax.broadcasted_iota(jnp.int32, (n // (2 * SUBLANES), SUBLANES, w), 1) < SUBLANES // 2
    return jnp.where(head, pltpu.roll(x4[:, 0], SUBLANES // 2, 1), x4[:, 1]).reshape(n // 2, w)


def _proj_body(x_ref, w_ref, o_ref, *pad_ref, sample_off, n_sample):
    res = jnp.dot(x_ref[...], w_ref[...], preferred_element_type=F32)
    o_ref[...] = res.astype(o_ref.dtype)
    if pad_ref:
        @pl.when(pl.program_id(1) == pl.num_programs(1) - 1)
        def _():
            pad_ref[0][...] = _pad_rows(res[sample_off:sample_off + n_sample])


def _proj(xn, w, layer, tn, out_dtype=F32, n_sample=0):
    n, d = xn.shape
    nn = w.shape[2]
    tm = _pick_tile(n, (1088, 544, 512, 256, 128, 64, 32, 16))
    out_specs = [pl.BlockSpec((tm, tn), lambda j, i: (i, j))]
    out_shape = [jax.ShapeDtypeStruct((n, nn), out_dtype)]
    sample_off = tm - n_sample
    if n_sample:
        assert 0 <= sample_off and sample_off % SUBLANES == 0 and n_sample % SUBLANES == 0
        out_specs.append(pl.BlockSpec((2 * n_sample, tn), lambda j, i: (0, j)))
        out_shape.append(jax.ShapeDtypeStruct((2 * n_sample, nn), F32))
    out = pl.pallas_call(
        functools.partial(_proj_body, sample_off=sample_off, n_sample=n_sample),
        grid=(nn // tn, n // tm),
        in_specs=[
            pl.BlockSpec((tm, d), lambda j, i: (i, 0)),
            pl.BlockSpec((None, d, tn), lambda j, i: (layer, 0, j)),
        ],
        out_specs=out_specs,
        out_shape=out_shape,
        compiler_params=_cparams(2),
        name="proj",
    )(xn, w)
    return out if n_sample else out[0]


MERGE_TN = 1024


def _merge_body(yap_ref, ybp_ref, ycp_ref, yas_ref, ybs_ref, ycs_ref, ga_ref, gb_ref, gc_ref,
                wb_ref, wo_ref, h_ref, o_ref, *, prompt_tiles):
    i = pl.program_id(0)
    j = pl.program_id(1)

    @pl.when(j == 0)
    def _():
        o_ref[...] = h_ref[...]

    is_prompt = i < prompt_tiles
    ya = jnp.where(is_prompt, yap_ref[...], yas_ref[...])
    yb = jnp.where(is_prompt, ybp_ref[...], ybs_ref[...])
    yc = jnp.where(is_prompt, ycp_ref[...], ycs_ref[...])
    m = _sigmoid(ga_ref[...].astype(F32)) * jnp.dot(ya, wb_ref[0], preferred_element_type=F32)
    m += _sigmoid(gb_ref[...].astype(F32)) * jnp.dot(yb, wb_ref[1], preferred_element_type=F32)
    m += _sigmoid(gc_ref[...].astype(F32)) * jnp.dot(yc, wb_ref[2], preferred_element_type=F32)
    o_ref[...] += jnp.dot(m.astype(BF16), wo_ref[...], preferred_element_type=F32)


def _merge(ys_prompt, ys_sample, pg, wb, wo, h, layer):
    n, d = h.shape
    n_p, n_s = ys_prompt[0].shape[0], ys_sample[0].shape[0]
    tm = _pick_tile(math.gcd(n_p, n_s), (512, 256, 128, 64, 32, 16))
    prompt_tiles = n_p // tm
    nj = d // MERGE_TN
    yp_spec = pl.BlockSpec((tm, BRANCH_W), lambda i, j: (jnp.minimum(i, prompt_tiles - 1), 0))
    ys_spec = pl.BlockSpec((tm, BRANCH_W), lambda i, j: (jnp.maximum(i - prompt_tiles, 0), 0))
    return pl.pallas_call(
        functools.partial(_merge_body, prompt_tiles=prompt_tiles),
        grid=(n // tm, nj),
        in_specs=[
            yp_spec, yp_spec, yp_spec, ys_spec, ys_spec, ys_spec,
            pl.BlockSpec((tm, MERGE_TN), lambda i, j: (i, j)),
            pl.BlockSpec((tm, MERGE_TN), lambda i, j: (i, nj + j)),
            pl.BlockSpec((tm, MERGE_TN), lambda i, j: (i, 2 * nj + j)),
            pl.BlockSpec((None, N_BRANCH, BRANCH_W, MERGE_TN), lambda i, j: (layer, 0, 0, j)),
            pl.BlockSpec((None, MERGE_TN, d), lambda i, j: (layer, j, 0)),
            pl.BlockSpec((tm, d), lambda i, j: (i, 0)),
        ],
        out_specs=pl.BlockSpec((tm, d), lambda i, j: (i, 0)),
        out_shape=jax.ShapeDtypeStruct((n, d), F32),
        compiler_params=_cparams(2),
        name="merge",
    )(*ys_prompt, *ys_sample, pg, pg, pg, wb, wo, h)


def _layer_state_spec(layer, nseq, tail):
    zeros = (0,) * len(tail)
    return pl.BlockSpec((None, nseq) + tail, lambda b, c: (layer, b) + zeros)


def _stack_io(stack, depth, nstate, tail, n_args):
    shape = jax.ShapeDtypeStruct((depth, nstate) + tail, F32)
    if stack is None:
        return [], [], shape, {}
    return [pl.BlockSpec(memory_space=pl.ANY)], [stack], shape, {n_args: 1}


def _split3_dot(lmat, x):
    lb = lmat.astype(BF16)
    hi = x.astype(BF16)
    r1 = x - hi.astype(F32)
    mid = r1.astype(BF16)
    lo = (r1 - mid.astype(F32)).astype(BF16)
    return (jnp.dot(lb, hi, preferred_element_type=F32) + jnp.dot(lb, mid, preferred_element_type=F32)
            + jnp.dot(lb, lo, preferred_element_type=F32))


def _gdn_body(*refs, ngroup, nseq, null_rows, has_state, has_stack):
    refs = list(refs)
    xbuf = refs.pop()
    so_ref = refs.pop()
    y_ref = refs.pop()
    if has_stack:
        refs.pop()
    row_refs = [refs[3 * g:3 * g + 3] for g in range(ngroup)]
    refs = refs[3 * ngroup:]
    if has_state:
        cs_ref, cw_ref, prm_ref, nw_ref, s0_ref = refs
    else:
        cw_ref, prm_ref, nw_ref = refs
        s0_ref = so_ref
    rows = CHUNK
    seq_len = rows // nseq
    c = pl.program_id(1)
    chains = [(g, h) for g in range(ngroup) for h in range(H_A)]

    @pl.when(c == 0)
    def _():
        for g in range(ngroup):
            xbuf[g, 0:SUBLANES, :] = jnp.zeros((SUBLANES, QKV_A), F32)
        if not has_state:
            so_ref[...] = jnp.zeros_like(so_ref)

    @pl.when(c > 0)
    def _():
        for g in range(ngroup):
            xbuf[g, 0:SUBLANES, :] = xbuf[g, rows:rows + SUBLANES, :]

    for g in range(ngroup):
        xbuf[g, SUBLANES:SUBLANES + rows, :] = row_refs[g][0][...]
        if has_state:
            for s in range(nseq):
                r0 = SUBLANES + s * seq_len + null_rows - (CONV_W - 1)
                xbuf[g, r0:r0 + CONV_W - 1, :] = cs_ref[g * nseq + s]

    incl, strict, same = _chunk_masks(rows, seq_len)
    eye = _eye(rows)
    valid = _row_valid(rows, seq_len, null_rows, LANES) if null_rows else None
    sels = [_seq_rows(rows, seq_len, s, LANES) for s in range(nseq)] if nseq > 1 else [None]
    sels2 = [_seq_rows(2 * rows, seq_len, s, LANES) for s in range(nseq)] if nseq > 1 else [None]
    lstack = jnp.concatenate([incl.astype(F32), same.astype(F32)], axis=0)

    def conv_cols(g, c0):
        acc = xbuf[g, SUBLANES - 3:SUBLANES - 3 + rows, c0:c0 + LANES] * cw_ref[0:1, c0:c0 + LANES]
        for j in range(1, CONV_W):
            acc += (xbuf[g, SUBLANES - 3 + j:SUBLANES - 3 + j + rows, c0:c0 + LANES]
                    * cw_ref[j:j + 1, c0:c0 + LANES])
        return _silu(acc)

    qs, ks, kbs, rhs, decays, wq_parts, kds, gtcs = [], [], [], [], [], [], [], []
    for g in range(ngroup):
        ba = row_refs[g][2][...]
        beta_all = _sigmoid(ba)
        g_all = -jnp.exp(prm_ref[0:1, :]) * _softplus(ba + prm_ref[1:2, :])
        if null_rows:
            g_all = jnp.where(valid, g_all, 0.0)
        cs = _split3_dot(lstack, g_all)
        gc_all, gt_all = cs[0:rows], cs[rows:2 * rows]
        gc_t = gc_all.T
        for h in range(H_A):
            q = conv_cols(g, h * DK_A)
            k = conv_cols(g, H_A * DK_A + h * DK_A)
            v = conv_cols(g, 2 * H_A * DK_A + h * DV_A)
            q = q * lax.rsqrt(jnp.sum(q * q, axis=-1, keepdims=True) + 1e-6) * DK_A ** -0.5
            k = k * lax.rsqrt(jnp.sum(k * k, axis=-1, keepdims=True) + 1e-6)
            if null_rows:
                q = jnp.where(valid, q, 0.0)
                k = jnp.where(valid, k, 0.0)
                v = jnp.where(valid, v, 0.0)
            beta = beta_all[:, h:h + 1]
            gcc = gc_all[:, H_A + h:H_A + h + 1]
            gtc = gt_all[:, H_A + h:H_A + h + 1]
            gcr = gc_t[H_A + h:H_A + h + 1, :]
            egc = jnp.exp(gcc)
            kb = k * beta
            qs.append(q)
            ks.append(k)
            kbs.append(kb)
            rhs.append(jnp.concatenate([v * beta, kb * egc], axis=1))
            decays.append(jnp.where(incl, jnp.exp(jnp.where(incl, gcc - gcr, 0.0)), 0.0))
            wq_parts.append(q * egc)
            kds.append(k * jnp.exp(gtc - gcc))
            gtcs.append(gtc)

    n = range(len(chains))
    lms = [jnp.where(strict, _bdot_nt(kbs[i], ks[i]) * decays[i], 0.0) for i in n]
    attns = [_bdot_nt(qs[i], ks[i]) * decays[i] for i in n]
    tmats = _inv_unit_lower(lms, eye, seq_len)
    uws = [_bdot(tmats[i], rhs[i]) for i in n]
    us, os_ = [], []
    for i, (g, h) in enumerate(chains):
        u = uws[i][:, 0:DV_A]
        wq = jnp.concatenate([uws[i][:, DV_A:DV_A + DK_A], wq_parts[i]], axis=0)
        o = None
        for s in range(nseq):
            both = _bdot(_pick(sels2[s], wq), s0_ref[g * nseq + s, h])
            u = u - both[0:rows]
            o = both[rows:2 * rows] if o is None else o + both[rows:2 * rows]
        us.append(u)
        os_.append(o)
    os_ = [os_[i] + _bdot(attns[i], us[i]) for i in n]
    for i, (g, h) in enumerate(chains):
        for s in range(nseq):
            g_last = jnp.exp(gtcs[i][s * seq_len:s * seq_len + 1, :])
            so_ref[g * nseq + s, h] = (g_last * s0_ref[g * nseq + s, h]
                                       + _bdot_tn(_pick(sels[s], kds[i]), us[i]))
    for i, (g, h) in enumerate(chains):
        z = row_refs[g][1][:, h * DV_A:(h + 1) * DV_A]
        _store_y(y_ref, g, h * DV_A, _rmsnorm_rows(os_[i], nw_ref[...]) * _silu(z), null_rows)


def _row_spec(width, col, g, ngroup, nchunk):
    return pl.BlockSpec((CHUNK, width), lambda b, c: ((b * ngroup + g) * nchunk + c, col))


def _y_io(nbatch, nchunk, ngroup, null_rows):
    out_rows = CHUNK // 2 if null_rows else CHUNK
    return (pl.BlockSpec((ngroup, out_rows, BRANCH_W), lambda b, c: (b, c, 0)),
            jax.ShapeDtypeStruct((nbatch, nchunk * out_rows, BRANCH_W), BF16))


def _store_y(y_ref, g, c0, yv, null_rows):
    if null_rows:
        yv = _compact_rows(yv)
    y_ref[g, :, c0:c0 + yv.shape[1]] = yv.astype(y_ref.dtype)


def _gdn(p1, p3, conv_state, state, stack, layer, depth, conv_w, prm, norm_w, *,
         nbatch, nchunk, nseq, null_rows, ngroup):
    has_state = state is not None
    tail = (H_A, DK_A, DV_A)
    in_specs, args = [], []
    for g in range(ngroup):
        in_specs += [_row_spec(QKV_A, 0, g, ngroup, nchunk),
                     _row_spec(BRANCH_W, QKV_A // BRANCH_W, g, ngroup, nchunk),
                     _row_spec(LANES, N_CC // LANES, g, ngroup, nchunk)]
        args += [p1, p1, p3]
    if has_state:
        in_specs.append(_layer_state_spec(layer, ngroup * nseq, (CONV_W - 1, QKV_A)))
        args.append(conv_state)
    in_specs += [
        pl.BlockSpec((CONV_W, QKV_A), lambda b, c: (0, 0)),
        pl.BlockSpec((SUBLANES, LANES), lambda b, c: (0, 0)),
        pl.BlockSpec((1, DV_A), lambda b, c: (0, 0)),
    ]
    args += [conv_w, prm, norm_w]
    if has_state:
        in_specs.append(_layer_state_spec(layer, ngroup * nseq, tail))
        args.append(state)
    stack_specs, stack_args, stack_shape, aliases = _stack_io(stack, depth, nbatch * nseq, tail, len(args))
    y_spec, y_shape = _y_io(nbatch, nchunk, ngroup, null_rows)
    return pl.pallas_call(
        functools.partial(_gdn_body, ngroup=ngroup, nseq=nseq, null_rows=null_rows, has_state=has_state,
                          has_stack=stack is not None),
        grid=(nbatch // ngroup, nchunk),
        in_specs=in_specs + stack_specs,
        out_specs=[y_spec, _layer_state_spec(layer, ngroup * nseq, tail)],
        out_shape=[y_shape, stack_shape],
        input_output_aliases=aliases,
        scratch_shapes=[pltpu.VMEM((ngroup, CHUNK + SUBLANES, QKV_A), F32)],
        compiler_params=_cparams(2),
        name="gdn",
    )(*args, *stack_args)


def _gla_body(*refs, ngroup, nseq, null_rows, has_state, has_stack):
    refs = list(refs)
    so_ref = refs.pop()
    y_ref = refs.pop()
    if has_stack:
        refs.pop()
    row_refs = [refs[5 * g:5 * g + 5] for g in range(ngroup)]
    refs = refs[5 * ngroup:]
    if has_state:
        up_ref, bias_ref, nw_ref, s0_ref = refs
    else:
        up_ref, bias_ref, nw_ref = refs
        s0_ref = so_ref
    rows = CHUNK
    seq_len = rows // nseq
    c = pl.program_id(1)
    chains = [(g, h) for g in range(ngroup) for h in range(H_B)]
    if not has_state:
        @pl.when(c == 0)
        def _():
            so_ref[...] = jnp.zeros_like(so_ref)

    incl, _, same = _chunk_masks(rows, seq_len)
    ci = lax.broadcasted_iota(jnp.int32, (rows, rows), 1)
    first_half = same & ((ci & (seq_len - 1)) < seq_len // 2)
    wide = H_B * DK_B
    lstack = jnp.concatenate([incl.astype(F32), same.astype(F32), first_half.astype(F32)], axis=0)
    valid_w = _row_valid(rows, seq_len, null_rows, wide) if null_rows else None
    valid_k = _row_valid(rows, seq_len, null_rows, DK_B) if null_rows else None
    valid_v = _row_valid(rows, seq_len, null_rows, DV_B) if null_rows else None
    sels = [_seq_rows(rows, seq_len, s, DK_B) for s in range(nseq)] if nseq > 1 else [None]

    qis, kis, vs, q_ins, k_decs, a_lasts = [], [], [], [], [], []
    for g in range(ngroup):
        q_ref, k_ref, v_ref, _, gkl_ref = row_refs[g]
        x = _bdot(gkl_ref[...], up_ref[...]) + bias_ref[...]
        gk_all = (jnp.minimum(x, 0.0) - jnp.log(1.0 + jnp.exp(-jnp.abs(x)))) / GLA_NORMALIZER
        if null_rows:
            gk_all = jnp.where(valid_w, gk_all, 0.0)
        cs = _split3_dot(lstack, gk_all)
        bc_all, bt_all, an_all = cs[0:rows], cs[rows:2 * rows], cs[2 * rows:3 * rows]
        bt_t = bt_all.T
        for h in range(H_B):
            c0 = h * DK_B
            q = q_ref[:, c0:c0 + DK_B] * DK_B ** -0.5
            k = k_ref[:, c0:c0 + DK_B]
            v = v_ref[:, h * DV_B:(h + 1) * DV_B]
            if null_rows:
                q = jnp.where(valid_k, q, 0.0)
                k = jnp.where(valid_k, k, 0.0)
                v = jnp.where(valid_v, v, 0.0)
            bc = bc_all[:, c0:c0 + DK_B]
            bt = bt_all[:, c0:c0 + DK_B]
            an = an_all[:, c0:c0 + DK_B]
            qis.append(q * jnp.exp(bc - an))
            kis.append(k * jnp.exp(an - bc))
            vs.append(v)
            q_ins.append(q * jnp.exp(bc))
            k_decs.append(k * jnp.exp(bt - bc))
            a_lasts.append([jnp.exp(bt_t[c0:c0 + DK_B, s * seq_len:s * seq_len + 1])
                            for s in range(nseq)])

    n = range(len(chains))
    attns = [jnp.where(incl, _bdot_nt(qis[i], kis[i]), 0.0) for i in n]
    os_ = [_bdot(attns[i], vs[i]) for i in n]
    for i, (g, h) in enumerate(chains):
        for s in range(nseq):
            os_[i] = os_[i] + _bdot(_pick(sels[s], q_ins[i]), s0_ref[g * nseq + s, h])
    for i, (g, h) in enumerate(chains):
        for s in range(nseq):
            so_ref[g * nseq + s, h] = (a_lasts[i][s] * s0_ref[g * nseq + s, h]
                                       + _bdot_tn(_pick(sels[s], k_decs[i]), vs[i]))
    for i, (g, h) in enumerate(chains):
        gate = row_refs[g][3][:, h * DV_B:(h + 1) * DV_B]
        _store_y(y_ref, g, h * DV_B, _rmsnorm_rows(os_[i], nw_ref[...]) * _silu(gate), null_rows)


def _gla(p2, p3, state, stack, layer, depth, up_w, bias, norm_w, *, nbatch, nchunk, nseq, null_rows, ngroup):
    has_state = state is not None
    wide = H_B * DK_B
    tail = (H_B, DK_B, DV_B)
    in_specs, args = [], []
    for g in range(ngroup):
        in_specs += [_row_spec(wide, 0, g, ngroup, nchunk), _row_spec(wide, 1, g, ngroup, nchunk),
                     _row_spec(BRANCH_W, 1, g, ngroup, nchunk), _row_spec(BRANCH_W, 2, g, ngroup, nchunk)]
        in_specs.append(_row_spec(LANES, N_CC // LANES + 1, g, ngroup, nchunk))
        args += [p2, p2, p2, p2, p3]
    in_specs += [pl.BlockSpec((LANES, wide), lambda b, c: (0, 0)),
                 pl.BlockSpec((1, wide), lambda b, c: (0, 0)),
                 pl.BlockSpec((1, DV_B), lambda b, c: (0, 0))]
    args += [up_w, bias, norm_w]
    if has_state:
        in_specs.append(_layer_state_spec(layer, ngroup * nseq, tail))
        args.append(state)
    stack_specs, stack_args, stack_shape, aliases = _stack_io(stack, depth, nbatch * nseq, tail, len(args))
    y_spec, y_shape = _y_io(nbatch, nchunk, ngroup, null_rows)
    return pl.pallas_call(
        functools.partial(_gla_body, ngroup=ngroup, nseq=nseq, null_rows=null_rows, has_state=has_state,
                          has_stack=stack is not None),
        grid=(nbatch // ngroup, nchunk),
        in_specs=in_specs + stack_specs,
        out_specs=[y_spec, _layer_state_spec(layer, ngroup * nseq, tail)],
        out_shape=[y_shape, stack_shape],
        input_output_aliases=aliases,
        compiler_params=_cparams(2),
        name="gla",
    )(*args, *stack_args)


def _rwkv_body(*refs, ngroup, nseq, null_rows, has_state, has_stack):
    refs = list(refs)
    sp_ref = refs.pop()
    pbuf = refs.pop()
    so_ref = refs.pop()
    y_ref = refs.pop()
    if has_stack:
        refs.pop()
    pc_refs = refs[:ngroup]
    refs = refs[ngroup:]
    if has_state:
        sh_ref, mu_ref, wwa_ref, gup_ref, vec_ref, s0_ref = refs
    else:
        mu_ref, wwa_ref, gup_ref, vec_ref = refs
    rows = CHUNK
    seq_len = rows // nseq
    c = pl.program_id(1)
    chains = [(g, p) for g in range(ngroup) for p in range(N_PAIR)]
    pairs = range(len(chains))

    @pl.when(c == 0)
    def _():
        for g in range(ngroup):
            pbuf[g, 0:SUBLANES, :] = jnp.zeros((SUBLANES, N_CC), F32)
        if not has_state:
            sp_ref[...] = jnp.zeros_like(sp_ref)

    @pl.when(c > 0)
    def _():
        for g in range(ngroup):
            pbuf[g, 0:SUBLANES, :] = pbuf[g, rows:rows + SUBLANES, :]

    for g in range(ngroup):
        pbuf[g, SUBLANES:SUBLANES + rows, :] = pc_refs[g][...]
        if has_state:
            for s in range(nseq):
                r0 = SUBLANES + s * seq_len + null_rows - 1
                pbuf[g, r0:r0 + 1, :] = sh_ref[g * nseq + s]

    def xc_cols(g, c0, w):
        cur = pbuf[g, SUBLANES:SUBLANES + rows, c0:c0 + w]
        prev = pbuf[g, SUBLANES - 1:SUBLANES - 1 + rows, c0:c0 + w]
        return cur + (prev - cur) * mu_ref[:, c0:c0 + w]

    incl, strict, same = _chunk_masks(rows, seq_len)
    eye = _eye(rows)
    lane = lax.broadcasted_iota(jnp.int32, (rows, LANES), 1)
    lo = lane < N_C
    lo2 = lax.broadcasted_iota(jnp.int32, (2 * rows, LANES), 1) < N_C
    valid = _row_valid(rows, seq_len, null_rows, LANES) if null_rows else None
    valid_w = _row_valid(rows, seq_len, null_rows, BRANCH_W) if null_rows else None
    sels2 = [_seq_rows(2 * rows, seq_len, s, LANES) for s in range(nseq)] if nseq > 1 else [None]
    lstack = jnp.concatenate([incl.astype(F32), same.astype(F32)], axis=0)
    w0 = vec_ref[0:1, :]
    a0 = vec_ref[1:2, :]

    r2 = lax.broadcasted_iota(jnp.int32, (2 * N_C, LANES), 0)
    c2 = lax.broadcasted_iota(jnp.int32, (2 * N_C, LANES), 1)
    blockdiag = (r2 >= N_C) == (c2 >= N_C)

    def seg_sum(xv):
        s_lo = jnp.sum(jnp.where(lo, xv, 0.0), axis=-1, keepdims=True)
        s_hi = jnp.sum(jnp.where(lo, 0.0, xv), axis=-1, keepdims=True)
        return jnp.where(lo, s_lo, s_hi)

    def by_head(stacked):
        return jnp.where(lo, stacked[0:rows], stacked[rows:2 * rows])

    rs, k2s, vs, ar_stacks, b_ts, k_ts, tots, gates = [], [], [], [], [], [], [], []
    for g in range(ngroup):
        lora = xc_cols(g, 3 * BRANCH_W, DECAY_LORA + AAA_LORA + GATE_LORA)
        wa_in = lora[:, 0:LANES]
        wa_in = jnp.where(lo, jnp.tanh(wa_in), wa_in)
        wa = _bdot(wa_in, wwa_ref[...])
        g_all = _bdot(_sigmoid(lora[:, LANES:2 * LANES]), gup_ref[...])
        w_log = -_softplus(-(w0 + wa[:, 0:BRANCH_W])) - 0.5
        lw_all = -jnp.exp(w_log)
        a_all = _sigmoid(a0 + wa[:, BRANCH_W:2 * BRANCH_W])
        if null_rows:
            lw_all = jnp.where(valid_w, lw_all, 0.0)
        cs = _split3_dot(lstack, lw_all)
        cum_all, tot_all = cs[0:rows], cs[rows:2 * rows]
        for p in range(N_PAIR):
            c0 = p * LANES
            r = xc_cols(g, c0, LANES)
            k = xc_cols(g, BRANCH_W + c0, LANES)
            v = xc_cols(g, 2 * BRANCH_W + c0, LANES)
            a_p = a_all[:, c0:c0 + LANES]
            kkr = k * vec_ref[2:3, c0:c0 + LANES]
            kk = kkr * lax.rsqrt(seg_sum(kkr * kkr) + 1e-6)
            k2 = k * (1.0 + (a_p - 1.0) * vec_ref[3:4, c0:c0 + LANES])
            av = -kk
            bv = kk * a_p
            if null_rows:
                r = jnp.where(valid, r, 0.0)
                k2 = jnp.where(valid, k2, 0.0)
                v = jnp.where(valid, v, 0.0)
                av = jnp.where(valid, av, 0.0)
                bv = jnp.where(valid, bv, 0.0)
            cum = cum_all[:, c0:c0 + LANES]
            lw = lw_all[:, c0:c0 + LANES]
            g_inv = jnp.exp(-cum)
            rs.append(r)
            k2s.append(k2)
            vs.append(v)
            ar_stacks.append(jnp.concatenate([av * jnp.exp(cum - lw), r * jnp.exp(cum)], axis=0))
            b_ts.append(bv * g_inv)
            k_ts.append(k2 * g_inv)
            tots.append(tot_all[:, c0:c0 + LANES])
            gates.append(g_all[:, c0:c0 + LANES])

    a_abs, a_rbs, a_aks, a_rks = [], [], [], []
    for p in pairs:
        for hh in range(2):
            lhs = jnp.where(lo2 if hh == 0 else jnp.logical_not(lo2), ar_stacks[p], 0.0)
            mb = _bdot_nt(lhs, b_ts[p])
            mk = _bdot_nt(lhs, k_ts[p])
            a_abs.append(jnp.where(strict, mb[0:rows], 0.0))
            a_rbs.append(jnp.where(incl, mb[rows:2 * rows], 0.0))
            a_aks.append(jnp.where(strict, mk[0:rows], 0.0))
            a_rks.append(jnp.where(incl, mk[rows:2 * rows], 0.0))
    tmats = _inv_unit_lower([-m for m in a_abs], eye, seq_len)

    def pair_stack(mats, p):
        return jnp.concatenate([mats[2 * p], mats[2 * p + 1]], axis=0)

    states, xss = [], []
    for i, (g, p) in enumerate(chains):
        sps, xs = [], None
        for s in range(nseq):
            if has_state:
                sq = g * nseq + s
                sv = jnp.concatenate([s0_ref[sq, 2 * p], s0_ref[sq, 2 * p + 1]], axis=0)
                sp = jnp.where(blockdiag, jnp.concatenate([sv, sv], axis=1), 0.0)
            else:
                sp = sp_ref[i]
            both = _bdot_nt(_pick(sels2[s], ar_stacks[i]), sp)
            xs = both if xs is None else xs + both
            sps.append(sp)
        states.append(sps)
        xss.append(xs)
    yvs = [xss[i][0:rows] + by_head(_bdot(pair_stack(a_aks, i), vs[i])) for i in pairs]
    us = [by_head(_bdot(pair_stack(tmats, i), yvs[i])) for i in pairs]
    os_ = [xss[i][rows:2 * rows]
           + by_head(_bdot(pair_stack(a_rbs, i), us[i]) + _bdot(pair_stack(a_rks, i), vs[i]))
           for i in pairs]
    for i, (g, p) in enumerate(chains):
        uv = jnp.concatenate([us[i], vs[i]], axis=0)
        bk = jnp.concatenate([b_ts[i], k_ts[i]], axis=0)
        for s in range(nseq):
            upd = jnp.where(blockdiag, _bdot_tn(_pick(sels2[s], uv), bk), 0.0)
            sp_new = (states[i][s] + upd) * jnp.exp(tots[i][s * seq_len:s * seq_len + 1, :])
            if has_state:
                sq = g * nseq + s
                so_ref[sq, 2 * p] = sp_new[0:N_C, 0:N_C]
                so_ref[sq, 2 * p + 1] = pltpu.roll(sp_new[N_C:2 * N_C, :], N_C, 1)[:, 0:N_C]
            else:
                sp_ref[i] = sp_new

    for i, (g, p) in enumerate(chains):
        c0 = p * LANES
        o = os_[i]
        mean = seg_sum(o) * (1.0 / N_C)
        d = o - mean
        var = seg_sum(d * d) * (1.0 / N_C)
        on = d * lax.rsqrt(var + GN_EPS) * vec_ref[5:6, c0:c0 + LANES] + vec_ref[6:7, c0:c0 + LANES]
        bonus = seg_sum(rs[i] * k2s[i] * vec_ref[4:5, c0:c0 + LANES]) * vs[i]
        _store_y(y_ref, g, c0, (on + bonus) * gates[i], null_rows)

    if not has_state:
        @pl.when(c == pl.num_programs(1) - 1)
        def _():
            for i, (g, p) in enumerate(chains):
                sp = sp_ref[i]
                so_ref[g, 2 * p] = sp[0:N_C, 0:N_C]
                so_ref[g, 2 * p + 1] = pltpu.roll(sp[N_C:2 * N_C, :], N_C, 1)[:, 0:N_C]


def _rwkv(p3, shift_state, state, stack, layer, depth, mu, wwa, gup, vec, *,
          nbatch, nchunk, nseq, null_rows, ngroup):
    has_state = state is not None
    tail = (H_C, N_C, N_C)
    in_specs = [_row_spec(N_CC, 0, g, ngroup, nchunk) for g in range(ngroup)]
    args = [p3] * ngroup
    if has_state:
        in_specs.append(_layer_state_spec(layer, ngroup * nseq, (1, N_CC)))
        args.append(shift_state)
    in_specs += [
        pl.BlockSpec((1, N_CC), lambda b, c: (0, 0)),
        pl.BlockSpec((LANES, 2 * BRANCH_W), lambda b, c: (0, 0)),
        pl.BlockSpec((GATE_LORA, BRANCH_W), lambda b, c: (0, 0)),
        pl.BlockSpec((SUBLANES, BRANCH_W), lambda b, c: (0, 0)),
    ]
    args += [mu, wwa, gup, vec]
    if has_state:
        in_specs.append(_layer_state_spec(layer, ngroup * nseq, tail))
        args.append(state)
    stack_specs, stack_args, stack_shape, aliases = _stack_io(stack, depth, nbatch * nseq, tail, len(args))
    y_spec, y_shape = _y_io(nbatch, nchunk, ngroup, null_rows)
    return pl.pallas_call(
        functools.partial(_rwkv_body, ngroup=ngroup, nseq=nseq, null_rows=null_rows, has_state=has_state,
                          has_stack=stack is not None),
        grid=(nbatch // ngroup, nchunk),
        in_specs=in_specs + stack_specs,
        out_specs=[y_spec, _layer_state_spec(layer, ngroup * nseq, tail)],
        out_shape=[y_shape, stack_shape],
        input_output_aliases=aliases,
        scratch_shapes=[pltpu.VMEM((ngroup, CHUNK + SUBLANES, N_CC), F32),
                        pltpu.VMEM((ngroup * N_PAIR, 2 * N_C, 2 * N_C), F32)],
        compiler_params=_cparams(2),
        name="rwkv",
    )(*args, *stack_args)


def _pad_cols(w, n):
    return jnp.pad(w, ((0, 0),) * (w.ndim - 1) + ((0, n - w.shape[-1]),))


def _prep_in(w_in):
    w = w_in.astype(BF16)
    wa = w[..., :N_A]
    wb = w[..., N_A:N_A + N_B]
    wc = w[..., N_A + N_B:N_A + N_B + N_CC]
    wg = w[..., N_A + N_B + N_CC:]
    w1 = wa[..., :QKV_A + H_A * DV_A]
    w2 = wb[..., :N_B - GLA_RANK]
    misc = jnp.concatenate([_pad_cols(wa[..., QKV_A + H_A * DV_A:], LANES),
                            _pad_cols(wb[..., N_B - GLA_RANK:], LANES)], axis=-1)
    w3 = jnp.concatenate([wc, misc], axis=-1)
    return w1, w2, w3, wg


def _prep_ffn(w_up, w_down):
    return w_up[..., :D_FF].astype(BF16), w_up[..., D_FF:].astype(BF16), w_down.astype(BF16)


def _tail_rows(p, nb, t, n_tail, width):
    return jnp.stack([p[(b + 1) * t - n_tail:(b + 1) * t, :width] for b in range(nb)])


def kernel(x_prompt, x_sample, state_gdn, state_gdn_conv, state_gla, state_rwkv, state_rwkv_shift, w_in, conv_a, a_log, dt_bias, gdn_norm, gla_gk_up, gla_gk_bias, gla_norm, rwkv_mu, rwkv_w0, rwkv_w_up, rwkv_a0, rwkv_a_up, rwkv_g_up, rwkv_k_k, rwkv_k_a, rwkv_r_k, rwkv_ln_w, rwkv_ln_b, w_branch, w_out, norm_ff1, w_ff1_up, w_ff1_down, norm_mix, norm_ff2, w_ff2_up, w_ff2_down, norm_final):
    bp, tp, d = x_prompt.shape
    bs, ts, _ = x_sample.shape
    depth = w_in.shape[0]
    n_p, n_s = bp * tp, bs * ts
    null_rows = SAMPLE_ROWS - ts
    assert tp % CHUNK == 0 and null_rows == SAMPLE_ROWS // 2 and CONV_W - 1 <= null_rows
    seq_per_chunk = CHUNK // SAMPLE_ROWS
    assert bs % seq_per_chunk == 0
    nchunk_p = tp // CHUNK
    nstep_s = bs // seq_per_chunk
    prompt = dict(nbatch=bp, nchunk=nchunk_p, nseq=1, null_rows=0, ngroup=math.gcd(bp, PROMPT_GROUPS))
    sample = dict(nbatch=nstep_s, nchunk=1, nseq=seq_per_chunk, null_rows=null_rows,
                  ngroup=math.gcd(nstep_s, SAMPLE_GROUPS))

    x = jnp.concatenate([x_prompt.reshape(n_p, d), x_sample.reshape(n_s, d)], axis=0)
    gdn_p = gdn_s = gla_p = gla_s = rwkv_p = rwkv_s = None
    conv_p, conv_s, shift_p, shift_s = [], [], [], []
    ff1 = _prep_ffn(w_ff1_up, w_ff1_down)
    ff2 = _prep_ffn(w_ff2_up, w_ff2_down)
    w1, w2, w3, wg = _prep_in(w_in)
    wbr = w_branch.astype(BF16)
    wout = w_out.astype(BF16)
    for l in range(depth):
        h = _ffn(x, norm_ff1[l], *ff1, l)

        hn = _norm_cast(h, norm_mix[l], BF16)
        p1, p1s = _proj(hn, w1, l, w1.shape[2] // 2, n_sample=n_s)
        p2, p2s = _proj(hn, w2, l, w2.shape[2] // 2, n_sample=n_s)
        p3, p3s = _proj(hn, w3, l, w3.shape[2] // 2, n_sample=n_s)
        pg = _proj(hn, wg, l, wg.shape[2] // 4, BF16)

        prm = jnp.zeros((SUBLANES, LANES), F32)
        prm = prm.at[0, H_A:2 * H_A].set(a_log[l]).at[1, H_A:2 * H_A].set(dt_bias[l])
        gnorm = gdn_norm[l].reshape(1, DV_A)
        up_w = jnp.pad(gla_gk_up[l], ((0, LANES - GLA_RANK), (0, 0))).astype(BF16)
        gk_bias = gla_gk_bias[l].reshape(1, H_B * DK_B)
        lnorm = gla_norm[l].reshape(1, DV_B)
        mu = rwkv_mu[l].reshape(1, N_CC)
        wwa = jnp.zeros((LANES, 2 * BRANCH_W), F32)
        wwa = wwa.at[0:DECAY_LORA, 0:BRANCH_W].set(rwkv_w_up[l])
        wwa = wwa.at[DECAY_LORA:DECAY_LORA + AAA_LORA, BRANCH_W:].set(rwkv_a_up[l]).astype(BF16)
        gup = rwkv_g_up[l].astype(BF16)
        vec = jnp.stack([rwkv_w0[l], rwkv_a0[l], rwkv_k_k[l], rwkv_k_a[l], rwkv_r_k[l],
                         rwkv_ln_w[l], rwkv_ln_b[l], jnp.zeros((BRANCH_W,), F32)], axis=0)

        ya_p, gdn_p = _gdn(p1, p3, None, None, gdn_p, l, depth, conv_a[l], prm, gnorm, **prompt)
        ya_s, gdn_s = _gdn(p1s, p3s, state_gdn_conv, state_gdn, gdn_s, l, depth, conv_a[l], prm, gnorm, **sample)
        yb_p, gla_p = _gla(p2, p3, None, gla_p, l, depth, up_w, gk_bias, lnorm, **prompt)
        yb_s, gla_s = _gla(p2s, p3s, state_gla, gla_s, l, depth, up_w, gk_bias, lnorm, **sample)
        yc_p, rwkv_p = _rwkv(p3, None, None, rwkv_p, l, depth, mu, wwa, gup, vec, **prompt)
        yc_s, rwkv_s = _rwkv(p3s, state_rwkv_shift, state_rwkv, rwkv_s, l, depth, mu, wwa, gup, vec, **sample)

        h = _merge([y.reshape(n_p, BRANCH_W) for y in (ya_p, yb_p, yc_p)],
                   [y.reshape(n_s, BRANCH_W) for y in (ya_s, yb_s, yc_s)], pg, wbr, wout, h, l)
        x = _ffn(h, norm_ff2[l], *ff2, l)

        conv_p.append(_tail_rows(p1, bp, tp, CONV_W - 1, QKV_A))
        conv_s.append(p1[n_p:].reshape(bs, ts, -1)[:, ts - (CONV_W - 1):, :QKV_A])
        shift_p.append(_tail_rows(p3, bp, tp, 1, N_CC))
        shift_s.append(p3[n_p:].reshape(bs, ts, -1)[:, ts - 1:, :N_CC])

    y_p = _norm_cast(x, norm_final, F32, 0, n_p)
    y_s = _norm_cast(x, norm_final, F32, n_p, n_s)
    return (y_p.reshape(bp, tp, d), y_s.reshape(bs, ts, d),
            gdn_p, gdn_s, jnp.stack(conv_p), jnp.stack(conv_s), gla_p, gla_s,
            rwkv_p, rwkv_s, jnp.stack(shift_p), jnp.stack(shift_s))
```

```python
import functools
import math

import jax
import jax.numpy as jnp
from jax import lax
from jax.experimental import pallas as pl
from jax.experimental.pallas import tpu as pltpu

F32 = jnp.float32
BF16 = jnp.bfloat16

D_MODEL = 2048
N_BRANCH = 3
BRANCH_W = 768
DK_A, DV_A, H_A, CONV_W = 128, 128, 6, 4
DK_B, DV_B, H_B, GLA_RANK, GLA_NORMALIZER = 64, 128, 6, 16, 16.0
N_C, H_C, DECAY_LORA, AAA_LORA, GATE_LORA = 64, 12, 64, 64, 128
GN_EPS = 64e-5
D_FF = 5504
NORM_EPS = 1e-6
QKV_A = 2 * H_A * DK_A + H_A * DV_A
N_A = QKV_A + H_A * DV_A + 2 * H_A
N_B = 2 * H_B * DK_B + 2 * H_B * DV_B + GLA_RANK
N_CC = 3 * BRANCH_W + DECAY_LORA + AAA_LORA + GATE_LORA
N_PAIR = H_C // 2

LANES = 128
SUBLANES = 8
CHUNK = 64
PROMPT_GROUPS = 4
SAMPLE_GROUPS = 2
SAMPLE_ROWS = 8
FF_TILE = 512
VMEM_LIMIT = 60 * 1024 * 1024


def _cparams(n_axes):
    return pltpu.CompilerParams(dimension_semantics=("arbitrary",) * n_axes,
                                vmem_limit_bytes=VMEM_LIMIT)


def _pick_tile(n, prefs):
    for t in prefs:
        if n % t == 0:
            return t
    raise ValueError(f"no tile for {n}")


def _bdot(a, b):
    return jnp.dot(a.astype(BF16), b.astype(BF16), preferred_element_type=F32)


def _bdot_nt(a, b):
    return lax.dot_general(a.astype(BF16), b.astype(BF16), (((1,), (1,)), ((), ())),
                           preferred_element_type=F32)


def _bdot_tn(a, b):
    return lax.dot_general(a.astype(BF16), b.astype(BF16), (((0,), (0,)), ((), ())),
                           preferred_element_type=F32)


def _sigmoid(x):
    return jax.nn.sigmoid(x)


def _silu(x):
    return x * _sigmoid(x)


def _softplus(x):
    return jnp.maximum(x, 0.0) + jnp.log(1.0 + jnp.exp(-jnp.abs(x)))


def _rmsnorm_rows(x, g):
    return x * lax.rsqrt(jnp.mean(x * x, axis=-1, keepdims=True) + NORM_EPS) * g


def _chunk_masks(rows, seq_len):
    sh = int(math.log2(seq_len))
    ri = lax.broadcasted_iota(jnp.int32, (rows, rows), 0)
    ci = lax.broadcasted_iota(jnp.int32, (rows, rows), 1)
    same = (ri >> sh) == (ci >> sh)
    return same & (ri >= ci), same & (ri > ci), same


def _eye(rows):
    return (lax.broadcasted_iota(jnp.int32, (rows, rows), 0)
            == lax.broadcasted_iota(jnp.int32, (rows, rows), 1)).astype(F32)


def _inv_unit_lower(lms, eye, nil):
    xs = [eye - lm for lm in lms]
    ps = list(lms)
    k = 2
    while k < nil:
        ps = [_bdot(p, p) for p in ps]
        xs = [x + _bdot(x, p) for x, p in zip(xs, ps)]
        k *= 2
    return xs


def _row_valid(rows, seq_len, null_rows, width):
    r = lax.broadcasted_iota(jnp.int32, (rows, width), 0)
    return (r & (seq_len - 1)) >= null_rows


def _seq_rows(rows, seq_len, s, width):
    r = lax.broadcasted_iota(jnp.int32, (rows, width), 0) & (CHUNK - 1)
    return (r >> int(math.log2(seq_len))) == s


def _pick(sel, xv):
    return xv if sel is None else jnp.where(sel, xv, 0.0)


def _ffn_body(x_ref, g_ref, wg_ref, wu_ref, wd_ref, o_ref, xn_ref, *, d_ff):
    j = pl.program_id(1)

    @pl.when(j == 0)
    def _():
        xn_ref[...] = _rmsnorm_rows(x_ref[...], g_ref[...]).astype(BF16)
        o_ref[...] = jnp.zeros_like(o_ref)

    dup = j * FF_TILE - _ff_start(j, d_ff)
    col_ok = lax.broadcasted_iota(jnp.int32, (1, FF_TILE), 1) >= dup
    row_ok = lax.broadcasted_iota(jnp.int32, (FF_TILE, 1), 0) >= dup
    xn = xn_ref[...]
    gate = jnp.dot(xn, wg_ref[0], preferred_element_type=F32)
    up = jnp.dot(xn, wu_ref[0], preferred_element_type=F32)
    act = jnp.where(col_ok, _silu(gate) * up, 0.0).astype(BF16)
    wd = jnp.where(row_ok, wd_ref[0], jnp.zeros((), BF16))
    o_ref[...] += jnp.dot(act, wd, preferred_element_type=F32)

    @pl.when(j == pl.num_programs(1) - 1)
    def _():
        o_ref[...] = x_ref[...] + 0.5 * o_ref[...]


def _ff_start(j, d_ff):
    return jnp.minimum(j * FF_TILE, d_ff - FF_TILE)


def _ffn(x, g, w_up, w_down, layer):
    n, d = x.shape
    tm = _pick_tile(n, (1088, 544, 512, 256, 128, 64, 32, 16, 8))
    d_ff = w_down.shape[1]
    assert d_ff % LANES == 0 and d_ff >= FF_TILE
    nf = pl.cdiv(d_ff, FF_TILE)

    def up_spec(col0):
        return pl.BlockSpec((pl.Element(1), pl.Element(d), pl.Element(FF_TILE)),
                            lambda i, j: (layer, 0, pl.multiple_of(col0 + _ff_start(j, d_ff), LANES)))

    return pl.pallas_call(
        functools.partial(_ffn_body, d_ff=d_ff),
        grid=(n // tm, nf),
        in_specs=[
            pl.BlockSpec((tm, d), lambda i, j: (i, 0)),
            pl.BlockSpec((1, d), lambda i, j: (0, 0)),
            up_spec(0),
            up_spec(d_ff),
            pl.BlockSpec((pl.Element(1), pl.Element(FF_TILE), pl.Element(d)),
                         lambda i, j: (layer, pl.multiple_of(_ff_start(j, d_ff), LANES), 0)),
        ],
        out_specs=pl.BlockSpec((tm, d), lambda i, j: (i, 0)),
        out_shape=jax.ShapeDtypeStruct((n, d), F32),
        scratch_shapes=[pltpu.VMEM((tm, d), BF16)],
        compiler_params=_cparams(2),
        name="ffn",
    )(x, g.reshape(1, d), w_up, w_up, w_down)


def _norm_cast_body(x_ref, g_ref, o_ref):
    o_ref[...] = _rmsnorm_rows(x_ref[...], g_ref[...]).astype(o_ref.dtype)


def _norm_cast(x, g, dtype, start=0, count=None):
    n, d = x.shape
    count = n - start if count is None else count
    tm = _pick_tile(math.gcd(start, count), (544, 512, 256, 128, 64, 32, 16, 8))
    first = start // tm
    return pl.pallas_call(
        _norm_cast_body,
        grid=(count // tm,),
        in_specs=[pl.BlockSpec((tm, d), lambda i: (first + i, 0)), pl.BlockSpec((1, d), lambda i: (0, 0))],
        out_specs=pl.BlockSpec((tm, d), lambda i: (i, 0)),
        out_shape=jax.ShapeDtypeStruct((count, d), dtype),
        compiler_params=_cparams(1),
        name="norm",
    )(x, g.reshape(1, d))


def _pad_rows(x):
    n, w = x.shape
    x3 = x.reshape(n // SUBLANES, SUBLANES, w)
    tail = lax.broadcasted_iota(jnp.int32, x3.shape, 1) >= SUBLANES // 2
    first = jnp.where(tail, pltpu.roll(x3, SUBLANES // 2, 1), 0.0)
    second = jnp.where(tail, x3, 0.0)
    return jnp.stack([first, second], axis=1).reshape(2 * n, w)


def _compact_rows(x):
    n, w = x.shape
    x4 = x.reshape(n // (2 * SUBLANES), 2, SUBLANES, w)
    head = lax.broadcasted_iota(jnp.int32, (n // (2 * SUBLANES), SUBLANES, w), 1) < SUBLANES // 2
    return jnp.where(head, pltpu.roll(x4[:, 0], SUBLANES // 2, 1), x4[:, 1]).reshape(n // 2, w)


def _proj_body(x_ref, w_ref, o_ref, *pad_ref, sample_off, n_sample):
    res = jnp.dot(x_ref[...], w_ref[...], preferred_element_type=F32)
    o_ref[...] = res.astype(o_ref.dtype)
    if pad_ref:
        @pl.when(pl.program_id(1) == pl.num_programs(1) - 1)
        def _():
            pad_ref[0][...] = _pad_rows(res[sample_off:sample_off + n_sample])


def _proj(xn, w, layer, tn, out_dtype=F32, n_sample=0):
    n, d = xn.shape
    nn = w.shape[2]
    tm = _pick_tile(n, (1088, 544, 512, 256, 128, 64, 32, 16))
    out_specs = [pl.BlockSpec((tm, tn), lambda j, i: (i, j))]
    out_shape = [jax.ShapeDtypeStruct((n, nn), out_dtype)]
    sample_off = tm - n_sample
    if n_sample:
        assert 0 <= sample_off and sample_off % SUBLANES == 0 and n_sample % SUBLANES == 0
        out_specs.append(pl.BlockSpec((2 * n_sample, tn), lambda j, i: (0, j)))
        out_shape.append(jax.ShapeDtypeStruct((2 * n_sample, nn), F32))
    out = pl.pallas_call(
        functools.partial(_proj_body, sample_off=sample_off, n_sample=n_sample),
        grid=(nn // tn, n // tm),
        in_specs=[
            pl.BlockSpec((tm, d), lambda j, i: (i, 0)),
            pl.BlockSpec((None, d, tn), lambda j, i: (layer, 0, j)),
        ],
        out_specs=out_specs,
        out_shape=out_shape,
        compiler_params=_cparams(2),
        name="proj",
    )(xn, w)
    return out if n_sample else out[0]


MERGE_TN = 1024


def _merge_body(yap_ref, ybp_ref, ycp_ref, yas_ref, ybs_ref, ycs_ref, ga_ref, gb_ref, gc_ref,
                wb_ref, wo_ref, h_ref, o_ref, *, prompt_tiles):
    i = pl.program_id(0)
    j = pl.program_id(1)

    @pl.when(j == 0)
    def _():
        o_ref[...] = h_ref[...]

    is_prompt = i < prompt_tiles
    ya = jnp.where(is_prompt, yap_ref[...], yas_ref[...])
    yb = jnp.where(is_prompt, ybp_ref[...], ybs_ref[...])
    yc = jnp.where(is_prompt, ycp_ref[...], ycs_ref[...])
    m = _sigmoid(ga_ref[...].astype(F32)) * jnp.dot(ya, wb_ref[0], preferred_element_type=F32)
    m += _sigmoid(gb_ref[...].astype(F32)) * jnp.dot(yb, wb_ref[1], preferred_element_type=F32)
    m += _sigmoid(gc_ref[...].astype(F32)) * jnp.dot(yc, wb_ref[2], preferred_element_type=F32)
    o_ref[...] += jnp.dot(m.astype(BF16), wo_ref[...], preferred_element_type=F32)


def _merge(ys_prompt, ys_sample, pg, wb, wo, h, layer):
    n, d = h.shape
    n_p, n_s = ys_prompt[0].shape[0], ys_sample[0].shape[0]
    tm = _pick_tile(math.gcd(n_p, n_s), (512, 256, 128, 64, 32, 16))
    prompt_tiles = n_p // tm
    nj = d // MERGE_TN
    yp_spec = pl.BlockSpec((tm, BRANCH_W), lambda i, j: (jnp.minimum(i, prompt_tiles - 1), 0))
    ys_spec = pl.BlockSpec((tm, BRANCH_W), lambda i, j: (jnp.maximum(i - prompt_tiles, 0), 0))
    return pl.pallas_call(
        functools.partial(_merge_body, prompt_tiles=prompt_tiles),
        grid=(n // tm, nj),
        in_specs=[
            yp_spec, yp_spec, yp_spec, ys_spec, ys_spec, ys_spec,
            pl.BlockSpec((tm, MERGE_TN), lambda i, j: (i, j)),
            pl.BlockSpec((tm, MERGE_TN), lambda i, j: (i, nj + j)),
            pl.BlockSpec((tm, MERGE_TN), lambda i, j: (i, 2 * nj + j)),
            pl.BlockSpec((None, N_BRANCH, BRANCH_W, MERGE_TN), lambda i, j: (layer, 0, 0, j)),
            pl.BlockSpec((None, MERGE_TN, d), lambda i, j: (layer, j, 0)),
            pl.BlockSpec((tm, d), lambda i, j: (i, 0)),
        ],
        out_specs=pl.BlockSpec((tm, d), lambda i, j: (i, 0)),
        out_shape=jax.ShapeDtypeStruct((n, d), F32),
        compiler_params=_cparams(2),
        name="merge",
    )(*ys_prompt, *ys_sample, pg, pg, pg, wb, wo, h)


def _layer_state_spec(layer, nseq, tail):
    zeros = (0,) * len(tail)
    return pl.BlockSpec((None, nseq) + tail, lambda b, c: (layer, b) + zeros)


def _stack_io(stack, depth, nstate, tail, n_args):
    shape = jax.ShapeDtypeStruct((depth, nstate) + tail, F32)
    if stack is None:
        return [], [], shape, {}
    return [pl.BlockSpec(memory_space=pl.ANY)], [stack], shape, {n_args: 1}


def _split3_dot(lmat, x):
    lb = lmat.astype(BF16)
    hi = x.astype(BF16)
    r1 = x - hi.astype(F32)
    mid = r1.astype(BF16)
    lo = (r1 - mid.astype(F32)).astype(BF16)
    return (jnp.dot(lb, hi, preferred_element_type=F32) + jnp.dot(lb, mid, preferred_element_type=F32)
            + jnp.dot(lb, lo, preferred_element_type=F32))


def _gdn_body(*refs, ngroup, nseq, null_rows, has_state, has_stack):
    refs = list(refs)
    xbuf = refs.pop()
    so_ref = refs.pop()
    y_ref = refs.pop()
    if has_stack:
        refs.pop()
    row_refs = [refs[3 * g:3 * g + 3] for g in range(ngroup)]
    refs = refs[3 * ngroup:]
    if has_state:
        cs_ref, cw_ref, prm_ref, nw_ref, s0_ref = refs
    else:
        cw_ref, prm_ref, nw_ref = refs
        s0_ref = so_ref
    rows = CHUNK
    seq_len = rows // nseq
    c = pl.program_id(1)
    chains = [(g, h) for g in range(ngroup) for h in range(H_A)]

    @pl.when(c == 0)
    def _():
        for g in range(ngroup):
            xbuf[g, 0:SUBLANES, :] = jnp.zeros((SUBLANES, QKV_A), F32)
        if not has_state:
            so_ref[...] = jnp.zeros_like(so_ref)

    @pl.when(c > 0)
    def _():
        for g in range(ngroup):
            xbuf[g, 0:SUBLANES, :] = xbuf[g, rows:rows + SUBLANES, :]

    for g in range(ngroup):
        xbuf[g, SUBLANES:SUBLANES + rows, :] = row_refs[g][0][...]
        if has_state:
            for s in range(nseq):
                r0 = SUBLANES + s * seq_len + null_rows - (CONV_W - 1)
                xbuf[g, r0:r0 + CONV_W - 1, :] = cs_ref[g * nseq + s]

    incl, strict, same = _chunk_masks(rows, seq_len)
    eye = _eye(rows)
    valid = _row_valid(rows, seq_len, null_rows, LANES) if null_rows else None
    sels = [_seq_rows(rows, seq_len, s, LANES) for s in range(nseq)] if nseq > 1 else [None]
    sels2 = [_seq_rows(2 * rows, seq_len, s, LANES) for s in range(nseq)] if nseq > 1 else [None]
    lstack = jnp.concatenate([incl.astype(F32), same.astype(F32)], axis=0)

    def conv_cols(g, c0):
        acc = xbuf[g, SUBLANES - 3:SUBLANES - 3 + rows, c0:c0 + LANES] * cw_ref[0:1, c0:c0 + LANES]
        for j in range(1, CONV_W):
            acc += (xbuf[g, SUBLANES - 3 + j:SUBLANES - 3 + j + rows, c0:c0 + LANES]
                    * cw_ref[j:j + 1, c0:c0 + LANES])
        return _silu(acc)

    qs, ks, kbs, rhs, decays, wq_parts, kds, gtcs = [], [], [], [], [], [], [], []
    for g in range(ngroup):
        ba = row_refs[g][2][...]
        beta_all = _sigmoid(ba)
        g_all = -jnp.exp(prm_ref[0:1, :]) * _softplus(ba + prm_ref[1:2, :])
        if null_rows:
            g_all = jnp.where(valid, g_all, 0.0)
        cs = _split3_dot(lstack, g_all)
        gc_all, gt_all = cs[0:rows], cs[rows:2 * rows]
        gc_t = gc_all.T
        for h in range(H_A):
            q = conv_cols(g, h * DK_A)
            k = conv_cols(g, H_A * DK_A + h * DK_A)
            v = conv_cols(g, 2 * H_A * DK_A + h * DV_A)
            q = q * lax.rsqrt(jnp.sum(q * q, axis=-1, keepdims=True) + 1e-6) * DK_A ** -0.5
            k = k * lax.rsqrt(jnp.sum(k * k, axis=-1, keepdims=True) + 1e-6)
            if null_rows:
                q = jnp.where(valid, q, 0.0)
                k = jnp.where(valid, k, 0.0)
                v = jnp.where(valid, v, 0.0)
            beta = beta_all[:, h:h + 1]
            gcc = gc_all[:, H_A + h:H_A + h + 1]
            gtc = gt_all[:, H_A + h:H_A + h + 1]
            gcr = gc_t[H_A + h:H_A + h + 1, :]
            egc = jnp.exp(gcc)
            kb = k * beta
            qs.append(q)
            ks.append(k)
            kbs.append(kb)
            rhs.append(jnp.concatenate([v * beta, kb * egc], axis=1))
            decays.append(jnp.where(incl, jnp.exp(jnp.where(incl, gcc - gcr, 0.0)), 0.0))
            wq_parts.append(q * egc)
            kds.append(k * jnp.exp(gtc - gcc))
            gtcs.append(gtc)

    n = range(len(chains))
    lms = [jnp.where(strict, _bdot_nt(kbs[i], ks[i]) * decays[i], 0.0) for i in n]
    attns = [_bdot_nt(qs[i], ks[i]) * decays[i] for i in n]
    tmats = _inv_unit_lower(lms, eye, seq_len)
    uws = [_bdot(tmats[i], rhs[i]) for i in n]
    us, os_ = [], []
    for i, (g, h) in enumerate(chains):
        u = uws[i][:, 0:DV_A]
        wq = jnp.concatenate([uws[i][:, DV_A:DV_A + DK_A], wq_parts[i]], axis=0)
        o = None
        for s in range(nseq):
            both = _bdot(_pick(sels2[s], wq), s0_ref[g * nseq + s, h])
            u = u - both[0:rows]
            o = both[rows:2 * rows] if o is None else o + both[rows:2 * rows]
        us.append(u)
        os_.append(o)
    os_ = [os_[i] + _bdot(attns[i], us[i]) for i in n]
    for i, (g, h) in enumerate(chains):
        for s in range(nseq):
            g_last = jnp.exp(gtcs[i][s * seq_len:s * seq_len + 1, :])
            so_ref[g * nseq + s, h] = (g_last * s0_ref[g * nseq + s, h]
                                       + _bdot_tn(_pick(sels[s], kds[i]), us[i]))
    for i, (g, h) in enumerate(chains):
        z = row_refs[g][1][:, h * DV_A:(h + 1) * DV_A]
        _store_y(y_ref, g, h * DV_A, _rmsnorm_rows(os_[i], nw_ref[...]) * _silu(z), null_rows)


def _row_spec(width, col, g, ngroup, nchunk):
    return pl.BlockSpec((CHUNK, width), lambda b, c: ((b * ngroup + g) * nchunk + c, col))


def _y_io(nbatch, nchunk, ngroup, null_rows):
    out_rows = CHUNK // 2 if null_rows else CHUNK
    return (pl.BlockSpec((ngroup, out_rows, BRANCH_W), lambda b, c: (b, c, 0)),
            jax.ShapeDtypeStruct((nbatch, nchunk * out_rows, BRANCH_W), BF16))


def _store_y(y_ref, g, c0, yv, null_rows):
    if null_rows:
        yv = _compact_rows(yv)
    y_ref[g, :, c0:c0 + yv.shape[1]] = yv.astype(y_ref.dtype)


def _gdn(p1, p3, conv_state, state, stack, layer, depth, conv_w, prm, norm_w, *,
         nbatch, nchunk, nseq, null_rows, ngroup):
    has_state = state is not None
    tail = (H_A, DK_A, DV_A)
    in_specs, args = [], []
    for g in range(ngroup):
        in_specs += [_row_spec(QKV_A, 0, g, ngroup, nchunk),
                     _row_spec(BRANCH_W, QKV_A // BRANCH_W, g, ngroup, nchunk),
                     _row_spec(LANES, N_CC // LANES, g, ngroup, nchunk)]
        args += [p1, p1, p3]
    if has_state:
        in_specs.append(_layer_state_spec(layer, ngroup * nseq, (CONV_W - 1, QKV_A)))
        args.append(conv_state)
    in_specs += [
        pl.BlockSpec((CONV_W, QKV_A), lambda b, c: (0, 0)),
        pl.BlockSpec((SUBLANES, LANES), lambda b, c: (0, 0)),
        pl.BlockSpec((1, DV_A), lambda b, c: (0, 0)),
    ]
    args += [conv_w, prm, norm_w]
    if has_state:
        in_specs.append(_layer_state_spec(layer, ngroup * nseq, tail))
        args.append(state)
    stack_specs, stack_args, stack_shape, aliases = _stack_io(stack, depth, nbatch * nseq, tail, len(args))
    y_spec, y_shape = _y_io(nbatch, nchunk, ngroup, null_rows)
    return pl.pallas_call(
        functools.partial(_gdn_body, ngroup=ngroup, nseq=nseq, null_rows=null_rows, has_state=has_state,
                          has_stack=stack is not None),
        grid=(nbatch // ngroup, nchunk),
        in_specs=in_specs + stack_specs,
        out_specs=[y_spec, _layer_state_spec(layer, ngroup * nseq, tail)],
        out_shape=[y_shape, stack_shape],
        input_output_aliases=aliases,
        scratch_shapes=[pltpu.VMEM((ngroup, CHUNK + SUBLANES, QKV_A), F32)],
        compiler_params=_cparams(2),
        name="gdn",
    )(*args, *stack_args)


def _gla_body(*refs, ngroup, nseq, null_rows, has_state, has_stack):
    refs = list(refs)
    so_ref = refs.pop()
    y_ref = refs.pop()
    if has_stack:
        refs.pop()
    row_refs = [refs[5 * g:5 * g + 5] for g in range(ngroup)]
    refs = refs[5 * ngroup:]
    if has_state:
        up_ref, bias_ref, nw_ref, s0_ref = refs
    else:
        up_ref, bias_ref, nw_ref = refs
        s0_ref = so_ref
    rows = CHUNK
    seq_len = rows // nseq
    c = pl.program_id(1)
    chains = [(g, h) for g in range(ngroup) for h in range(H_B)]
    if not has_state:
        @pl.when(c == 0)
        def _():
            so_ref[...] = jnp.zeros_like(so_ref)

    incl, _, same = _chunk_masks(rows, seq_len)
    ci = lax.broadcasted_iota(jnp.int32, (rows, rows), 1)
    first_half = same & ((ci & (seq_len - 1)) < seq_len // 2)
    wide = H_B * DK_B
    lstack = jnp.concatenate([incl.astype(F32), same.astype(F32), first_half.astype(F32)], axis=0)
    valid_w = _row_valid(rows, seq_len, null_rows, wide) if null_rows else None
    valid_k = _row_valid(rows, seq_len, null_rows, DK_B) if null_rows else None
    valid_v = _row_valid(rows, seq_len, null_rows, DV_B) if null_rows else None
    sels = [_seq_rows(rows, seq_len, s, DK_B) for s in range(nseq)] if nseq > 1 else [None]

    qis, kis, vs, q_ins, k_decs, a_lasts = [], [], [], [], [], []
    for g in range(ngroup):
        q_ref, k_ref, v_ref, _, gkl_ref = row_refs[g]
        x = _bdot(gkl_ref[...], up_ref[...]) + bias_ref[...]
        gk_all = (jnp.minimum(x, 0.0) - jnp.log(1.0 + jnp.exp(-jnp.abs(x)))) / GLA_NORMALIZER
        if null_rows:
            gk_all = jnp.where(valid_w, gk_all, 0.0)
        cs = _split3_dot(lstack, gk_all)
        bc_all, bt_all, an_all = cs[0:rows], cs[rows:2 * rows], cs[2 * rows:3 * rows]
        bt_t = bt_all.T
        for h in range(H_B):
            c0 = h * DK_B
            q = q_ref[:, c0:c0 + DK_B] * DK_B ** -0.5
            k = k_ref[:, c0:c0 + DK_B]
            v = v_ref[:, h * DV_B:(h + 1) * DV_B]
            if null_rows:
                q = jnp.where(valid_k, q, 0.0)
                k = jnp.where(valid_k, k, 0.0)
                v = jnp.where(valid_v, v, 0.0)
            bc = bc_all[:, c0:c0 + DK_B]
            bt = bt_all[:, c0:c0 + DK_B]
            an = an_all[:, c0:c0 + DK_B]
            qis.append(q * jnp.exp(bc - an))
            kis.append(k * jnp.exp(an - bc))
            vs.append(v)
            q_ins.append(q * jnp.exp(bc))
            k_decs.append(k * jnp.exp(bt - bc))
            a_lasts.append([jnp.exp(bt_t[c0:c0 + DK_B, s * seq_len:s * seq_len + 1])
                            for s in range(nseq)])

    n = range(len(chains))
    attns = [jnp.where(incl, _bdot_nt(qis[i], kis[i]), 0.0) for i in n]
    os_ = [_bdot(attns[i], vs[i]) for i in n]
    for i, (g, h) in enumerate(chains):
        for s in range(nseq):
            os_[i] = os_[i] + _bdot(_pick(sels[s], q_ins[i]), s0_ref[g * nseq + s, h])
    for i, (g, h) in enumerate(chains):
        for s in range(nseq):
            so_ref[g * nseq + s, h] = (a_lasts[i][s] * s0_ref[g * nseq + s, h]
                                       + _bdot_tn(_pick(sels[s], k_decs[i]), vs[i]))
    for i, (g, h) in enumerate(chains):
        gate = row_refs[g][3][:, h * DV_B:(h + 1) * DV_B]
        _store_y(y_ref, g, h * DV_B, _rmsnorm_rows(os_[i], nw_ref[...]) * _silu(gate), null_rows)


def _gla(p2, p3, state, stack, layer, depth, up_w, bias, norm_w, *, nbatch, nchunk, nseq, null_rows, ngroup):
    has_state = state is not None
    wide = H_B * DK_B
    tail = (H_B, DK_B, DV_B)
    in_specs, args = [], []
    for g in range(ngroup):
        in_specs += [_row_spec(wide, 0, g, ngroup, nchunk), _row_spec(wide, 1, g, ngroup, nchunk),
                     _row_spec(BRANCH_W, 1, g, ngroup, nchunk), _row_spec(BRANCH_W, 2, g, ngroup, nchunk)]
        in_specs.append(_row_spec(LANES, N_CC // LANES + 1, g, ngroup, nchunk))
        args += [p2, p2, p2, p2, p3]
    in_specs += [pl.BlockSpec((LANES, wide), lambda b, c: (0, 0)),
                 pl.BlockSpec((1, wide), lambda b, c: (0, 0)),
                 pl.BlockSpec((1, DV_B), lambda b, c: (0, 0))]
    args += [up_w, bias, norm_w]
    if has_state:
        in_specs.append(_layer_state_spec(layer, ngroup * nseq, tail))
        args.append(state)
    stack_specs, stack_args, stack_shape, aliases = _stack_io(stack, depth, nbatch * nseq, tail, len(args))
    y_spec, y_shape = _y_io(nbatch, nchunk, ngroup, null_rows)
    return pl.pallas_call(
        functools.partial(_gla_body, ngroup=ngroup, nseq=nseq, null_rows=null_rows, has_state=has_state,
                          has_stack=stack is not None),
        grid=(nbatch // ngroup, nchunk),
        in_specs=in_specs + stack_specs,
        out_specs=[y_spec, _layer_state_spec(layer, ngroup * nseq, tail)],
        out_shape=[y_shape, stack_shape],
        input_output_aliases=aliases,
        compiler_params=_cparams(2),
        name="gla",
    )(*args, *stack_args)


def _rwkv_body(*refs, ngroup, nseq, null_rows, has_state, has_stack):
    refs = list(refs)
    sp_ref = refs.pop()
    pbuf = refs.pop()
    so_ref = refs.pop()
    y_ref = refs.pop()
    if has_stack:
        refs.pop()
    pc_refs = refs[:ngroup]
    refs = refs[ngroup:]
    if has_state:
        sh_ref, mu_ref, wwa_ref, gup_ref, vec_ref, s0_ref = refs
    else:
        mu_ref, wwa_ref, gup_ref, vec_ref = refs
    rows = CHUNK
    seq_len = rows // nseq
    c = pl.program_id(1)
    chains = [(g, p) for g in range(ngroup) for p in range(N_PAIR)]
    pairs = range(len(chains))

    @pl.when(c == 0)
    def _():
        for g in range(ngroup):
            pbuf[g, 0:SUBLANES, :] = jnp.zeros((SUBLANES, N_CC), F32)
        if not has_state:
            sp_ref[...] = jnp.zeros_like(sp_ref)

    @pl.when(c > 0)
    def _():
        for g in range(ngroup):
            pbuf[g, 0:SUBLANES, :] = pbuf[g, rows:rows + SUBLANES, :]

    for g in range(ngroup):
        pbuf[g, SUBLANES:SUBLANES + rows, :] = pc_refs[g][...]
        if has_state:
            for s in range(nseq):
                r0 = SUBLANES + s * seq_len + null_rows - 1
                pbuf[g, r0:r0 + 1, :] = sh_ref[g * nseq + s]

    def xc_cols(g, c0, w):
        cur = pbuf[g, SUBLANES:SUBLANES + rows, c0:c0 + w]
        prev = pbuf[g, SUBLANES - 1:SUBLANES - 1 + rows, c0:c0 + w]
        return cur + (prev - cur) * mu_ref[:, c0:c0 + w]

    incl, strict, same = _chunk_masks(rows, seq_len)
    eye = _eye(rows)
    lane = lax.broadcasted_iota(jnp.int32, (rows, LANES), 1)
    lo = lane < N_C
    lo2 = lax.broadcasted_iota(jnp.int32, (2 * rows, LANES), 1) < N_C
    valid = _row_valid(rows, seq_len, null_rows, LANES) if null_rows else None
    valid_w = _row_valid(rows, seq_len, null_rows, BRANCH_W) if null_rows else None
    sels2 = [_seq_rows(2 * rows, seq_len, s, LANES) for s in range(nseq)] if nseq > 1 else [None]
    lstack = jnp.concatenate([incl.astype(F32), same.astype(F32)], axis=0)
    w0 = vec_ref[0:1, :]
    a0 = vec_ref[1:2, :]

    r2 = lax.broadcasted_iota(jnp.int32, (2 * N_C, LANES), 0)
    c2 = lax.broadcasted_iota(jnp.int32, (2 * N_C, LANES), 1)
    blockdiag = (r2 >= N_C) == (c2 >= N_C)

    def seg_sum(xv):
        s_lo = jnp.sum(jnp.where(lo, xv, 0.0), axis=-1, keepdims=True)
        s_hi = jnp.sum(jnp.where(lo, 0.0, xv), axis=-1, keepdims=True)
        return jnp.where(lo, s_lo, s_hi)

    def by_head(stacked):
        return jnp.where(lo, stacked[0:rows], stacked[rows:2 * rows])

    rs, k2s, vs, ar_stacks, b_ts, k_ts, tots, gates = [], [], [], [], [], [], [], []
    for g in range(ngroup):
        lora = xc_cols(g, 3 * BRANCH_W, DECAY_LORA + AAA_LORA + GATE_LORA)
        wa_in = lora[:, 0:LANES]
        wa_in = jnp.where(lo, jnp.tanh(wa_in), wa_in)
        wa = _bdot(wa_in, wwa_ref[...])
        g_all = _bdot(_sigmoid(lora[:, LANES:2 * LANES]), gup_ref[...])
        w_log = -_softplus(-(w0 + wa[:, 0:BRANCH_W])) - 0.5
        lw_all = -jnp.exp(w_log)
        a_all = _sigmoid(a0 + wa[:, BRANCH_W:2 * BRANCH_W])
        if null_rows:
            lw_all = jnp.where(valid_w, lw_all, 0.0)
        cs = _split3_dot(lstack, lw_all)
        cum_all, tot_all = cs[0:rows], cs[rows:2 * rows]
        for p in range(N_PAIR):
            c0 = p * LANES
            r = xc_cols(g, c0, LANES)
            k = xc_cols(g, BRANCH_W + c0, LANES)
            v = xc_cols(g, 2 * BRANCH_W + c0, LANES)
            a_p = a_all[:, c0:c0 + LANES]
            kkr = k * vec_ref[2:3, c0:c0 + LANES]
            kk = kkr * lax.rsqrt(seg_sum(kkr * kkr) + 1e-6)
            k2 = k * (1.0 + (a_p - 1.0) * vec_ref[3:4, c0:c0 + LANES])
            av = -kk
            bv = kk * a_p
            if null_rows:
                r = jnp.where(valid, r, 0.0)
                k2 = jnp.where(valid, k2, 0.0)
                v = jnp.where(valid, v, 0.0)
                av = jnp.where(valid, av, 0.0)
                bv = jnp.where(valid, bv, 0.0)
            cum = cum_all[:, c0:c0 + LANES]
            lw = lw_all[:, c0:c0 + LANES]
            g_inv = jnp.exp(-cum)
            rs.append(r)
            k2s.append(k2)
            vs.append(v)
            ar_stacks.append(jnp.concatenate([av * jnp.exp(cum - lw), r * jnp.exp(cum)], axis=0))
            b_ts.append(bv * g_inv)
            k_ts.append(k2 * g_inv)
            tots.append(tot_all[:, c0:c0 + LANES])
            gates.append(g_all[:, c0:c0 + LANES])

    a_abs, a_rbs, a_aks, a_rks = [], [], [], []
    for p in pairs:
        for hh in range(2):
            lhs = jnp.where(lo2 if hh == 0 else jnp.logical_not(lo2), ar_stacks[p], 0.0)
            mb = _bdot_nt(lhs, b_ts[p])
            mk = _bdot_nt(lhs, k_ts[p])
            a_abs.append(jnp.where(strict, mb[0:rows], 0.0))
            a_rbs.append(jnp.where(incl, mb[rows:2 * rows], 0.0))
            a_aks.append(jnp.where(strict, mk[0:rows], 0.0))
            a_rks.append(jnp.where(incl, mk[rows:2 * rows], 0.0))
    tmats = _inv_unit_lower([-m for m in a_abs], eye, seq_len)

    def pair_stack(mats, p):
        return jnp.concatenate([mats[2 * p], mats[2 * p + 1]], axis=0)

    states, xss = [], []
    for i, (g, p) in enumerate(chains):
        sps, xs = [], None
        for s in range(nseq):
            if has_state:
                sq = g * nseq + s
                sv = jnp.concatenate([s0_ref[sq, 2 * p], s0_ref[sq, 2 * p + 1]], axis=0)
                sp = jnp.where(blockdiag, jnp.concatenate([sv, sv], axis=1), 0.0)
            else:
                sp = sp_ref[i]
            both = _bdot_nt(_pick(sels2[s], ar_stacks[i]), sp)
            xs = both if xs is None else xs + both
            sps.append(sp)
        states.append(sps)
        xss.append(xs)
    yvs = [xss[i][0:rows] + by_head(_bdot(pair_stack(a_aks, i), vs[i])) for i in pairs]
    us = [by_head(_bdot(pair_stack(tmats, i), yvs[i])) for i in pairs]
    os_ = [xss[i][rows:2 * rows]
           + by_head(_bdot(pair_stack(a_rbs, i), us[i]) + _bdot(pair_stack(a_rks, i), vs[i]))
           for i in pairs]
    for i, (g, p) in enumerate(chains):
        uv = jnp.concatenate([us[i], vs[i]], axis=0)
        bk = jnp.concatenate([b_ts[i], k_ts[i]], axis=0)
        for s in range(nseq):
            upd = jnp.where(blockdiag, _bdot_tn(_pick(sels2[s], uv), bk), 0.0)
            sp_new = (states[i][s] + upd) * jnp.exp(tots[i][s * seq_len:s * seq_len + 1, :])
            if has_state:
                sq = g * nseq + s
                so_ref[sq, 2 * p] = sp_new[0:N_C, 0:N_C]
                so_ref[sq, 2 * p + 1] = pltpu.roll(sp_new[N_C:2 * N_C, :], N_C, 1)[:, 0:N_C]
            else:
                sp_ref[i] = sp_new

    for i, (g, p) in enumerate(chains):
        c0 = p * LANES
        o = os_[i]
        mean = seg_sum(o) * (1.0 / N_C)
        d = o - mean
        var = seg_sum(d * d) * (1.0 / N_C)
        on = d * lax.rsqrt(var + GN_EPS) * vec_ref[5:6, c0:c0 + LANES] + vec_ref[6:7, c0:c0 + LANES]
        bonus = seg_sum(rs[i] * k2s[i] * vec_ref[4:5, c0:c0 + LANES]) * vs[i]
        _store_y(y_ref, g, c0, (on + bonus) * gates[i], null_rows)

    if not has_state:
        @pl.when(c == pl.num_programs(1) - 1)
        def _():
            for i, (g, p) in enumerate(chains):
                sp = sp_ref[i]
                so_ref[g, 2 * p] = sp[0:N_C, 0:N_C]
                so_ref[g, 2 * p + 1] = pltpu.roll(sp[N_C:2 * N_C, :], N_C, 1)[:, 0:N_C]


def _rwkv(p3, shift_state, state, stack, layer, depth, mu, wwa, gup, vec, *,
          nbatch, nchunk, nseq, null_rows, ngroup):
    has_state = state is not None
    tail = (H_C, N_C, N_C)
    in_specs = [_row_spec(N_CC, 0, g, ngroup, nchunk) for g in range(ngroup)]
    args = [p3] * ngroup
    if has_state:
        in_specs.append(_layer_state_spec(layer, ngroup * nseq, (1, N_CC)))
        args.append(shift_state)
    in_specs += [
        pl.BlockSpec((1, N_CC), lambda b, c: (0, 0)),
        pl.BlockSpec((LANES, 2 * BRANCH_W), lambda b, c: (0, 0)),
        pl.BlockSpec((GATE_LORA, BRANCH_W), lambda b, c: (0, 0)),
        pl.BlockSpec((SUBLANES, BRANCH_W), lambda b, c: (0, 0)),
    ]
    args += [mu, wwa, gup, vec]
    if has_state:
        in_specs.append(_layer_state_spec(layer, ngroup * nseq, tail))
        args.append(state)
    stack_specs, stack_args, stack_shape, aliases = _stack_io(stack, depth, nbatch * nseq, tail, len(args))
    y_spec, y_shape = _y_io(nbatch, nchunk, ngroup, null_rows)
    return pl.pallas_call(
        functools.partial(_rwkv_body, ngroup=ngroup, nseq=nseq, null_rows=null_rows, has_state=has_state,
                          has_stack=stack is not None),
        grid=(nbatch // ngroup, nchunk),
        in_specs=in_specs + stack_specs,
        out_specs=[y_spec, _layer_state_spec(layer, ngroup * nseq, tail)],
        out_shape=[y_shape, stack_shape],
        input_output_aliases=aliases,
        scratch_shapes=[pltpu.VMEM((ngroup, CHUNK + SUBLANES, N_CC), F32),
                        pltpu.VMEM((ngroup * N_PAIR, 2 * N_C, 2 * N_C), F32)],
        compiler_params=_cparams(2),
        name="rwkv",
    )(*args, *stack_args)


def _pad_cols(w, n):
    return jnp.pad(w, ((0, 0),) * (w.ndim - 1) + ((0, n - w.shape[-1]),))


def _prep_in(w_in):
    w = w_in.astype(BF16)
    wa = w[..., :N_A]
    wb = w[..., N_A:N_A + N_B]
    wc = w[..., N_A + N_B:N_A + N_B + N_CC]
    wg = w[..., N_A + N_B + N_CC:]
    w1 = wa[..., :QKV_A + H_A * DV_A]
    w2 = wb[..., :N_B - GLA_RANK]
    misc = jnp.concatenate([_pad_cols(wa[..., QKV_A + H_A * DV_A:], LANES),
                            _pad_cols(wb[..., N_B - GLA_RANK:], LANES)], axis=-1)
    w3 = jnp.concatenate([wc, misc], axis=-1)
    return w1, w2, w3, wg


def _prep_ffn(w_up, w_down):
    return w_up.astype(BF16), w_down.astype(BF16)


def _tail_rows(p, nb, t, n_tail, width):
    return jnp.stack([p[(b + 1) * t - n_tail:(b + 1) * t, :width] for b in range(nb)])


def kernel(x_prompt, x_sample, state_gdn, state_gdn_conv, state_gla, state_rwkv, state_rwkv_shift, w_in, conv_a, a_log, dt_bias, gdn_norm, gla_gk_up, gla_gk_bias, gla_norm, rwkv_mu, rwkv_w0, rwkv_w_up, rwkv_a0, rwkv_a_up, rwkv_g_up, rwkv_k_k, rwkv_k_a, rwkv_r_k, rwkv_ln_w, rwkv_ln_b, w_branch, w_out, norm_ff1, w_ff1_up, w_ff1_down, norm_mix, norm_ff2, w_ff2_up, w_ff2_down, norm_final):
    bp, tp, d = x_prompt.shape
    bs, ts, _ = x_sample.shape
    depth = w_in.shape[0]
    n_p, n_s = bp * tp, bs * ts
    null_rows = SAMPLE_ROWS - ts
    assert tp % CHUNK == 0 and null_rows == SAMPLE_ROWS // 2 and CONV_W - 1 <= null_rows
    seq_per_chunk = CHUNK // SAMPLE_ROWS
    assert bs % seq_per_chunk == 0
    nchunk_p = tp // CHUNK
    nstep_s = bs // seq_per_chunk
    prompt = dict(nbatch=bp, nchunk=nchunk_p, nseq=1, null_rows=0, ngroup=math.gcd(bp, PROMPT_GROUPS))
    sample = dict(nbatch=nstep_s, nchunk=1, nseq=seq_per_chunk, null_rows=null_rows,
                  ngroup=math.gcd(nstep_s, SAMPLE_GROUPS))

    x = jnp.concatenate([x_prompt.reshape(n_p, d), x_sample.reshape(n_s, d)], axis=0)
    gdn_p = gdn_s = gla_p = gla_s = rwkv_p = rwkv_s = None
    conv_p, conv_s, shift_p, shift_s = [], [], [], []
    ff1 = _prep_ffn(w_ff1_up, w_ff1_down)
    ff2 = _prep_ffn(w_ff2_up, w_ff2_down)
    w1, w2, w3, wg = _prep_in(w_in)
    wbr = w_branch.astype(BF16)
    wout = w_out.astype(BF16)
    for l in range(depth):
        h = _ffn(x, norm_ff1[l], *ff1, l)

        hn = _norm_cast(h, norm_mix[l], BF16)
        p1, p1s = _proj(hn, w1, l, w1.shape[2] // 2, n_sample=n_s)
        p2, p2s = _proj(hn, w2, l, w2.shape[2] // 2, n_sample=n_s)
        p3, p3s = _proj(hn, w3, l, w3.shape[2] // 2, n_sample=n_s)
        pg = _proj(hn, wg, l, wg.shape[2] // 4, BF16)

        prm = jnp.zeros((SUBLANES, LANES), F32)
        prm = prm.at[0, H_A:2 * H_A].set(a_log[l]).at[1, H_A:2 * H_A].set(dt_bias[l])
        gnorm = gdn_norm[l].reshape(1, DV_A)
        up_w = jnp.pad(gla_gk_up[l], ((0, LANES - GLA_RANK), (0, 0))).astype(BF16)
        gk_bias = gla_gk_bias[l].reshape(1, H_B * DK_B)
        lnorm = gla_norm[l].reshape(1, DV_B)
        mu = rwkv_mu[l].reshape(1, N_CC)
        wwa = jnp.zeros((LANES, 2 * BRANCH_W), F32)
        wwa = wwa.at[0:DECAY_LORA, 0:BRANCH_W].set(rwkv_w_up[l])
        wwa = wwa.at[DECAY_LORA:DECAY_LORA + AAA_LORA, BRANCH_W:].set(rwkv_a_up[l]).astype(BF16)
        gup = rwkv_g_up[l].astype(BF16)
        vec = jnp.stack([rwkv_w0[l], rwkv_a0[l], rwkv_k_k[l], rwkv_k_a[l], rwkv_r_k[l],
                         rwkv_ln_w[l], rwkv_ln_b[l], jnp.zeros((BRANCH_W,), F32)], axis=0)

        ya_p, gdn_p = _gdn(p1, p3, None, None, gdn_p, l, depth, conv_a[l], prm, gnorm, **prompt)
        ya_s, gdn_s = _gdn(p1s, p3s, state_gdn_conv, state_gdn, gdn_s, l, depth, conv_a[l], prm, gnorm, **sample)
        yb_p, gla_p = _gla(p2, p3, None, gla_p, l, depth, up_w, gk_bias, lnorm, **prompt)
        yb_s, gla_s = _gla(p2s, p3s, state_gla, gla_s, l, depth, up_w, gk_bias, lnorm, **sample)
        yc_p, rwkv_p = _rwkv(p3, None, None, rwkv_p, l, depth, mu, wwa, gup, vec, **prompt)
        yc_s, rwkv_s = _rwkv(p3s, state_rwkv_shift, state_rwkv, rwkv_s, l, depth, mu, wwa, gup, vec, **sample)

        h = _merge([y.reshape(n_p, BRANCH_W) for y in (ya_p, yb_p, yc_p)],
                   [y.reshape(n_s, BRANCH_W) for y in (ya_s, yb_s, yc_s)], pg, wbr, wout, h, l)
        x = _ffn(h, norm_ff2[l], *ff2, l)

        conv_p.append(_tail_rows(p1, bp, tp, CONV_W - 1, QKV_A))
        conv_s.append(p1[n_p:].reshape(bs, ts, -1)[:, ts - (CONV_W - 1):, :QKV_A])
        shift_p.append(_tail_rows(p3, bp, tp, 1, N_CC))
        shift_s.append(p3[n_p:].reshape(bs, ts, -1)[:, ts - 1:, :N_CC])

    y_p = _norm_cast(x, norm_final, F32, 0, n_p)
    y_s = _norm_cast(x, norm_final, F32, n_p, n_s)
    return (y_p.reshape(bp, tp, d), y_s.reshape(bs, ts, d),
            gdn_p, gdn_s, jnp.stack(conv_p), jnp.stack(conv_s), gla_p, gla_s,
            rwkv_p, rwkv_s, jnp.stack(shift_p), jnp.stack(shift_s))
```

```python
import functools
import math

import jax
import jax.numpy as jnp
from jax import lax
from jax.experimental import pallas as pl
from jax.experimental.pallas import tpu as pltpu

F32 = jnp.float32
BF16 = jnp.bfloat16

D_MODEL = 2048
N_BRANCH = 3
BRANCH_W = 768
DK_A, DV_A, H_A, CONV_W = 128, 128, 6, 4
DK_B, DV_B, H_B, GLA_RANK, GLA_NORMALIZER = 64, 128, 6, 16, 16.0
N_C, H_C, DECAY_LORA, AAA_LORA, GATE_LORA = 64, 12, 64, 64, 128
GN_EPS = 64e-5
D_FF = 5504
NORM_EPS = 1e-6
QKV_A = 2 * H_A * DK_A + H_A * DV_A
N_A = QKV_A + H_A * DV_A + 2 * H_A
N_B = 2 * H_B * DK_B + 2 * H_B * DV_B + GLA_RANK
N_CC = 3 * BRANCH_W + DECAY_LORA + AAA_LORA + GATE_LORA
N_PAIR = H_C // 2

LANES = 128
SUBLANES = 8
CHUNK = 64
PROMPT_GROUPS = 4
SAMPLE_GROUPS = 2
SAMPLE_ROWS = 8
FF_TILE = 512
VMEM_LIMIT = 60 * 1024 * 1024


def _cparams(n_axes):
    return pltpu.CompilerParams(dimension_semantics=("arbitrary",) * n_axes,
                                vmem_limit_bytes=VMEM_LIMIT)


def _pick_tile(n, prefs):
    for t in prefs:
        if n % t == 0:
            return t
    raise ValueError(f"no tile for {n}")


def _bdot(a, b):
    return jnp.dot(a.astype(BF16), b.astype(BF16), preferred_element_type=F32)


def _bdot_nt(a, b):
    return lax.dot_general(a.astype(BF16), b.astype(BF16), (((1,), (1,)), ((), ())),
                           preferred_element_type=F32)


def _bdot_tn(a, b):
    return lax.dot_general(a.astype(BF16), b.astype(BF16), (((0,), (0,)), ((), ())),
                           preferred_element_type=F32)


def _sigmoid(x):
    return jax.nn.sigmoid(x)


def _silu(x):
    return x * _sigmoid(x)


def _softplus(x):
    return jnp.maximum(x, 0.0) + jnp.log(1.0 + jnp.exp(-jnp.abs(x)))


def _rmsnorm_rows(x, g):
    return x * lax.rsqrt(jnp.mean(x * x, axis=-1, keepdims=True) + NORM_EPS) * g


def _chunk_masks(rows, seq_len):
    sh = int(math.log2(seq_len))
    ri = lax.broadcasted_iota(jnp.int32, (rows, rows), 0)
    ci = lax.broadcasted_iota(jnp.int32, (rows, rows), 1)
    same = (ri >> sh) == (ci >> sh)
    return same & (ri >= ci), same & (ri > ci), same


def _eye(rows):
    return (lax.broadcasted_iota(jnp.int32, (rows, rows), 0)
            == lax.broadcasted_iota(jnp.int32, (rows, rows), 1)).astype(F32)


def _inv_unit_lower(lms, eye, nil):
    xs = [eye - lm for lm in lms]
    ps = list(lms)
    k = 2
    while k < nil:
        ps = [_bdot(p, p) for p in ps]
        xs = [x + _bdot(x, p) for x, p in zip(xs, ps)]
        k *= 2
    return xs


def _row_valid(rows, seq_len, null_rows, width):
    r = lax.broadcasted_iota(jnp.int32, (rows, width), 0)
    return (r & (seq_len - 1)) >= null_rows


def _seq_rows(rows, seq_len, s, width):
    r = lax.broadcasted_iota(jnp.int32, (rows, width), 0) & (CHUNK - 1)
    return (r >> int(math.log2(seq_len))) == s


def _pick(sel, xv):
    return xv if sel is None else jnp.where(sel, xv, 0.0)


def _ffn_body(x_ref, g_ref, wg_ref, wu_ref, wd_ref, o_ref, xn_ref, *, d_ff):
    j = pl.program_id(1)

    @pl.when(j == 0)
    def _():
        xn_ref[...] = _rmsnorm_rows(x_ref[...], g_ref[...]).astype(BF16)
        o_ref[...] = jnp.zeros_like(o_ref)

    dup = j * FF_TILE - _ff_start(j, d_ff)
    col_ok = lax.broadcasted_iota(jnp.int32, (1, FF_TILE), 1) >= dup
    row_ok = lax.broadcasted_iota(jnp.int32, (FF_TILE, 1), 0) >= dup
    xn = xn_ref[...]
    gate = jnp.dot(xn, wg_ref[0], preferred_element_type=F32)
    up = jnp.dot(xn, wu_ref[0], preferred_element_type=F32)
    act = jnp.where(col_ok, _silu(gate) * up, 0.0).astype(BF16)
    wd = jnp.where(row_ok, wd_ref[0], jnp.zeros((), BF16))
    o_ref[...] += jnp.dot(act, wd, preferred_element_type=F32)

    @pl.when(j == pl.num_programs(1) - 1)
    def _():
        o_ref[...] = x_ref[...] + 0.5 * o_ref[...]


def _ff_start(j, d_ff):
    return jnp.minimum(j * FF_TILE, d_ff - FF_TILE)


def _ffn(x, g, w_up, w_down, layer):
    n, d = x.shape
    tm = _pick_tile(n, (1088, 544, 512, 256, 128, 64, 32, 16, 8))
    d_ff = w_down.shape[1]
    assert d_ff % LANES == 0 and d_ff >= FF_TILE
    nf = pl.cdiv(d_ff, FF_TILE)

    def up_spec(col0):
        return pl.BlockSpec((pl.Element(1), pl.Element(d), pl.Element(FF_TILE)),
                            lambda i, j: (layer, 0, pl.multiple_of(col0 + _ff_start(j, d_ff), LANES)))

    return pl.pallas_call(
        functools.partial(_ffn_body, d_ff=d_ff),
        grid=(n // tm, nf),
        in_specs=[
            pl.BlockSpec((tm, d), lambda i, j: (i, 0)),
            pl.BlockSpec((1, d), lambda i, j: (0, 0)),
            up_spec(0),
            up_spec(d_ff),
            pl.BlockSpec((pl.Element(1), pl.Element(FF_TILE), pl.Element(d)),
                         lambda i, j: (layer, pl.multiple_of(_ff_start(j, d_ff), LANES), 0)),
        ],
        out_specs=pl.BlockSpec((tm, d), lambda i, j: (i, 0)),
        out_shape=jax.ShapeDtypeStruct((n, d), F32),
        scratch_shapes=[pltpu.VMEM((tm, d), BF16)],
        compiler_params=_cparams(2),
        name="ffn",
    )(x, g.reshape(1, d), w_up, w_up, w_down)


def _norm_cast_body(x_ref, g_ref, o_ref):
    o_ref[...] = _rmsnorm_rows(x_ref[...], g_ref[...]).astype(o_ref.dtype)


def _norm_cast(x, g, dtype, start=0, count=None):
    n, d = x.shape
    count = n - start if count is None else count
    tm = _pick_tile(math.gcd(start, count), (544, 512, 256, 128, 64, 32, 16, 8))
    first = start // tm
    return pl.pallas_call(
        _norm_cast_body,
        grid=(count // tm,),
        in_specs=[pl.BlockSpec((tm, d), lambda i: (first + i, 0)), pl.BlockSpec((1, d), lambda i: (0, 0))],
        out_specs=pl.BlockSpec((tm, d), lambda i: (i, 0)),
        out_shape=jax.ShapeDtypeStruct((count, d), dtype),
        compiler_params=_cparams(1),
        name="norm",
    )(x, g.reshape(1, d))


def _pad_rows(x):
    n, w = x.shape
    x3 = x.reshape(n // SUBLANES, SUBLANES, w)
    tail = lax.broadcasted_iota(jnp.int32, x3.shape, 1) >= SUBLANES // 2
    first = jnp.where(tail, pltpu.roll(x3, SUBLANES // 2, 1), 0.0)
    second = jnp.where(tail, x3, 0.0)
    return jnp.stack([first, second], axis=1).reshape(2 * n, w)


def _compact_rows(x):
    n, w = x.shape
    x4 = x.reshape(n // (2 * SUBLANES), 2, SUBLANES, w)
    head = lax.broadcasted_iota(jnp.int32, (n // (2 * SUBLANES), SUBLANES, w), 1) < SUBLANES // 2
    return jnp.where(head, pltpu.roll(x4[:, 0], SUBLANES // 2, 1), x4[:, 1]).reshape(n // 2, w)


WPREP_ROWS = 256


def _wprep_body(*refs, depth, n_main, misc):
    if misc:
        w_ref, ma_ref, mb_ref, o_ref = refs
    else:
        w_ref, o_ref = refs
    i = pl.program_id(0)

    @pl.when(i < n_main)
    def _():
        for l in range(depth):
            o_ref[l] = w_ref[:, l, :].astype(BF16)

    if misc:
        na, nb = misc

        @pl.when(i == n_main)
        def _():
            rows = lax.broadcasted_iota(jnp.int32, ma_ref.shape[0:1] + ma_ref.shape[2:3], 0)
            for l in range(depth):
                o_ref[l] = jnp.zeros(o_ref.shape[1:], BF16)
                o_ref[l, 0:ma_ref.shape[0], :] = jnp.where(rows < na, ma_ref[:, l, :], 0.0).astype(BF16)
                o_ref[l, LANES:LANES + mb_ref.shape[0], :] = jnp.where(rows < nb, mb_ref[:, l, :], 0.0).astype(BF16)


def _wprep(wt, row0, nrows, misc=None):
    n_all, depth, k = wt.shape
    assert nrows % WPREP_ROWS == 0
    n_main = nrows // WPREP_ROWS

    def win(rows, start_of):
        return pl.BlockSpec((pl.Element(rows), pl.Element(depth), pl.Element(k)),
                            lambda i: (start_of(i), 0, 0))

    in_specs = [win(WPREP_ROWS, lambda i: row0 + jnp.minimum(i, n_main - 1) * WPREP_ROWS)]
    args = [wt]
    body_misc = None
    if misc:
        row_a, n_a, row_b, n_b = misc
        small = 2 * SUBLANES
        assert n_a <= small and n_b <= small
        in_specs += [win(small, lambda i: row_a), win(small, lambda i: row_b)]
        args += [wt, wt]
        body_misc = (n_a, n_b)
    nblk = n_main + (1 if misc else 0)
    return pl.pallas_call(
        functools.partial(_wprep_body, depth=depth, n_main=n_main, misc=body_misc),
        grid=(nblk,),
        in_specs=in_specs,
        out_specs=pl.BlockSpec((depth, WPREP_ROWS, k), lambda i: (0, i, 0)),
        out_shape=jax.ShapeDtypeStruct((depth, nblk * WPREP_ROWS, k), BF16),
        compiler_params=_cparams(1),
        name="wprep",
    )(*args)


def _proj_body(x_ref, w_ref, o_ref, *pad_ref, sample_off, n_sample):
    res = lax.dot_general(x_ref[...], w_ref[...], (((1,), (1,)), ((), ())), preferred_element_type=F32)
    o_ref[...] = res.astype(o_ref.dtype)
    if pad_ref:
        @pl.when(pl.program_id(1) == pl.num_programs(1) - 1)
        def _():
            pad_ref[0][...] = _pad_rows(res[sample_off:sample_off + n_sample])


def _proj(xn, w, layer, tn, out_dtype=F32, n_sample=0):
    n, d = xn.shape
    nn = w.shape[1]
    tm = _pick_tile(n, (1088, 544, 512, 256, 128, 64, 32, 16))
    out_specs = [pl.BlockSpec((tm, tn), lambda j, i: (i, j))]
    out_shape = [jax.ShapeDtypeStruct((n, nn), out_dtype)]
    sample_off = tm - n_sample
    if n_sample:
        assert 0 <= sample_off and sample_off % SUBLANES == 0 and n_sample % SUBLANES == 0
        out_specs.append(pl.BlockSpec((2 * n_sample, tn), lambda j, i: (0, j)))
        out_shape.append(jax.ShapeDtypeStruct((2 * n_sample, nn), F32))
    out = pl.pallas_call(
        functools.partial(_proj_body, sample_off=sample_off, n_sample=n_sample),
        grid=(nn // tn, n // tm),
        in_specs=[
            pl.BlockSpec((tm, d), lambda j, i: (i, 0)),
            pl.BlockSpec((None, tn, d), lambda j, i: (layer, j, 0)),
        ],
        out_specs=out_specs,
        out_shape=out_shape,
        compiler_params=_cparams(2),
        name="proj",
    )(xn, w)
    return out if n_sample else out[0]


MERGE_TN = 2048


def _merge_body(yap_ref, ybp_ref, ycp_ref, yas_ref, ybs_ref, ycs_ref, ga_ref, gb_ref, gc_ref,
                wb_ref, wo_ref, h_ref, o_ref, *, prompt_tiles):
    i = pl.program_id(0)
    j = pl.program_id(1)

    @pl.when(j == 0)
    def _():
        o_ref[...] = h_ref[...]

    is_prompt = i < prompt_tiles
    ya = jnp.where(is_prompt, yap_ref[...], yas_ref[...])
    yb = jnp.where(is_prompt, ybp_ref[...], ybs_ref[...])
    yc = jnp.where(is_prompt, ycp_ref[...], ycs_ref[...])
    m = _sigmoid(ga_ref[...].astype(F32)) * jnp.dot(ya, wb_ref[0], preferred_element_type=F32)
    m += _sigmoid(gb_ref[...].astype(F32)) * jnp.dot(yb, wb_ref[1], preferred_element_type=F32)
    m += _sigmoid(gc_ref[...].astype(F32)) * jnp.dot(yc, wb_ref[2], preferred_element_type=F32)
    o_ref[...] += jnp.dot(m.astype(BF16), wo_ref[...], preferred_element_type=F32)


def _merge(ys_prompt, ys_sample, pg, wb, wo, h, layer):
    n, d = h.shape
    n_p, n_s = ys_prompt[0].shape[0], ys_sample[0].shape[0]
    tm = _pick_tile(math.gcd(n_p, n_s), (256, 128, 64, 32, 16))
    prompt_tiles = n_p // tm
    nj = d // MERGE_TN
    yp_spec = pl.BlockSpec((tm, BRANCH_W), lambda i, j: (jnp.minimum(i, prompt_tiles - 1), 0))
    ys_spec = pl.BlockSpec((tm, BRANCH_W), lambda i, j: (jnp.maximum(i - prompt_tiles, 0), 0))
    return pl.pallas_call(
        functools.partial(_merge_body, prompt_tiles=prompt_tiles),
        grid=(n // tm, nj),
        in_specs=[
            yp_spec, yp_spec, yp_spec, ys_spec, ys_spec, ys_spec,
            pl.BlockSpec((tm, MERGE_TN), lambda i, j: (i, j)),
            pl.BlockSpec((tm, MERGE_TN), lambda i, j: (i, nj + j)),
            pl.BlockSpec((tm, MERGE_TN), lambda i, j: (i, 2 * nj + j)),
            pl.BlockSpec((None, N_BRANCH, BRANCH_W, MERGE_TN), lambda i, j: (layer, 0, 0, j),
                         pipeline_mode=pl.Buffered(1)),
            pl.BlockSpec((None, MERGE_TN, d), lambda i, j: (layer, j, 0), pipeline_mode=pl.Buffered(1)),
            pl.BlockSpec((tm, d), lambda i, j: (i, 0)),
        ],
        out_specs=pl.BlockSpec((tm, d), lambda i, j: (i, 0)),
        out_shape=jax.ShapeDtypeStruct((n, d), F32),
        compiler_params=_cparams(2),
        name="merge",
    )(*ys_prompt, *ys_sample, pg, pg, pg, wb, wo, h)


def _layer_state_spec(layer, nseq, tail):
    zeros = (0,) * len(tail)
    return pl.BlockSpec((None, nseq) + tail, lambda b, c: (layer, b) + zeros)


def _stack_io(stack, depth, nstate, tail, n_args):
    shape = jax.ShapeDtypeStruct((depth, nstate) + tail, F32)
    if stack is None:
        return [], [], shape, {}
    return [pl.BlockSpec(memory_space=pl.ANY)], [stack], shape, {n_args: 1}


def _split3_dot(lmat, x):
    lb = lmat.astype(BF16)
    hi = x.astype(BF16)
    r1 = x - hi.astype(F32)
    mid = r1.astype(BF16)
    lo = (r1 - mid.astype(F32)).astype(BF16)
    return (jnp.dot(lb, hi, preferred_element_type=F32) + jnp.dot(lb, mid, preferred_element_type=F32)
            + jnp.dot(lb, lo, preferred_element_type=F32))


def _gdn_body(*refs, ngroup, nseq, null_rows, has_state, has_stack):
    refs = list(refs)
    xbuf = refs.pop()
    so_ref = refs.pop()
    y_ref = refs.pop()
    if has_stack:
        refs.pop()
    row_refs = [refs[3 * g:3 * g + 3] for g in range(ngroup)]
    refs = refs[3 * ngroup:]
    if has_state:
        cs_ref, cw_ref, prm_ref, nw_ref, s0_ref = refs
    else:
        cw_ref, prm_ref, nw_ref = refs
        s0_ref = so_ref
    rows = CHUNK
    seq_len = rows // nseq
    c = pl.program_id(1)
    chains = [(g, h) for g in range(ngroup) for h in range(H_A)]

    @pl.when(c == 0)
    def _():
        for g in range(ngroup):
            xbuf[g, 0:SUBLANES, :] = jnp.zeros((SUBLANES, QKV_A), F32)
        if not has_state:
            so_ref[...] = jnp.zeros_like(so_ref)

    @pl.when(c > 0)
    def _():
        for g in range(ngroup):
            xbuf[g, 0:SUBLANES, :] = xbuf[g, rows:rows + SUBLANES, :]

    for g in range(ngroup):
        xbuf[g, SUBLANES:SUBLANES + rows, :] = row_refs[g][0][...]
        if has_state:
            for s in range(nseq):
                r0 = SUBLANES + s * seq_len + null_rows - (CONV_W - 1)
                xbuf[g, r0:r0 + CONV_W - 1, :] = cs_ref[g * nseq + s]

    incl, strict, same = _chunk_masks(rows, seq_len)
    eye = _eye(rows)
    valid = _row_valid(rows, seq_len, null_rows, LANES) if null_rows else None
    sels = [_seq_rows(rows, seq_len, s, LANES) for s in range(nseq)] if nseq > 1 else [None]
    sels2 = [_seq_rows(2 * rows, seq_len, s, LANES) for s in range(nseq)] if nseq > 1 else [None]
    lstack = jnp.concatenate([incl.astype(F32), same.astype(F32)], axis=0)

    def conv_cols(g, c0):
        acc = xbuf[g, SUBLANES - 3:SUBLANES - 3 + rows, c0:c0 + LANES] * cw_ref[0:1, c0:c0 + LANES]
        for j in range(1, CONV_W):
            acc += (xbuf[g, SUBLANES - 3 + j:SUBLANES - 3 + j + rows, c0:c0 + LANES]
                    * cw_ref[j:j + 1, c0:c0 + LANES])
        return _silu(acc)

    qs, ks, kbs, rhs, decays, wq_parts, kds, gtcs = [], [], [], [], [], [], [], []
    for g in range(ngroup):
        ba = row_refs[g][2][...]
        beta_all = _sigmoid(ba)
        g_all = -jnp.exp(prm_ref[0:1, :]) * _softplus(ba + prm_ref[1:2, :])
        if null_rows:
            g_all = jnp.where(valid, g_all, 0.0)
        cs = _split3_dot(lstack, g_all)
        gc_all, gt_all = cs[0:rows], cs[rows:2 * rows]
        gc_t = gc_all.T
        for h in range(H_A):
            q = conv_cols(g, h * DK_A)
            k = conv_cols(g, H_A * DK_A + h * DK_A)
            v = conv_cols(g, 2 * H_A * DK_A + h * DV_A)
            q = q * lax.rsqrt(jnp.sum(q * q, axis=-1, keepdims=True) + 1e-6) * DK_A ** -0.5
            k = k * lax.rsqrt(jnp.sum(k * k, axis=-1, keepdims=True) + 1e-6)
            if null_rows:
                q = jnp.where(valid, q, 0.0)
                k = jnp.where(valid, k, 0.0)
                v = jnp.where(valid, v, 0.0)
            beta = beta_all[:, h:h + 1]
            gcc = gc_all[:, H_A + h:H_A + h + 1]
            gtc = gt_all[:, H_A + h:H_A + h + 1]
            gcr = gc_t[H_A + h:H_A + h + 1, :]
            egc = jnp.exp(gcc)
            kb = k * beta
            qs.append(q)
            ks.append(k)
            kbs.append(kb)
            rhs.append(jnp.concatenate([v * beta, kb * egc], axis=1))
            decays.append(jnp.where(incl, jnp.exp(jnp.where(incl, gcc - gcr, 0.0)), 0.0))
            wq_parts.append(q * egc)
            kds.append(k * jnp.exp(gtc - gcc))
            gtcs.append(gtc)

    n = range(len(chains))
    lms = [jnp.where(strict, _bdot_nt(kbs[i], ks[i]) * decays[i], 0.0) for i in n]
    attns = [_bdot_nt(qs[i], ks[i]) * decays[i] for i in n]
    tmats = _inv_unit_lower(lms, eye, seq_len)
    uws = [_bdot(tmats[i], rhs[i]) for i in n]
    us, os_ = [], []
    for i, (g, h) in enumerate(chains):
        u = uws[i][:, 0:DV_A]
        wq = jnp.concatenate([uws[i][:, DV_A:DV_A + DK_A], wq_parts[i]], axis=0)
        o = None
        for s in range(nseq):
            both = _bdot(_pick(sels2[s], wq), s0_ref[g * nseq + s, h])
            u = u - both[0:rows]
            o = both[rows:2 * rows] if o is None else o + both[rows:2 * rows]
        us.append(u)
        os_.append(o)
    os_ = [os_[i] + _bdot(attns[i], us[i]) for i in n]
    for i, (g, h) in enumerate(chains):
        for s in range(nseq):
            g_last = jnp.exp(gtcs[i][s * seq_len:s * seq_len + 1, :])
            so_ref[g * nseq + s, h] = (g_last * s0_ref[g * nseq + s, h]
                                       + _bdot_tn(_pick(sels[s], kds[i]), us[i]))
    for i, (g, h) in enumerate(chains):
        z = row_refs[g][1][:, h * DV_A:(h + 1) * DV_A]
        _store_y(y_ref, g, h * DV_A, _rmsnorm_rows(os_[i], nw_ref[...]) * _silu(z), null_rows)


def _row_spec(width, col, g, ngroup, nchunk):
    return pl.BlockSpec((CHUNK, width), lambda b, c: ((b * ngroup + g) * nchunk + c, col))


def _y_io(nbatch, nchunk, ngroup, null_rows):
    out_rows = CHUNK // 2 if null_rows else CHUNK
    return (pl.BlockSpec((ngroup, out_rows, BRANCH_W), lambda b, c: (b, c, 0)),
            jax.ShapeDtypeStruct((nbatch, nchunk * out_rows, BRANCH_W), BF16))


def _store_y(y_ref, g, c0, yv, null_rows):
    if null_rows:
        yv = _compact_rows(yv)
    y_ref[g, :, c0:c0 + yv.shape[1]] = yv.astype(y_ref.dtype)


def _gdn(p1, p3, conv_state, state, stack, layer, depth, conv_w, prm, norm_w, *,
         nbatch, nchunk, nseq, null_rows, ngroup):
    has_state = state is not None
    tail = (H_A, DK_A, DV_A)
    in_specs, args = [], []
    for g in range(ngroup):
        in_specs += [_row_spec(QKV_A, 0, g, ngroup, nchunk),
                     _row_spec(BRANCH_W, QKV_A // BRANCH_W, g, ngroup, nchunk),
                     _row_spec(LANES, N_CC // LANES, g, ngroup, nchunk)]
        args += [p1, p1, p3]
    if has_state:
        in_specs.append(_layer_state_spec(layer, ngroup * nseq, (CONV_W - 1, QKV_A)))
        args.append(conv_state)
    in_specs += [
        pl.BlockSpec((CONV_W, QKV_A), lambda b, c: (0, 0)),
        pl.BlockSpec((SUBLANES, LANES), lambda b, c: (0, 0)),
        pl.BlockSpec((1, DV_A), lambda b, c: (0, 0)),
    ]
    args += [conv_w, prm, norm_w]
    if has_state:
        in_specs.append(_layer_state_spec(layer, ngroup * nseq, tail))
        args.append(state)
    stack_specs, stack_args, stack_shape, aliases = _stack_io(stack, depth, nbatch * nseq, tail, len(args))
    y_spec, y_shape = _y_io(nbatch, nchunk, ngroup, null_rows)
    return pl.pallas_call(
        functools.partial(_gdn_body, ngroup=ngroup, nseq=nseq, null_rows=null_rows, has_state=has_state,
                          has_stack=stack is not None),
        grid=(nbatch // ngroup, nchunk),
        in_specs=in_specs + stack_specs,
        out_specs=[y_spec, _layer_state_spec(layer, ngroup * nseq, tail)],
        out_shape=[y_shape, stack_shape],
        input_output_aliases=aliases,
        scratch_shapes=[pltpu.VMEM((ngroup, CHUNK + SUBLANES, QKV_A), F32)],
        compiler_params=_cparams(2),
        name="gdn",
    )(*args, *stack_args)


def _gla_body(*refs, ngroup, nseq, null_rows, has_state, has_stack):
    refs = list(refs)
    so_ref = refs.pop()
    y_ref = refs.pop()
    if has_stack:
        refs.pop()
    row_refs = [refs[5 * g:5 * g + 5] for g in range(ngroup)]
    refs = refs[5 * ngroup:]
    if has_state:
        up_ref, bias_ref, nw_ref, s0_ref = refs
    else:
        up_ref, bias_ref, nw_ref = refs
        s0_ref = so_ref
    rows = CHUNK
    seq_len = rows // nseq
    c = pl.program_id(1)
    chains = [(g, h) for g in range(ngroup) for h in range(H_B)]
    if not has_state:
        @pl.when(c == 0)
        def _():
            so_ref[...] = jnp.zeros_like(so_ref)

    incl, _, same = _chunk_masks(rows, seq_len)
    ci = lax.broadcasted_iota(jnp.int32, (rows, rows), 1)
    first_half = same & ((ci & (seq_len - 1)) < seq_len // 2)
    wide = H_B * DK_B
    lstack = jnp.concatenate([incl.astype(F32), same.astype(F32), first_half.astype(F32)], axis=0)
    valid_w = _row_valid(rows, seq_len, null_rows, wide) if null_rows else None
    valid_k = _row_valid(rows, seq_len, null_rows, DK_B) if null_rows else None
    valid_v = _row_valid(rows, seq_len, null_rows, DV_B) if null_rows else None
    sels = [_seq_rows(rows, seq_len, s, DK_B) for s in range(nseq)] if nseq > 1 else [None]

    qis, kis, vs, q_ins, k_decs, a_lasts = [], [], [], [], [], []
    for g in range(ngroup):
        q_ref, k_ref, v_ref, _, gkl_ref = row_refs[g]
        x = _bdot(gkl_ref[...], up_ref[...]) + bias_ref[...]
        gk_all = (jnp.minimum(x, 0.0) - jnp.log(1.0 + jnp.exp(-jnp.abs(x)))) / GLA_NORMALIZER
        if null_rows:
            gk_all = jnp.where(valid_w, gk_all, 0.0)
        cs = _split3_dot(lstack, gk_all)
        bc_all, bt_all, an_all = cs[0:rows], cs[rows:2 * rows], cs[2 * rows:3 * rows]
        bt_t = bt_all.T
        for h in range(H_B):
            c0 = h * DK_B
            q = q_ref[:, c0:c0 + DK_B] * DK_B ** -0.5
            k = k_ref[:, c0:c0 + DK_B]
            v = v_ref[:, h * DV_B:(h + 1) * DV_B]
            if null_rows:
                q = jnp.where(valid_k, q, 0.0)
                k = jnp.where(valid_k, k, 0.0)
                v = jnp.where(valid_v, v, 0.0)
            bc = bc_all[:, c0:c0 + DK_B]
            bt = bt_all[:, c0:c0 + DK_B]
            an = an_all[:, c0:c0 + DK_B]
            qis.append(q * jnp.exp(bc - an))
            kis.append(k * jnp.exp(an - bc))
            vs.append(v)
            q_ins.append(q * jnp.exp(bc))
            k_decs.append(k * jnp.exp(bt - bc))
            a_lasts.append([jnp.exp(bt_t[c0:c0 + DK_B, s * seq_len:s * seq_len + 1])
                            for s in range(nseq)])

    n = range(len(chains))
    attns = [jnp.where(incl, _bdot_nt(qis[i], kis[i]), 0.0) for i in n]
    os_ = [_bdot(attns[i], vs[i]) for i in n]
    for i, (g, h) in enumerate(chains):
        for s in range(nseq):
            os_[i] = os_[i] + _bdot(_pick(sels[s], q_ins[i]), s0_ref[g * nseq + s, h])
    for i, (g, h) in enumerate(chains):
        for s in range(nseq):
            so_ref[g * nseq + s, h] = (a_lasts[i][s] * s0_ref[g * nseq + s, h]
                                       + _bdot_tn(_pick(sels[s], k_decs[i]), vs[i]))
    for i, (g, h) in enumerate(chains):
        gate = row_refs[g][3][:, h * DV_B:(h + 1) * DV_B]
        _store_y(y_ref, g, h * DV_B, _rmsnorm_rows(os_[i], nw_ref[...]) * _silu(gate), null_rows)


def _gla(p2, p3, state, stack, layer, depth, up_w, bias, norm_w, *, nbatch, nchunk, nseq, null_rows, ngroup):
    has_state = state is not None
    wide = H_B * DK_B
    tail = (H_B, DK_B, DV_B)
    in_specs, args = [], []
    for g in range(ngroup):
        in_specs += [_row_spec(wide, 0, g, ngroup, nchunk), _row_spec(wide, 1, g, ngroup, nchunk),
                     _row_spec(BRANCH_W, 1, g, ngroup, nchunk), _row_spec(BRANCH_W, 2, g, ngroup, nchunk)]
        in_specs.append(_row_spec(LANES, N_CC // LANES + 1, g, ngroup, nchunk))
        args += [p2, p2, p2, p2, p3]
    in_specs += [pl.BlockSpec((LANES, wide), lambda b, c: (0, 0)),
                 pl.BlockSpec((1, wide), lambda b, c: (0, 0)),
                 pl.BlockSpec((1, DV_B), lambda b, c: (0, 0))]
    args += [up_w, bias, norm_w]
    if has_state:
        in_specs.append(_layer_state_spec(layer, ngroup * nseq, tail))
        args.append(state)
    stack_specs, stack_args, stack_shape, aliases = _stack_io(stack, depth, nbatch * nseq, tail, len(args))
    y_spec, y_shape = _y_io(nbatch, nchunk, ngroup, null_rows)
    return pl.pallas_call(
        functools.partial(_gla_body, ngroup=ngroup, nseq=nseq, null_rows=null_rows, has_state=has_state,
                          has_stack=stack is not None),
        grid=(nbatch // ngroup, nchunk),
        in_specs=in_specs + stack_specs,
        out_specs=[y_spec, _layer_state_spec(layer, ngroup * nseq, tail)],
        out_shape=[y_shape, stack_shape],
        input_output_aliases=aliases,
        compiler_params=_cparams(2),
        name="gla",
    )(*args, *stack_args)


def _rwkv_body(*refs, ngroup, nseq, null_rows, has_state, has_stack):
    refs = list(refs)
    sp_ref = refs.pop()
    pbuf = refs.pop()
    so_ref = refs.pop()
    y_ref = refs.pop()
    if has_stack:
        refs.pop()
    pc_refs = refs[:ngroup]
    refs = refs[ngroup:]
    if has_state:
        sh_ref, mu_ref, wwa_ref, gup_ref, vec_ref, s0_ref = refs
    else:
        mu_ref, wwa_ref, gup_ref, vec_ref = refs
    rows = CHUNK
    seq_len = rows // nseq
    c = pl.program_id(1)
    chains = [(g, p) for g in range(ngroup) for p in range(N_PAIR)]
    pairs = range(len(chains))

    @pl.when(c == 0)
    def _():
        for g in range(ngroup):
            pbuf[g, 0:SUBLANES, :] = jnp.zeros((SUBLANES, N_CC), F32)
        if not has_state:
            sp_ref[...] = jnp.zeros_like(sp_ref)

    @pl.when(c > 0)
    def _():
        for g in range(ngroup):
            pbuf[g, 0:SUBLANES, :] = pbuf[g, rows:rows + SUBLANES, :]

    for g in range(ngroup):
        pbuf[g, SUBLANES:SUBLANES + rows, :] = pc_refs[g][...]
        if has_state:
            for s in range(nseq):
                r0 = SUBLANES + s * seq_len + null_rows - 1
                pbuf[g, r0:r0 + 1, :] = sh_ref[g * nseq + s]

    def xc_cols(g, c0, w):
        cur = pbuf[g, SUBLANES:SUBLANES + rows, c0:c0 + w]
        prev = pbuf[g, SUBLANES - 1:SUBLANES - 1 + rows, c0:c0 + w]
        return cur + (prev - cur) * mu_ref[:, c0:c0 + w]

    incl, strict, same = _chunk_masks(rows, seq_len)
    eye = _eye(rows)
    lane = lax.broadcasted_iota(jnp.int32, (rows, LANES), 1)
    lo = lane < N_C
    lo2 = lax.broadcasted_iota(jnp.int32, (2 * rows, LANES), 1) < N_C
    valid = _row_valid(rows, seq_len, null_rows, LANES) if null_rows else None
    valid_w = _row_valid(rows, seq_len, null_rows, BRANCH_W) if null_rows else None
    sels2 = [_seq_rows(2 * rows, seq_len, s, LANES) for s in range(nseq)] if nseq > 1 else [None]
    lstack = jnp.concatenate([incl.astype(F32), same.astype(F32)], axis=0)
    w0 = vec_ref[0:1, :]
    a0 = vec_ref[1:2, :]

    r2 = lax.broadcasted_iota(jnp.int32, (2 * N_C, LANES), 0)
    c2 = lax.broadcasted_iota(jnp.int32, (2 * N_C, LANES), 1)
    blockdiag = (r2 >= N_C) == (c2 >= N_C)

    def seg_sum(xv):
        s_lo = jnp.sum(jnp.where(lo, xv, 0.0), axis=-1, keepdims=True)
        s_hi = jnp.sum(jnp.where(lo, 0.0, xv), axis=-1, keepdims=True)
        return jnp.where(lo, s_lo, s_hi)

    def by_head(stacked):
        return jnp.where(lo, stacked[0:rows], stacked[rows:2 * rows])

    rs, k2s, vs, ar_stacks, b_ts, k_ts, tots, gates = [], [], [], [], [], [], [], []
    for g in range(ngroup):
        lora = xc_cols(g, 3 * BRANCH_W, DECAY_LORA + AAA_LORA + GATE_LORA)
        wa_in = lora[:, 0:LANES]
        wa_in = jnp.where(lo, jnp.tanh(wa_in), wa_in)
        wa = _bdot(wa_in, wwa_ref[...])
        g_all = _bdot(_sigmoid(lora[:, LANES:2 * LANES]), gup_ref[...])
        w_log = -_softplus(-(w0 + wa[:, 0:BRANCH_W])) - 0.5
        lw_all = -jnp.exp(w_log)
        a_all = _sigmoid(a0 + wa[:, BRANCH_W:2 * BRANCH_W])
        if null_rows:
            lw_all = jnp.where(valid_w, lw_all, 0.0)
        cs = _split3_dot(lstack, lw_all)
        cum_all, tot_all = cs[0:rows], cs[rows:2 * rows]
        for p in range(N_PAIR):
            c0 = p * LANES
            r = xc_cols(g, c0, LANES)
            k = xc_cols(g, BRANCH_W + c0, LANES)
            v = xc_cols(g, 2 * BRANCH_W + c0, LANES)
            a_p = a_all[:, c0:c0 + LANES]
            kkr = k * vec_ref[2:3, c0:c0 + LANES]
            kk = kkr * lax.rsqrt(seg_sum(kkr * kkr) + 1e-6)
            k2 = k * (1.0 + (a_p - 1.0) * vec_ref[3:4, c0:c0 + LANES])
            av = -kk
            bv = kk * a_p
            if null_rows:
                r = jnp.where(valid, r, 0.0)
                k2 = jnp.where(valid, k2, 0.0)
                v = jnp.where(valid, v, 0.0)
                av = jnp.where(valid, av, 0.0)
                bv = jnp.where(valid, bv, 0.0)
            cum = cum_all[:, c0:c0 + LANES]
            lw = lw_all[:, c0:c0 + LANES]
            g_inv = jnp.exp(-cum)
            rs.append(r)
            k2s.append(k2)
            vs.append(v)
            ar_stacks.append(jnp.concatenate([av * jnp.exp(cum - lw), r * jnp.exp(cum)], axis=0))
            b_ts.append(bv * g_inv)
            k_ts.append(k2 * g_inv)
            tots.append(tot_all[:, c0:c0 + LANES])
            gates.append(g_all[:, c0:c0 + LANES])

    a_abs, a_rbs, a_aks, a_rks = [], [], [], []
    for p in pairs:
        for hh in range(2):
            lhs = jnp.where(lo2 if hh == 0 else jnp.logical_not(lo2), ar_stacks[p], 0.0)
            mb = _bdot_nt(lhs, b_ts[p])
            mk = _bdot_nt(lhs, k_ts[p])
            a_abs.append(jnp.where(strict, mb[0:rows], 0.0))
            a_rbs.append(jnp.where(incl, mb[rows:2 * rows], 0.0))
            a_aks.append(jnp.where(strict, mk[0:rows], 0.0))
            a_rks.append(jnp.where(incl, mk[rows:2 * rows], 0.0))
    tmats = _inv_unit_lower([-m for m in a_abs], eye, seq_len)

    def pair_stack(mats, p):
        return jnp.concatenate([mats[2 * p], mats[2 * p + 1]], axis=0)

    states, xss = [], []
    for i, (g, p) in enumerate(chains):
        sps, xs = [], None
        for s in range(nseq):
            if has_state:
                sq = g * nseq + s
                sv = jnp.concatenate([s0_ref[sq, 2 * p], s0_ref[sq, 2 * p + 1]], axis=0)
                sp = jnp.where(blockdiag, jnp.concatenate([sv, sv], axis=1), 0.0)
            else:
                sp = sp_ref[i]
            both = _bdot_nt(_pick(sels2[s], ar_stacks[i]), sp)
            xs = both if xs is None else xs + both
            sps.append(sp)
        states.append(sps)
        xss.append(xs)
    yvs = [xss[i][0:rows] + by_head(_bdot(pair_stack(a_aks, i), vs[i])) for i in pairs]
    us = [by_head(_bdot(pair_stack(tmats, i), yvs[i])) for i in pairs]
    os_ = [xss[i][rows:2 * rows]
           + by_head(_bdot(pair_stack(a_rbs, i), us[i]) + _bdot(pair_stack(a_rks, i), vs[i]))
           for i in pairs]
    for i, (g, p) in enumerate(chains):
        uv = jnp.concatenate([us[i], vs[i]], axis=0)
        bk = jnp.concatenate([b_ts[i], k_ts[i]], axis=0)
        for s in range(nseq):
            upd = jnp.where(blockdiag, _bdot_tn(_pick(sels2[s], uv), bk), 0.0)
            sp_new = (states[i][s] + upd) * jnp.exp(tots[i][s * seq_len:s * seq_len + 1, :])
            if has_state:
                sq = g * nseq + s
                so_ref[sq, 2 * p] = sp_new[0:N_C, 0:N_C]
                so_ref[sq, 2 * p + 1] = pltpu.roll(sp_new[N_C:2 * N_C, :], N_C, 1)[:, 0:N_C]
            else:
                sp_ref[i] = sp_new

    for i, (g, p) in enumerate(chains):
        c0 = p * LANES
        o = os_[i]
        mean = seg_sum(o) * (1.0 / N_C)
        d = o - mean
        var = seg_sum(d * d) * (1.0 / N_C)
        on = d * lax.rsqrt(var + GN_EPS) * vec_ref[5:6, c0:c0 + LANES] + vec_ref[6:7, c0:c0 + LANES]
        bonus = seg_sum(rs[i] * k2s[i] * vec_ref[4:5, c0:c0 + LANES]) * vs[i]
        _store_y(y_ref, g, c0, (on + bonus) * gates[i], null_rows)

    if not has_state:
        @pl.when(c == pl.num_programs(1) - 1)
        def _():
            for i, (g, p) in enumerate(chains):
                sp = sp_ref[i]
                so_ref[g, 2 * p] = sp[0:N_C, 0:N_C]
                so_ref[g, 2 * p + 1] = pltpu.roll(sp[N_C:2 * N_C, :], N_C, 1)[:, 0:N_C]


def _rwkv(p3, shift_state, state, stack, layer, depth, mu, wwa, gup, vec, *,
          nbatch, nchunk, nseq, null_rows, ngroup):
    has_state = state is not None
    tail = (H_C, N_C, N_C)
    in_specs = [_row_spec(N_CC, 0, g, ngroup, nchunk) for g in range(ngroup)]
    args = [p3] * ngroup
    if has_state:
        in_specs.append(_layer_state_spec(layer, ngroup * nseq, (1, N_CC)))
        args.append(shift_state)
    in_specs += [
        pl.BlockSpec((1, N_CC), lambda b, c: (0, 0)),
        pl.BlockSpec((LANES, 2 * BRANCH_W), lambda b, c: (0, 0)),
        pl.BlockSpec((GATE_LORA, BRANCH_W), lambda b, c: (0, 0)),
        pl.BlockSpec((SUBLANES, BRANCH_W), lambda b, c: (0, 0)),
    ]
    args += [mu, wwa, gup, vec]
    if has_state:
        in_specs.append(_layer_state_spec(layer, ngroup * nseq, tail))
        args.append(state)
    stack_specs, stack_args, stack_shape, aliases = _stack_io(stack, depth, nbatch * nseq, tail, len(args))
    y_spec, y_shape = _y_io(nbatch, nchunk, ngroup, null_rows)
    return pl.pallas_call(
        functools.partial(_rwkv_body, ngroup=ngroup, nseq=nseq, null_rows=null_rows, has_state=has_state,
                          has_stack=stack is not None),
        grid=(nbatch // ngroup, nchunk),
        in_specs=in_specs + stack_specs,
        out_specs=[y_spec, _layer_state_spec(layer, ngroup * nseq, tail)],
        out_shape=[y_shape, stack_shape],
        input_output_aliases=aliases,
        scratch_shapes=[pltpu.VMEM((ngroup, CHUNK + SUBLANES, N_CC), F32),
                        pltpu.VMEM((ngroup * N_PAIR, 2 * N_C, 2 * N_C), F32)],
        compiler_params=_cparams(2),
        name="rwkv",
    )(*args, *stack_args)


def _prep_in(w_in):
    wt = jnp.transpose(w_in, (2, 0, 1))
    ab = QKV_A + H_A * DV_A
    w1 = _wprep(wt, 0, ab)
    w2 = _wprep(wt, N_A, N_B - GLA_RANK)
    w3 = _wprep(wt, N_A + N_B, N_CC, misc=(ab, 2 * H_A, N_A + N_B - GLA_RANK, GLA_RANK))
    wg = _wprep(wt, N_A + N_B + N_CC, N_BRANCH * D_MODEL)
    return w1, w2, w3, wg


def _prep_ffn(w_up, w_down):
    return w_up.astype(BF16), w_down.astype(BF16)


def _tail_rows(p, nb, t, n_tail, width):
    return jnp.stack([p[(b + 1) * t - n_tail:(b + 1) * t, :width] for b in range(nb)])


def kernel(x_prompt, x_sample, state_gdn, state_gdn_conv, state_gla, state_rwkv, state_rwkv_shift, w_in, conv_a, a_log, dt_bias, gdn_norm, gla_gk_up, gla_gk_bias, gla_norm, rwkv_mu, rwkv_w0, rwkv_w_up, rwkv_a0, rwkv_a_up, rwkv_g_up, rwkv_k_k, rwkv_k_a, rwkv_r_k, rwkv_ln_w, rwkv_ln_b, w_branch, w_out, norm_ff1, w_ff1_up, w_ff1_down, norm_mix, norm_ff2, w_ff2_up, w_ff2_down, norm_final):
    bp, tp, d = x_prompt.shape
    bs, ts, _ = x_sample.shape
    depth = w_in.shape[0]
    n_p, n_s = bp * tp, bs * ts
    null_rows = SAMPLE_ROWS - ts
    assert tp % CHUNK == 0 and null_rows == SAMPLE_ROWS // 2 and CONV_W - 1 <= null_rows
    seq_per_chunk = CHUNK // SAMPLE_ROWS
    assert bs % seq_per_chunk == 0
    nchunk_p = tp // CHUNK
    nstep_s = bs // seq_per_chunk
    prompt = dict(nbatch=bp, nchunk=nchunk_p, nseq=1, null_rows=0, ngroup=math.gcd(bp, PROMPT_GROUPS))
    sample = dict(nbatch=nstep_s, nchunk=1, nseq=seq_per_chunk, null_rows=null_rows,
                  ngroup=math.gcd(nstep_s, SAMPLE_GROUPS))

    x = jnp.concatenate([x_prompt.reshape(n_p, d), x_sample.reshape(n_s, d)], axis=0)
    gdn_p = gdn_s = gla_p = gla_s = rwkv_p = rwkv_s = None
    conv_p, conv_s, shift_p, shift_s = [], [], [], []
    ff1 = _prep_ffn(w_ff1_up, w_ff1_down)
    ff2 = _prep_ffn(w_ff2_up, w_ff2_down)
    w1, w2, w3, wg = _prep_in(w_in)
    wbr = w_branch.astype(BF16)
    wout = w_out.astype(BF16)
    for l in range(depth):
        h = _ffn(x, norm_ff1[l], *ff1, l)

        hn = _norm_cast(h, norm_mix[l], BF16)
        p1, p1s = _proj(hn, w1, l, w1.shape[1] // 2, n_sample=n_s)
        p2, p2s = _proj(hn, w2, l, w2.shape[1] // 2, n_sample=n_s)
        p3, p3s = _proj(hn, w3, l, w3.shape[1] // 2, n_sample=n_s)
        pg = _proj(hn, wg, l, wg.shape[1] // 4, BF16)

        prm = jnp.zeros((SUBLANES, LANES), F32)
        prm = prm.at[0, H_A:2 * H_A].set(a_log[l]).at[1, H_A:2 * H_A].set(dt_bias[l])
        gnorm = gdn_norm[l].reshape(1, DV_A)
        up_w = jnp.pad(gla_gk_up[l], ((0, LANES - GLA_RANK), (0, 0))).astype(BF16)
        gk_bias = gla_gk_bias[l].reshape(1, H_B * DK_B)
        lnorm = gla_norm[l].reshape(1, DV_B)
        mu = rwkv_mu[l].reshape(1, N_CC)
        wwa = jnp.zeros((LANES, 2 * BRANCH_W), F32)
        wwa = wwa.at[0:DECAY_LORA, 0:BRANCH_W].set(rwkv_w_up[l])
        wwa = wwa.at[DECAY_LORA:DECAY_LORA + AAA_LORA, BRANCH_W:].set(rwkv_a_up[l]).astype(BF16)
        gup = rwkv_g_up[l].astype(BF16)
        vec = jnp.stack([rwkv_w0[l], rwkv_a0[l], rwkv_k_k[l], rwkv_k_a[l], rwkv_r_k[l],
                         rwkv_ln_w[l], rwkv_ln_b[l], jnp.zeros((BRANCH_W,), F32)], axis=0)

        ya_p, gdn_p = _gdn(p1, p3, None, None, gdn_p, l, depth, conv_a[l], prm, gnorm, **prompt)
        ya_s, gdn_s = _gdn(p1s, p3s, state_gdn_conv, state_gdn, gdn_s, l, depth, conv_a[l], prm, gnorm, **sample)
        yb_p, gla_p = _gla(p2, p3, None, gla_p, l, depth, up_w, gk_bias, lnorm, **prompt)
        yb_s, gla_s = _gla(p2s, p3s, state_gla, gla_s, l, depth, up_w, gk_bias, lnorm, **sample)
        yc_p, rwkv_p = _rwkv(p3, None, None, rwkv_p, l, depth, mu, wwa, gup, vec, **prompt)
        yc_s, rwkv_s = _rwkv(p3s, state_rwkv_shift, state_rwkv, rwkv_s, l, depth, mu, wwa, gup, vec, **sample)

        h = _merge([y.reshape(n_p, BRANCH_W) for y in (ya_p, yb_p, yc_p)],
                   [y.reshape(n_s, BRANCH_W) for y in (ya_s, yb_s, yc_s)], pg, wbr, wout, h, l)
        x = _ffn(h, norm_ff2[l], *ff2, l)

        conv_p.append(_tail_rows(p1, bp, tp, CONV_W - 1, QKV_A))
        conv_s.append(p1[n_p:].reshape(bs, ts, -1)[:, ts - (CONV_W - 1):, :QKV_A])
        shift_p.append(_tail_rows(p3, bp, tp, 1, N_CC))
        shift_s.append(p3[n_p:].reshape(bs, ts, -1)[:, ts - 1:, :N_CC])

    y_p = _norm_cast(x, norm_final, F32, 0, n_p)
    y_s = _norm_cast(x, norm_final, F32, n_p, n_s)
    return (y_p.reshape(bp, tp, d), y_s.reshape(bs, ts, d),
            gdn_p, gdn_s, jnp.stack(conv_p), jnp.stack(conv_s), gla_p, gla_s,
            rwkv_p, rwkv_s, jnp.stack(shift_p), jnp.stack(shift_s))
```

```python
import functools
import math

import jax
import jax.numpy as jnp
from jax import lax
from jax.experimental import pallas as pl
from jax.experimental.pallas import tpu as pltpu

F32 = jnp.float32
BF16 = jnp.bfloat16

D_MODEL = 2048
N_BRANCH = 3
BRANCH_W = 768
DK_A, DV_A, H_A, CONV_W = 128, 128, 6, 4
DK_B, DV_B, H_B, GLA_RANK, GLA_NORMALIZER = 64, 128, 6, 16, 16.0
N_C, H_C, DECAY_LORA, AAA_LORA, GATE_LORA = 64, 12, 64, 64, 128
GN_EPS = 64e-5
D_FF = 5504
NORM_EPS = 1e-6
QKV_A = 2 * H_A * DK_A + H_A * DV_A
N_A = QKV_A + H_A * DV_A + 2 * H_A
N_B = 2 * H_B * DK_B + 2 * H_B * DV_B + GLA_RANK
N_CC = 3 * BRANCH_W + DECAY_LORA + AAA_LORA + GATE_LORA
N_PAIR = H_C // 2

LANES = 128
SUBLANES = 8
CHUNK = 64
PROMPT_GROUPS = 4
SAMPLE_GROUPS = 2
SAMPLE_ROWS = 8
FF_TILE = 512
VMEM_LIMIT = 60 * 1024 * 1024


def _cparams(n_axes):
    return pltpu.CompilerParams(dimension_semantics=("arbitrary",) * n_axes,
                                vmem_limit_bytes=VMEM_LIMIT)


def _pick_tile(n, prefs):
    for t in prefs:
        if n % t == 0:
            return t
    raise ValueError(f"no tile for {n}")


def _bdot(a, b):
    return jnp.dot(a.astype(BF16), b.astype(BF16), preferred_element_type=F32)


def _bdot_nt(a, b):
    return lax.dot_general(a.astype(BF16), b.astype(BF16), (((1,), (1,)), ((), ())),
                           preferred_element_type=F32)


def _bdot_tn(a, b):
    return lax.dot_general(a.astype(BF16), b.astype(BF16), (((0,), (0,)), ((), ())),
                           preferred_element_type=F32)


def _sigmoid(x):
    return jax.nn.sigmoid(x)


def _silu(x):
    return x * _sigmoid(x)


def _softplus(x):
    return jnp.maximum(x, 0.0) + jnp.log(1.0 + jnp.exp(-jnp.abs(x)))


def _rmsnorm_rows(x, g):
    return x * lax.rsqrt(jnp.mean(x * x, axis=-1, keepdims=True) + NORM_EPS) * g


def _chunk_masks(rows, seq_len):
    sh = int(math.log2(seq_len))
    ri = lax.broadcasted_iota(jnp.int32, (rows, rows), 0)
    ci = lax.broadcasted_iota(jnp.int32, (rows, rows), 1)
    same = (ri >> sh) == (ci >> sh)
    return same & (ri >= ci), same & (ri > ci), same


def _eye(rows):
    return (lax.broadcasted_iota(jnp.int32, (rows, rows), 0)
            == lax.broadcasted_iota(jnp.int32, (rows, rows), 1)).astype(F32)


def _inv_unit_lower(lms, eye, nil):
    xs = [eye - lm for lm in lms]
    ps = list(lms)
    k = 2
    while k < nil:
        ps = [_bdot(p, p) for p in ps]
        xs = [x + _bdot(x, p) for x, p in zip(xs, ps)]
        k *= 2
    return xs


def _row_valid(rows, seq_len, null_rows, width):
    r = lax.broadcasted_iota(jnp.int32, (rows, width), 0)
    return (r & (seq_len - 1)) >= null_rows


def _seq_rows(rows, seq_len, s, width):
    r = lax.broadcasted_iota(jnp.int32, (rows, width), 0) & (CHUNK - 1)
    return (r >> int(math.log2(seq_len))) == s


def _pick(sel, xv):
    return xv if sel is None else jnp.where(sel, xv, 0.0)


def _ffn_body(x_ref, g_ref, wg_ref, wu_ref, wd_ref, o_ref, xn_ref, *, d_ff):
    j = pl.program_id(1)

    @pl.when(j == 0)
    def _():
        xn_ref[...] = _rmsnorm_rows(x_ref[...], g_ref[...]).astype(BF16)
        o_ref[...] = jnp.zeros_like(o_ref)

    dup = j * FF_TILE - _ff_start(j, d_ff)
    col_ok = lax.broadcasted_iota(jnp.int32, (1, FF_TILE), 1) >= dup
    row_ok = lax.broadcasted_iota(jnp.int32, (FF_TILE, 1), 0) >= dup
    xn = xn_ref[...]
    gate = jnp.dot(xn, wg_ref[0], preferred_element_type=F32)
    up = jnp.dot(xn, wu_ref[0], preferred_element_type=F32)
    act = jnp.where(col_ok, _silu(gate) * up, 0.0).astype(BF16)
    wd = jnp.where(row_ok, wd_ref[0], jnp.zeros((), BF16))
    o_ref[...] += jnp.dot(act, wd, preferred_element_type=F32)

    @pl.when(j == pl.num_programs(1) - 1)
    def _():
        o_ref[...] = x_ref[...] + 0.5 * o_ref[...]


def _ff_start(j, d_ff):
    return jnp.minimum(j * FF_TILE, d_ff - FF_TILE)


def _ffn(x, g, w_up, w_down, layer):
    n, d = x.shape
    tm = _pick_tile(n, (1088, 544, 512, 256, 128, 64, 32, 16, 8))
    d_ff = w_down.shape[1]
    assert d_ff % LANES == 0 and d_ff >= FF_TILE
    nf = pl.cdiv(d_ff, FF_TILE)

    def up_spec(col0):
        return pl.BlockSpec((pl.Element(1), pl.Element(d), pl.Element(FF_TILE)),
                            lambda i, j: (layer, 0, pl.multiple_of(col0 + _ff_start(j, d_ff), LANES)))

    return pl.pallas_call(
        functools.partial(_ffn_body, d_ff=d_ff),
        grid=(n // tm, nf),
        in_specs=[
            pl.BlockSpec((tm, d), lambda i, j: (i, 0)),
            pl.BlockSpec((1, d), lambda i, j: (0, 0)),
            up_spec(0),
            up_spec(d_ff),
            pl.BlockSpec((pl.Element(1), pl.Element(FF_TILE), pl.Element(d)),
                         lambda i, j: (layer, pl.multiple_of(_ff_start(j, d_ff), LANES), 0)),
        ],
        out_specs=pl.BlockSpec((tm, d), lambda i, j: (i, 0)),
        out_shape=jax.ShapeDtypeStruct((n, d), F32),
        scratch_shapes=[pltpu.VMEM((tm, d), BF16)],
        compiler_params=_cparams(2),
        name="ffn",
    )(x, g.reshape(1, d), w_up, w_up, w_down)


def _norm_cast_body(x_ref, g_ref, o_ref):
    o_ref[...] = _rmsnorm_rows(x_ref[...], g_ref[...]).astype(o_ref.dtype)


def _norm_cast(x, g, dtype, start=0, count=None):
    n, d = x.shape
    count = n - start if count is None else count
    tm = _pick_tile(math.gcd(start, count), (544, 512, 256, 128, 64, 32, 16, 8))
    first = start // tm
    return pl.pallas_call(
        _norm_cast_body,
        grid=(count // tm,),
        in_specs=[pl.BlockSpec((tm, d), lambda i: (first + i, 0)), pl.BlockSpec((1, d), lambda i: (0, 0))],
        out_specs=pl.BlockSpec((tm, d), lambda i: (i, 0)),
        out_shape=jax.ShapeDtypeStruct((count, d), dtype),
        compiler_params=_cparams(1),
        name="norm",
    )(x, g.reshape(1, d))


def _pad_rows(x):
    n, w = x.shape
    x3 = x.reshape(n // SUBLANES, SUBLANES, w)
    tail = lax.broadcasted_iota(jnp.int32, x3.shape, 1) >= SUBLANES // 2
    first = jnp.where(tail, pltpu.roll(x3, SUBLANES // 2, 1), 0.0)
    second = jnp.where(tail, x3, 0.0)
    return jnp.stack([first, second], axis=1).reshape(2 * n, w)


def _compact_rows(x):
    n, w = x.shape
    x4 = x.reshape(n // (2 * SUBLANES), 2, SUBLANES, w)
    head = lax.broadcasted_iota(jnp.int32, (n // (2 * SUBLANES), SUBLANES, w), 1) < SUBLANES // 2
    return jnp.where(head, pltpu.roll(x4[:, 0], SUBLANES // 2, 1), x4[:, 1]).reshape(n // 2, w)


WPREP_ROWS = 256


def _wprep_body(*refs, depth, n_main, misc):
    if misc:
        w_ref, ma_ref, mb_ref, o_ref = refs
    else:
        w_ref, o_ref = refs
    i = pl.program_id(0)

    @pl.when(i < n_main)
    def _():
        wb = w_ref[...].astype(BF16)
        for l in range(depth):
            o_ref[l] = wb[:, l, :]

    if misc:
        na, nb = misc

        @pl.when(i == n_main)
        def _():
            rows = lax.broadcasted_iota(jnp.int32, ma_ref.shape[0:1] + ma_ref.shape[2:3], 0)
            for l in range(depth):
                o_ref[l] = jnp.zeros(o_ref.shape[1:], BF16)
                o_ref[l, 0:ma_ref.shape[0], :] = jnp.where(rows < na, ma_ref[:, l, :], 0.0).astype(BF16)
                o_ref[l, LANES:LANES + mb_ref.shape[0], :] = jnp.where(rows < nb, mb_ref[:, l, :], 0.0).astype(BF16)


def _wprep(wt, row0, nrows, misc=None):
    n_all, depth, k = wt.shape
    assert nrows % WPREP_ROWS == 0
    n_main = nrows // WPREP_ROWS

    def win(rows, start_of):
        return pl.BlockSpec((pl.Element(rows), pl.Element(depth), pl.Element(k)),
                            lambda i: (start_of(i), 0, 0))

    in_specs = [win(WPREP_ROWS, lambda i: row0 + jnp.minimum(i, n_main - 1) * WPREP_ROWS)]
    args = [wt]
    body_misc = None
    if misc:
        row_a, n_a, row_b, n_b = misc
        small = 2 * SUBLANES
        assert n_a <= small and n_b <= small
        in_specs += [win(small, lambda i: row_a), win(small, lambda i: row_b)]
        args += [wt, wt]
        body_misc = (n_a, n_b)
    nblk = n_main + (1 if misc else 0)
    return pl.pallas_call(
        functools.partial(_wprep_body, depth=depth, n_main=n_main, misc=body_misc),
        grid=(nblk,),
        in_specs=in_specs,
        out_specs=pl.BlockSpec((depth, WPREP_ROWS, k), lambda i: (0, i, 0)),
        out_shape=jax.ShapeDtypeStruct((depth, nblk * WPREP_ROWS, k), BF16),
        compiler_params=_cparams(1),
        name="wprep",
    )(*args)


def _proj_body(x_ref, w_ref, o_ref, *pad_ref, sample_off, n_sample):
    res = lax.dot_general(x_ref[...], w_ref[...], (((1,), (1,)), ((), ())), preferred_element_type=F32)
    o_ref[...] = res.astype(o_ref.dtype)
    if pad_ref:
        @pl.when(pl.program_id(1) == pl.num_programs(1) - 1)
        def _():
            pad_ref[0][...] = _pad_rows(res[sample_off:sample_off + n_sample])


def _proj(xn, w, layer, tn, out_dtype=F32, n_sample=0):
    n, d = xn.shape
    nn = w.shape[1]
    tm = _pick_tile(n, (1088, 544, 512, 256, 128, 64, 32, 16))
    out_specs = [pl.BlockSpec((tm, tn), lambda j, i: (i, j))]
    out_shape = [jax.ShapeDtypeStruct((n, nn), out_dtype)]
    sample_off = tm - n_sample
    if n_sample:
        assert 0 <= sample_off and sample_off % SUBLANES == 0 and n_sample % SUBLANES == 0
        out_specs.append(pl.BlockSpec((2 * n_sample, tn), lambda j, i: (0, j)))
        out_shape.append(jax.ShapeDtypeStruct((2 * n_sample, nn), F32))
    out = pl.pallas_call(
        functools.partial(_proj_body, sample_off=sample_off, n_sample=n_sample),
        grid=(nn // tn, n // tm),
        in_specs=[
            pl.BlockSpec((tm, d), lambda j, i: (i, 0)),
            pl.BlockSpec((None, tn, d), lambda j, i: (layer, j, 0)),
        ],
        out_specs=out_specs,
        out_shape=out_shape,
        compiler_params=_cparams(2),
        name="proj",
    )(xn, w)
    return out if n_sample else out[0]


MERGE_TN = 2048


def _merge_body(yap_ref, ybp_ref, ycp_ref, yas_ref, ybs_ref, ycs_ref, ga_ref, gb_ref, gc_ref,
                wb_ref, wo_ref, h_ref, o_ref, *, prompt_tiles):
    i = pl.program_id(0)
    j = pl.program_id(1)

    @pl.when(j == 0)
    def _():
        o_ref[...] = h_ref[...]

    is_prompt = i < prompt_tiles
    ya = jnp.where(is_prompt, yap_ref[...], yas_ref[...])
    yb = jnp.where(is_prompt, ybp_ref[...], ybs_ref[...])
    yc = jnp.where(is_prompt, ycp_ref[...], ycs_ref[...])
    m = _sigmoid(ga_ref[...].astype(F32)) * jnp.dot(ya, wb_ref[0], preferred_element_type=F32)
    m += _sigmoid(gb_ref[...].astype(F32)) * jnp.dot(yb, wb_ref[1], preferred_element_type=F32)
    m += _sigmoid(gc_ref[...].astype(F32)) * jnp.dot(yc, wb_ref[2], preferred_element_type=F32)
    o_ref[...] += jnp.dot(m.astype(BF16), wo_ref[...], preferred_element_type=F32)


def _merge(ys_prompt, ys_sample, pg, wb, wo, h, layer):
    n, d = h.shape
    n_p, n_s = ys_prompt[0].shape[0], ys_sample[0].shape[0]
    tm = _pick_tile(math.gcd(n_p, n_s), (256, 128, 64, 32, 16))
    prompt_tiles = n_p // tm
    nj = d // MERGE_TN
    yp_spec = pl.BlockSpec((tm, BRANCH_W), lambda i, j: (jnp.minimum(i, prompt_tiles - 1), 0))
    ys_spec = pl.BlockSpec((tm, BRANCH_W), lambda i, j: (jnp.maximum(i - prompt_tiles, 0), 0))
    return pl.pallas_call(
        functools.partial(_merge_body, prompt_tiles=prompt_tiles),
        grid=(n // tm, nj),
        in_specs=[
            yp_spec, yp_spec, yp_spec, ys_spec, ys_spec, ys_spec,
            pl.BlockSpec((tm, MERGE_TN), lambda i, j: (i, j)),
            pl.BlockSpec((tm, MERGE_TN), lambda i, j: (i, nj + j)),
            pl.BlockSpec((tm, MERGE_TN), lambda i, j: (i, 2 * nj + j)),
            pl.BlockSpec((None, N_BRANCH, BRANCH_W, MERGE_TN), lambda i, j: (layer, 0, 0, j),
                         pipeline_mode=pl.Buffered(1)),
            pl.BlockSpec((None, MERGE_TN, d), lambda i, j: (layer, j, 0), pipeline_mode=pl.Buffered(1)),
            pl.BlockSpec((tm, d), lambda i, j: (i, 0)),
        ],
        out_specs=pl.BlockSpec((tm, d), lambda i, j: (i, 0)),
        out_shape=jax.ShapeDtypeStruct((n, d), F32),
        compiler_params=_cparams(2),
        name="merge",
    )(*ys_prompt, *ys_sample, pg, pg, pg, wb, wo, h)


def _layer_state_spec(layer, nseq, tail):
    zeros = (0,) * len(tail)
    return pl.BlockSpec((None, nseq) + tail, lambda b, c: (layer, b) + zeros)


def _stack_io(stack, depth, nstate, tail, n_args):
    shape = jax.ShapeDtypeStruct((depth, nstate) + tail, F32)
    if stack is None:
        return [], [], shape, {}
    return [pl.BlockSpec(memory_space=pl.ANY)], [stack], shape, {n_args: 1}


def _split3_dot(lmat, x):
    lb = lmat.astype(BF16)
    hi = x.astype(BF16)
    r1 = x - hi.astype(F32)
    mid = r1.astype(BF16)
    lo = (r1 - mid.astype(F32)).astype(BF16)
    return (jnp.dot(lb, hi, preferred_element_type=F32) + jnp.dot(lb, mid, preferred_element_type=F32)
            + jnp.dot(lb, lo, preferred_element_type=F32))


def _gdn_body(*refs, ngroup, nseq, null_rows, has_state, has_stack):
    refs = list(refs)
    xbuf = refs.pop()
    so_ref = refs.pop()
    y_ref = refs.pop()
    if has_stack:
        refs.pop()
    row_refs = [refs[3 * g:3 * g + 3] for g in range(ngroup)]
    refs = refs[3 * ngroup:]
    if has_state:
        cs_ref, cw_ref, prm_ref, nw_ref, s0_ref = refs
    else:
        cw_ref, prm_ref, nw_ref = refs
        s0_ref = so_ref
    rows = CHUNK
    seq_len = rows // nseq
    c = pl.program_id(1)
    chains = [(g, h) for g in range(ngroup) for h in range(H_A)]

    @pl.when(c == 0)
    def _():
        for g in range(ngroup):
            xbuf[g, 0:SUBLANES, :] = jnp.zeros((SUBLANES, QKV_A), F32)
        if not has_state:
            so_ref[...] = jnp.zeros_like(so_ref)

    @pl.when(c > 0)
    def _():
        for g in range(ngroup):
            xbuf[g, 0:SUBLANES, :] = xbuf[g, rows:rows + SUBLANES, :]

    for g in range(ngroup):
        xbuf[g, SUBLANES:SUBLANES + rows, :] = row_refs[g][0][...]
        if has_state:
            for s in range(nseq):
                r0 = SUBLANES + s * seq_len + null_rows - (CONV_W - 1)
                xbuf[g, r0:r0 + CONV_W - 1, :] = cs_ref[g * nseq + s]

    incl, strict, same = _chunk_masks(rows, seq_len)
    eye = _eye(rows)
    valid = _row_valid(rows, seq_len, null_rows, LANES) if null_rows else None
    sels = [_seq_rows(rows, seq_len, s, LANES) for s in range(nseq)] if nseq > 1 else [None]
    sels2 = [_seq_rows(2 * rows, seq_len, s, LANES) for s in range(nseq)] if nseq > 1 else [None]
    lstack = jnp.concatenate([incl.astype(F32), same.astype(F32)], axis=0)

    def conv_cols(g, c0):
        acc = xbuf[g, SUBLANES - 3:SUBLANES - 3 + rows, c0:c0 + LANES] * cw_ref[0:1, c0:c0 + LANES]
        for j in range(1, CONV_W):
            acc += (xbuf[g, SUBLANES - 3 + j:SUBLANES - 3 + j + rows, c0:c0 + LANES]
                    * cw_ref[j:j + 1, c0:c0 + LANES])
        return _silu(acc)

    qs, ks, kbs, rhs, decays, wq_parts, kds, gtcs = [], [], [], [], [], [], [], []
    for g in range(ngroup):
        ba = row_refs[g][2][...]
        beta_all = _sigmoid(ba)
        g_all = -jnp.exp(prm_ref[0:1, :]) * _softplus(ba + prm_ref[1:2, :])
        if null_rows:
            g_all = jnp.where(valid, g_all, 0.0)
        cs = _split3_dot(lstack, g_all)
        gc_all, gt_all = cs[0:rows], cs[rows:2 * rows]
        gc_t = gc_all.T
        for h in range(H_A):
            q = conv_cols(g, h * DK_A)
            k = conv_cols(g, H_A * DK_A + h * DK_A)
            v = conv_cols(g, 2 * H_A * DK_A + h * DV_A)
            q = q * lax.rsqrt(jnp.sum(q * q, axis=-1, keepdims=True) + 1e-6) * DK_A ** -0.5
            k = k * lax.rsqrt(jnp.sum(k * k, axis=-1, keepdims=True) + 1e-6)
            if null_rows:
                q = jnp.where(valid, q, 0.0)
                k = jnp.where(valid, k, 0.0)
                v = jnp.where(valid, v, 0.0)
            beta = beta_all[:, h:h + 1]
            gcc = gc_all[:, H_A + h:H_A + h + 1]
            gtc = gt_all[:, H_A + h:H_A + h + 1]
            gcr = gc_t[H_A + h:H_A + h + 1, :]
            egc = jnp.exp(gcc)
            kb = k * beta
            qs.append(q)
            ks.append(k)
            kbs.append(kb)
            rhs.append(jnp.concatenate([v * beta, kb * egc], axis=1))
            decays.append(jnp.where(incl, jnp.exp(jnp.where(incl, gcc - gcr, 0.0)), 0.0))
            wq_parts.append(q * egc)
            kds.append(k * jnp.exp(gtc - gcc))
            gtcs.append(gtc)

    n = range(len(chains))
    lms = [jnp.where(strict, _bdot_nt(kbs[i], ks[i]) * decays[i], 0.0) for i in n]
    attns = [_bdot_nt(qs[i], ks[i]) * decays[i] for i in n]
    tmats = _inv_unit_lower(lms, eye, seq_len)
    uws = [_bdot(tmats[i], rhs[i]) for i in n]
    us, os_ = [], []
    for i, (g, h) in enumerate(chains):
        u = uws[i][:, 0:DV_A]
        wq = jnp.concatenate([uws[i][:, DV_A:DV_A + DK_A], wq_parts[i]], axis=0)
        o = None
        for s in range(nseq):
            both = _bdot(_pick(sels2[s], wq), s0_ref[g * nseq + s, h])
            u = u - both[0:rows]
            o = both[rows:2 * rows] if o is None else o + both[rows:2 * rows]
        us.append(u)
        os_.append(o)
    os_ = [os_[i] + _bdot(attns[i], us[i]) for i in n]
    for i, (g, h) in enumerate(chains):
        for s in range(nseq):
            g_last = jnp.exp(gtcs[i][s * seq_len:s * seq_len + 1, :])
            so_ref[g * nseq + s, h] = (g_last * s0_ref[g * nseq + s, h]
                                       + _bdot_tn(_pick(sels[s], kds[i]), us[i]))
    for i, (g, h) in enumerate(chains):
        z = row_refs[g][1][:, h * DV_A:(h + 1) * DV_A]
        _store_y(y_ref, g, h * DV_A, _rmsnorm_rows(os_[i], nw_ref[...]) * _silu(z), null_rows)


def _row_spec(width, col, g, ngroup, nchunk):
    return pl.BlockSpec((CHUNK, width), lambda b, c: ((b * ngroup + g) * nchunk + c, col))


def _y_io(nbatch, nchunk, ngroup, null_rows):
    out_rows = CHUNK // 2 if null_rows else CHUNK
    return (pl.BlockSpec((ngroup, out_rows, BRANCH_W), lambda b, c: (b, c, 0)),
            jax.ShapeDtypeStruct((nbatch, nchunk * out_rows, BRANCH_W), BF16))


def _store_y(y_ref, g, c0, yv, null_rows):
    if null_rows:
        yv = _compact_rows(yv)
    y_ref[g, :, c0:c0 + yv.shape[1]] = yv.astype(y_ref.dtype)


def _gdn(p1, p3, conv_state, state, stack, layer, depth, conv_w, prm, norm_w, *,
         nbatch, nchunk, nseq, null_rows, ngroup):
    has_state = state is not None
    tail = (H_A, DK_A, DV_A)
    in_specs, args = [], []
    for g in range(ngroup):
        in_specs += [_row_spec(QKV_A, 0, g, ngroup, nchunk),
                     _row_spec(BRANCH_W, QKV_A // BRANCH_W, g, ngroup, nchunk),
                     _row_spec(LANES, N_CC // LANES, g, ngroup, nchunk)]
        args += [p1, p1, p3]
    if has_state:
        in_specs.append(_layer_state_spec(layer, ngroup * nseq, (CONV_W - 1, QKV_A)))
        args.append(conv_state)
    in_specs += [
        pl.BlockSpec((CONV_W, QKV_A), lambda b, c: (0, 0)),
        pl.BlockSpec((SUBLANES, LANES), lambda b, c: (0, 0)),
        pl.BlockSpec((1, DV_A), lambda b, c: (0, 0)),
    ]
    args += [conv_w, prm, norm_w]
    if has_state:
        in_specs.append(_layer_state_spec(layer, ngroup * nseq, tail))
        args.append(state)
    stack_specs, stack_args, stack_shape, aliases = _stack_io(stack, depth, nbatch * nseq, tail, len(args))
    y_spec, y_shape = _y_io(nbatch, nchunk, ngroup, null_rows)
    return pl.pallas_call(
        functools.partial(_gdn_body, ngroup=ngroup, nseq=nseq, null_rows=null_rows, has_state=has_state,
                          has_stack=stack is not None),
        grid=(nbatch // ngroup, nchunk),
        in_specs=in_specs + stack_specs,
        out_specs=[y_spec, _layer_state_spec(layer, ngroup * nseq, tail)],
        out_shape=[y_shape, stack_shape],
        input_output_aliases=aliases,
        scratch_shapes=[pltpu.VMEM((ngroup, CHUNK + SUBLANES, QKV_A), F32)],
        compiler_params=_cparams(2),
        name="gdn",
    )(*args, *stack_args)


def _gla_body(*refs, ngroup, nseq, null_rows, has_state, has_stack):
    refs = list(refs)
    so_ref = refs.pop()
    y_ref = refs.pop()
    if has_stack:
        refs.pop()
    row_refs = [refs[5 * g:5 * g + 5] for g in range(ngroup)]
    refs = refs[5 * ngroup:]
    if has_state:
        up_ref, bias_ref, nw_ref, s0_ref = refs
    else:
        up_ref, bias_ref, nw_ref = refs
        s0_ref = so_ref
    rows = CHUNK
    seq_len = rows // nseq
    c = pl.program_id(1)
    chains = [(g, h) for g in range(ngroup) for h in range(H_B)]
    if not has_state:
        @pl.when(c == 0)
        def _():
            so_ref[...] = jnp.zeros_like(so_ref)

    incl, _, same = _chunk_masks(rows, seq_len)
    ci = lax.broadcasted_iota(jnp.int32, (rows, rows), 1)
    first_half = same & ((ci & (seq_len - 1)) < seq_len // 2)
    wide = H_B * DK_B
    lstack = jnp.concatenate([incl.astype(F32), same.astype(F32), first_half.astype(F32)], axis=0)
    valid_w = _row_valid(rows, seq_len, null_rows, wide) if null_rows else None
    valid_k = _row_valid(rows, seq_len, null_rows, DK_B) if null_rows else None
    valid_v = _row_valid(rows, seq_len, null_rows, DV_B) if null_rows else None
    sels = [_seq_rows(rows, seq_len, s, DK_B) for s in range(nseq)] if nseq > 1 else [None]

    qis, kis, vs, q_ins, k_decs, a_lasts = [], [], [], [], [], []
    for g in range(ngroup):
        q_ref, k_ref, v_ref, _, gkl_ref = row_refs[g]
        x = _bdot(gkl_ref[...], up_ref[...]) + bias_ref[...]
        gk_all = (jnp.minimum(x, 0.0) - jnp.log(1.0 + jnp.exp(-jnp.abs(x)))) / GLA_NORMALIZER
        if null_rows:
            gk_all = jnp.where(valid_w, gk_all, 0.0)
        cs = _split3_dot(lstack, gk_all)
        bc_all, bt_all, an_all = cs[0:rows], cs[rows:2 * rows], cs[2 * rows:3 * rows]
        bt_t = bt_all.T
        for h in range(H_B):
            c0 = h * DK_B
            q = q_ref[:, c0:c0 + DK_B] * DK_B ** -0.5
            k = k_ref[:, c0:c0 + DK_B]
            v = v_ref[:, h * DV_B:(h + 1) * DV_B]
            if null_rows:
                q = jnp.where(valid_k, q, 0.0)
                k = jnp.where(valid_k, k, 0.0)
                v = jnp.where(valid_v, v, 0.0)
            bc = bc_all[:, c0:c0 + DK_B]
            bt = bt_all[:, c0:c0 + DK_B]
            an = an_all[:, c0:c0 + DK_B]
            qis.append(q * jnp.exp(bc - an))
            kis.append(k * jnp.exp(an - bc))
            vs.append(v)
            q_ins.append(q * jnp.exp(bc))
            k_decs.append(k * jnp.exp(bt - bc))
            a_lasts.append([jnp.exp(bt_t[c0:c0 + DK_B, s * seq_len:s * seq_len + 1])
                            for s in range(nseq)])

    n = range(len(chains))
    attns = [jnp.where(incl, _bdot_nt(qis[i], kis[i]), 0.0) for i in n]
    os_ = [_bdot(attns[i], vs[i]) for i in n]
    for i, (g, h) in enumerate(chains):
        for s in range(nseq):
            os_[i] = os_[i] + _bdot(_pick(sels[s], q_ins[i]), s0_ref[g * nseq + s, h])
    for i, (g, h) in enumerate(chains):
        for s in range(nseq):
            so_ref[g * nseq + s, h] = (a_lasts[i][s] * s0_ref[g * nseq + s, h]
                                       + _bdot_tn(_pick(sels[s], k_decs[i]), vs[i]))
    for i, (g, h) in enumerate(chains):
        gate = row_refs[g][3][:, h * DV_B:(h + 1) * DV_B]
        _store_y(y_ref, g, h * DV_B, _rmsnorm_rows(os_[i], nw_ref[...]) * _silu(gate), null_rows)


def _gla(p2, p3, state, stack, layer, depth, up_w, bias, norm_w, *, nbatch, nchunk, nseq, null_rows, ngroup):
    has_state = state is not None
    wide = H_B * DK_B
    tail = (H_B, DK_B, DV_B)
    in_specs, args = [], []
    for g in range(ngroup):
        in_specs += [_row_spec(wide, 0, g, ngroup, nchunk), _row_spec(wide, 1, g, ngroup, nchunk),
                     _row_spec(BRANCH_W, 1, g, ngroup, nchunk), _row_spec(BRANCH_W, 2, g, ngroup, nchunk)]
        in_specs.append(_row_spec(LANES, N_CC // LANES + 1, g, ngroup, nchunk))
        args += [p2, p2, p2, p2, p3]
    in_specs += [pl.BlockSpec((LANES, wide), lambda b, c: (0, 0)),
                 pl.BlockSpec((1, wide), lambda b, c: (0, 0)),
                 pl.BlockSpec((1, DV_B), lambda b, c: (0, 0))]
    args += [up_w, bias, norm_w]
    if has_state:
        in_specs.append(_layer_state_spec(layer, ngroup * nseq, tail))
        args.append(state)
    stack_specs, stack_args, stack_shape, aliases = _stack_io(stack, depth, nbatch * nseq, tail, len(args))
    y_spec, y_shape = _y_io(nbatch, nchunk, ngroup, null_rows)
    return pl.pallas_call(
        functools.partial(_gla_body, ngroup=ngroup, nseq=nseq, null_rows=null_rows, has_state=has_state,
                          has_stack=stack is not None),
        grid=(nbatch // ngroup, nchunk),
        in_specs=in_specs + stack_specs,
        out_specs=[y_spec, _layer_state_spec(layer, ngroup * nseq, tail)],
        out_shape=[y_shape, stack_shape],
        input_output_aliases=aliases,
        compiler_params=_cparams(2),
        name="gla",
    )(*args, *stack_args)


def _rwkv_body(*refs, ngroup, nseq, null_rows, has_state, has_stack):
    refs = list(refs)
    sp_ref = refs.pop()
    pbuf = refs.pop()
    so_ref = refs.pop()
    y_ref = refs.pop()
    if has_stack:
        refs.pop()
    pc_refs = refs[:ngroup]
    refs = refs[ngroup:]
    if has_state:
        sh_ref, mu_ref, wwa_ref, gup_ref, vec_ref, s0_ref = refs
    else:
        mu_ref, wwa_ref, gup_ref, vec_ref = refs
    rows = CHUNK
    seq_len = rows // nseq
    c = pl.program_id(1)
    chains = [(g, p) for g in range(ngroup) for p in range(N_PAIR)]
    pairs = range(len(chains))

    @pl.when(c == 0)
    def _():
        for g in range(ngroup):
            pbuf[g, 0:SUBLANES, :] = jnp.zeros((SUBLANES, N_CC), F32)
        if not has_state:
            sp_ref[...] = jnp.zeros_like(sp_ref)

    @pl.when(c > 0)
    def _():
        for g in range(ngroup):
            pbuf[g, 0:SUBLANES, :] = pbuf[g, rows:rows + SUBLANES, :]

    for g in range(ngroup):
        pbuf[g, SUBLANES:SUBLANES + rows, :] = pc_refs[g][...]
        if has_state:
            for s in range(nseq):
                r0 = SUBLANES + s * seq_len + null_rows - 1
                pbuf[g, r0:r0 + 1, :] = sh_ref[g * nseq + s]

    def xc_cols(g, c0, w):
        cur = pbuf[g, SUBLANES:SUBLANES + rows, c0:c0 + w]
        prev = pbuf[g, SUBLANES - 1:SUBLANES - 1 + rows, c0:c0 + w]
        return cur + (prev - cur) * mu_ref[:, c0:c0 + w]

    incl, strict, same = _chunk_masks(rows, seq_len)
    eye = _eye(rows)
    lane = lax.broadcasted_iota(jnp.int32, (rows, LANES), 1)
    lo = lane < N_C
    lo2 = lax.broadcasted_iota(jnp.int32, (2 * rows, LANES), 1) < N_C
    valid = _row_valid(rows, seq_len, null_rows, LANES) if null_rows else None
    valid_w = _row_valid(rows, seq_len, null_rows, BRANCH_W) if null_rows else None
    sels2 = [_seq_rows(2 * rows, seq_len, s, LANES) for s in range(nseq)] if nseq > 1 else [None]
    lstack = jnp.concatenate([incl.astype(F32), same.astype(F32)], axis=0)
    w0 = vec_ref[0:1, :]
    a0 = vec_ref[1:2, :]

    r2 = lax.broadcasted_iota(jnp.int32, (2 * N_C, LANES), 0)
    c2 = lax.broadcasted_iota(jnp.int32, (2 * N_C, LANES), 1)
    blockdiag = (r2 >= N_C) == (c2 >= N_C)

    def seg_sum(xv):
        s_lo = jnp.sum(jnp.where(lo, xv, 0.0), axis=-1, keepdims=True)
        s_hi = jnp.sum(jnp.where(lo, 0.0, xv), axis=-1, keepdims=True)
        return jnp.where(lo, s_lo, s_hi)

    def by_head(stacked):
        return jnp.where(lo, stacked[0:rows], stacked[rows:2 * rows])

    rs, k2s, vs, ar_stacks, b_ts, k_ts, tots, gates = [], [], [], [], [], [], [], []
    for g in range(ngroup):
        lora = xc_cols(g, 3 * BRANCH_W, DECAY_LORA + AAA_LORA + GATE_LORA)
        wa_in = lora[:, 0:LANES]
        wa_in = jnp.where(lo, jnp.tanh(wa_in), wa_in)
        wa = _bdot(wa_in, wwa_ref[...])
        g_all = _bdot(_sigmoid(lora[:, LANES:2 * LANES]), gup_ref[...])
        w_log = -_softplus(-(w0 + wa[:, 0:BRANCH_W])) - 0.5
        lw_all = -jnp.exp(w_log)
        a_all = _sigmoid(a0 + wa[:, BRANCH_W:2 * BRANCH_W])
        if null_rows:
            lw_all = jnp.where(valid_w, lw_all, 0.0)
        cs = _split3_dot(lstack, lw_all)
        cum_all, tot_all = cs[0:rows], cs[rows:2 * rows]
        for p in range(N_PAIR):
            c0 = p * LANES
            r = xc_cols(g, c0, LANES)
            k = xc_cols(g, BRANCH_W + c0, LANES)
            v = xc_cols(g, 2 * BRANCH_W + c0, LANES)
            a_p = a_all[:, c0:c0 + LANES]
            kkr = k * vec_ref[2:3, c0:c0 + LANES]
            kk = kkr * lax.rsqrt(seg_sum(kkr * kkr) + 1e-6)
            k2 = k * (1.0 + (a_p - 1.0) * vec_ref[3:4, c0:c0 + LANES])
            av = -kk
            bv = kk * a_p
            if null_rows:
                r = jnp.where(valid, r, 0.0)
                k2 = jnp.where(valid, k2, 0.0)
                v = jnp.where(valid, v, 0.0)
                av = jnp.where(valid, av, 0.0)
                bv = jnp.where(valid, bv, 0.0)
            cum = cum_all[:, c0:c0 + LANES]
            lw = lw_all[:, c0:c0 + LANES]
            g_inv = jnp.exp(-cum)
            rs.append(r)
            k2s.append(k2)
            vs.append(v)
            ar_stacks.append(jnp.concatenate([av * jnp.exp(cum - lw), r * jnp.exp(cum)], axis=0))
            b_ts.append(bv * g_inv)
            k_ts.append(k2 * g_inv)
            tots.append(tot_all[:, c0:c0 + LANES])
            gates.append(g_all[:, c0:c0 + LANES])

    a_abs, a_rbs, a_aks, a_rks = [], [], [], []
    for p in pairs:
        for hh in range(2):
            lhs = jnp.where(lo2 if hh == 0 else jnp.logical_not(lo2), ar_stacks[p], 0.0)
            mb = _bdot_nt(lhs, b_ts[p])
            mk = _bdot_nt(lhs, k_ts[p])
            a_abs.append(jnp.where(strict, mb[0:rows], 0.0))
            a_rbs.append(jnp.where(incl, mb[rows:2 * rows], 0.0))
            a_aks.append(jnp.where(strict, mk[0:rows], 0.0))
            a_rks.append(jnp.where(incl, mk[rows:2 * rows], 0.0))
    tmats = _inv_unit_lower([-m for m in a_abs], eye, seq_len)

    def pair_stack(mats, p):
        return jnp.concatenate([mats[2 * p], mats[2 * p + 1]], axis=0)

    states, xss = [], []
    for i, (g, p) in enumerate(chains):
        sps, xs = [], None
        for s in range(nseq):
            if has_state:
                sq = g * nseq + s
                sv = jnp.concatenate([s0_ref[sq, 2 * p], s0_ref[sq, 2 * p + 1]], axis=0)
                sp = jnp.where(blockdiag, jnp.concatenate([sv, sv], axis=1), 0.0)
            else:
                sp = sp_ref[i]
            both = _bdot_nt(_pick(sels2[s], ar_stacks[i]), sp)
            xs = both if xs is None else xs + both
            sps.append(sp)
        states.append(sps)
        xss.append(xs)
    yvs = [xss[i][0:rows] + by_head(_bdot(pair_stack(a_aks, i), vs[i])) for i in pairs]
    us = [by_head(_bdot(pair_stack(tmats, i), yvs[i])) for i in pairs]
    os_ = [xss[i][rows:2 * rows]
           + by_head(_bdot(pair_stack(a_rbs, i), us[i]) + _bdot(pair_stack(a_rks, i), vs[i]))
           for i in pairs]
    for i, (g, p) in enumerate(chains):
        uv = jnp.concatenate([us[i], vs[i]], axis=0)
        bk = jnp.concatenate([b_ts[i], k_ts[i]], axis=0)
        for s in range(nseq):
            upd = jnp.where(blockdiag, _bdot_tn(_pick(sels2[s], uv), bk), 0.0)
            sp_new = (states[i][s] + upd) * jnp.exp(tots[i][s * seq_len:s * seq_len + 1, :])
            if has_state:
                sq = g * nseq + s
                so_ref[sq, 2 * p] = sp_new[0:N_C, 0:N_C]
                so_ref[sq, 2 * p + 1] = pltpu.roll(sp_new[N_C:2 * N_C, :], N_C, 1)[:, 0:N_C]
            else:
                sp_ref[i] = sp_new

    for i, (g, p) in enumerate(chains):
        c0 = p * LANES
        o = os_[i]
        mean = seg_sum(o) * (1.0 / N_C)
        d = o - mean
        var = seg_sum(d * d) * (1.0 / N_C)
        on = d * lax.rsqrt(var + GN_EPS) * vec_ref[5:6, c0:c0 + LANES] + vec_ref[6:7, c0:c0 + LANES]
        bonus = seg_sum(rs[i] * k2s[i] * vec_ref[4:5, c0:c0 + LANES]) * vs[i]
        _store_y(y_ref, g, c0, (on + bonus) * gates[i], null_rows)

    if not has_state:
        @pl.when(c == pl.num_programs(1) - 1)
        def _():
            for i, (g, p) in enumerate(chains):
                sp = sp_ref[i]
                so_ref[g, 2 * p] = sp[0:N_C, 0:N_C]
                so_ref[g, 2 * p + 1] = pltpu.roll(sp[N_C:2 * N_C, :], N_C, 1)[:, 0:N_C]


def _rwkv(p3, shift_state, state, stack, layer, depth, mu, wwa, gup, vec, *,
          nbatch, nchunk, nseq, null_rows, ngroup):
    has_state = state is not None
    tail = (H_C, N_C, N_C)
    in_specs = [_row_spec(N_CC, 0, g, ngroup, nchunk) for g in range(ngroup)]
    args = [p3] * ngroup
    if has_state:
        in_specs.append(_layer_state_spec(layer, ngroup * nseq, (1, N_CC)))
        args.append(shift_state)
    in_specs += [
        pl.BlockSpec((1, N_CC), lambda b, c: (0, 0)),
        pl.BlockSpec((LANES, 2 * BRANCH_W), lambda b, c: (0, 0)),
        pl.BlockSpec((GATE_LORA, BRANCH_W), lambda b, c: (0, 0)),
        pl.BlockSpec((SUBLANES, BRANCH_W), lambda b, c: (0, 0)),
    ]
    args += [mu, wwa, gup, vec]
    if has_state:
        in_specs.append(_layer_state_spec(layer, ngroup * nseq, tail))
        args.append(state)
    stack_specs, stack_args, stack_shape, aliases = _stack_io(stack, depth, nbatch * nseq, tail, len(args))
    y_spec, y_shape = _y_io(nbatch, nchunk, ngroup, null_rows)
    return pl.pallas_call(
        functools.partial(_rwkv_body, ngroup=ngroup, nseq=nseq, null_rows=null_rows, has_state=has_state,
                          has_stack=stack is not None),
        grid=(nbatch // ngroup, nchunk),
        in_specs=in_specs + stack_specs,
        out_specs=[y_spec, _layer_state_spec(layer, ngroup * nseq, tail)],
        out_shape=[y_shape, stack_shape],
        input_output_aliases=aliases,
        scratch_shapes=[pltpu.VMEM((ngroup, CHUNK + SUBLANES, N_CC), F32),
                        pltpu.VMEM((ngroup * N_PAIR, 2 * N_C, 2 * N_C), F32)],
        compiler_params=_cparams(2),
        name="rwkv",
    )(*args, *stack_args)


def _prep_in(w_in):
    wt = jnp.transpose(w_in, (2, 0, 1))
    ab = QKV_A + H_A * DV_A
    w1 = _wprep(wt, 0, ab)
    w2 = _wprep(wt, N_A, N_B - GLA_RANK)
    w3 = _wprep(wt, N_A + N_B, N_CC, misc=(ab, 2 * H_A, N_A + N_B - GLA_RANK, GLA_RANK))
    wg = _wprep(wt, N_A + N_B + N_CC, N_BRANCH * D_MODEL)
    return w1, w2, w3, wg


def _prep_ffn(w_up, w_down):
    return w_up.astype(BF16), w_down.astype(BF16)


def _tail_rows(p, nb, t, n_tail, width):
    return jnp.stack([p[(b + 1) * t - n_tail:(b + 1) * t, :width] for b in range(nb)])


def kernel(x_prompt, x_sample, state_gdn, state_gdn_conv, state_gla, state_rwkv, state_rwkv_shift, w_in, conv_a, a_log, dt_bias, gdn_norm, gla_gk_up, gla_gk_bias, gla_norm, rwkv_mu, rwkv_w0, rwkv_w_up, rwkv_a0, rwkv_a_up, rwkv_g_up, rwkv_k_k, rwkv_k_a, rwkv_r_k, rwkv_ln_w, rwkv_ln_b, w_branch, w_out, norm_ff1, w_ff1_up, w_ff1_down, norm_mix, norm_ff2, w_ff2_up, w_ff2_down, norm_final):
    bp, tp, d = x_prompt.shape
    bs, ts, _ = x_sample.shape
    depth = w_in.shape[0]
    n_p, n_s = bp * tp, bs * ts
    null_rows = SAMPLE_ROWS - ts
    assert tp % CHUNK == 0 and null_rows == SAMPLE_ROWS // 2 and CONV_W - 1 <= null_rows
    seq_per_chunk = CHUNK // SAMPLE_ROWS
    assert bs % seq_per_chunk == 0
    nchunk_p = tp // CHUNK
    nstep_s = bs // seq_per_chunk
    prompt = dict(nbatch=bp, nchunk=nchunk_p, nseq=1, null_rows=0, ngroup=math.gcd(bp, PROMPT_GROUPS))
    sample = dict(nbatch=nstep_s, nchunk=1, nseq=seq_per_chunk, null_rows=null_rows,
                  ngroup=math.gcd(nstep_s, SAMPLE_GROUPS))

    x = jnp.concatenate([x_prompt.reshape(n_p, d), x_sample.reshape(n_s, d)], axis=0)
    gdn_p = gdn_s = gla_p = gla_s = rwkv_p = rwkv_s = None
    conv_p, conv_s, shift_p, shift_s = [], [], [], []
    ff1 = _prep_ffn(w_ff1_up, w_ff1_down)
    ff2 = _prep_ffn(w_ff2_up, w_ff2_down)
    w1, w2, w3, wg = _prep_in(w_in)
    wbr = w_branch.astype(BF16)
    wout = w_out.astype(BF16)
    for l in range(depth):
        h = _ffn(x, norm_ff1[l], *ff1, l)

        hn = _norm_cast(h, norm_mix[l], BF16)
        p1, p1s = _proj(hn, w1, l, w1.shape[1] // 2, n_sample=n_s)
        p2, p2s = _proj(hn, w2, l, w2.shape[1] // 2, n_sample=n_s)
        p3, p3s = _proj(hn, w3, l, w3.shape[1] // 2, n_sample=n_s)
        pg = _proj(hn, wg, l, wg.shape[1] // 4, BF16)

        prm = jnp.zeros((SUBLANES, LANES), F32)
        prm = prm.at[0, H_A:2 * H_A].set(a_log[l]).at[1, H_A:2 * H_A].set(dt_bias[l])
        gnorm = gdn_norm[l].reshape(1, DV_A)
        up_w = jnp.pad(gla_gk_up[l], ((0, LANES - GLA_RANK), (0, 0))).astype(BF16)
        gk_bias = gla_gk_bias[l].reshape(1, H_B * DK_B)
        lnorm = gla_norm[l].reshape(1, DV_B)
        mu = rwkv_mu[l].reshape(1, N_CC)
        wwa = jnp.zeros((LANES, 2 * BRANCH_W), F32)
        wwa = wwa.at[0:DECAY_LORA, 0:BRANCH_W].set(rwkv_w_up[l])
        wwa = wwa.at[DECAY_LORA:DECAY_LORA + AAA_LORA, BRANCH_W:].set(rwkv_a_up[l]).astype(BF16)
        gup = rwkv_g_up[l].astype(BF16)
        vec = jnp.stack([rwkv_w0[l], rwkv_a0[l], rwkv_k_k[l], rwkv_k_a[l], rwkv_r_k[l],
                         rwkv_ln_w[l], rwkv_ln_b[l], jnp.zeros((BRANCH_W,), F32)], axis=0)

        ya_p, gdn_p = _gdn(p1, p3, None, None, gdn_p, l, depth, conv_a[l], prm, gnorm, **prompt)
        ya_s, gdn_s = _gdn(p1s, p3s, state_gdn_conv, state_gdn, gdn_s, l, depth, conv_a[l], prm, gnorm, **sample)
        yb_p, gla_p = _gla(p2, p3, None, gla_p, l, depth, up_w, gk_bias, lnorm, **prompt)
        yb_s, gla_s = _gla(p2s, p3s, state_gla, gla_s, l, depth, up_w, gk_bias, lnorm, **sample)
        yc_p, rwkv_p = _rwkv(p3, None, None, rwkv_p, l, depth, mu, wwa, gup, vec, **prompt)
        yc_s, rwkv_s = _rwkv(p3s, state_rwkv_shift, state_rwkv, rwkv_s, l, depth, mu, wwa, gup, vec, **sample)

        h = _merge([y.reshape(n_p, BRANCH_W) for y in (ya_p, yb_p, yc_p)],
                   [y.reshape(n_s, BRANCH_W) for y in (ya_s, yb_s, yc_s)], pg, wbr, wout, h, l)
        x = _ffn(h, norm_ff2[l], *ff2, l)

        conv_p.append(_tail_rows(p1, bp, tp, CONV_W - 1, QKV_A))
        conv_s.append(p1[n_p:].reshape(bs, ts, -1)[:, ts - (CONV_W - 1):, :QKV_A])
        shift_p.append(_tail_rows(p3, bp, tp, 1, N_CC))
        shift_s.append(p3[n_p:].reshape(bs, ts, -1)[:, ts - 1:, :N_CC])

    y_p = _norm_cast(x, norm_final, F32, 0, n_p)
    y_s = _norm_cast(x, norm_final, F32, n_p, n_s)
    return (y_p.reshape(bp, tp, d), y_s.reshape(bs, ts, d),
            gdn_p, gdn_s, jnp.stack(conv_p), jnp.stack(conv_s), gla_p, gla_s,
            rwkv_p, rwkv_s, jnp.stack(shift_p), jnp.stack(shift_s))
```

```python
import functools
import math

import jax
import jax.numpy as jnp
from jax import lax
from jax.experimental import pallas as pl
from jax.experimental.pallas import tpu as pltpu

F32 = jnp.float32
BF16 = jnp.bfloat16

D_MODEL = 2048
N_BRANCH = 3
BRANCH_W = 768
DK_A, DV_A, H_A, CONV_W = 128, 128, 6, 4
DK_B, DV_B, H_B, GLA_RANK, GLA_NORMALIZER = 64, 128, 6, 16, 16.0
N_C, H_C, DECAY_LORA, AAA_LORA, GATE_LORA = 64, 12, 64, 64, 128
GN_EPS = 64e-5
D_FF = 5504
NORM_EPS = 1e-6
QKV_A = 2 * H_A * DK_A + H_A * DV_A
N_A = QKV_A + H_A * DV_A + 2 * H_A
N_B = 2 * H_B * DK_B + 2 * H_B * DV_B + GLA_RANK
N_CC = 3 * BRANCH_W + DECAY_LORA + AAA_LORA + GATE_LORA
N_PAIR = H_C // 2

LANES = 128
SUBLANES = 8
CHUNK = 64
PROMPT_GROUPS = 4
SAMPLE_GROUPS = 2
SAMPLE_ROWS = 8
FF_TILE = 512
VMEM_LIMIT = 60 * 1024 * 1024
VMEM_LIMIT_FFN = 63 * 1024 * 1024


def _cparams(n_axes, vmem_limit=VMEM_LIMIT):
    return pltpu.CompilerParams(dimension_semantics=("arbitrary",) * n_axes,
                                vmem_limit_bytes=vmem_limit)


def _pick_tile(n, prefs):
    for t in prefs:
        if n % t == 0:
            return t
    raise ValueError(f"no tile for {n}")


def _bdot(a, b):
    return jnp.dot(a.astype(BF16), b.astype(BF16), preferred_element_type=F32)


def _bdot_nt(a, b):
    return lax.dot_general(a.astype(BF16), b.astype(BF16), (((1,), (1,)), ((), ())),
                           preferred_element_type=F32)


def _bdot_tn(a, b):
    return lax.dot_general(a.astype(BF16), b.astype(BF16), (((0,), (0,)), ((), ())),
                           preferred_element_type=F32)


def _sigmoid(x):
    return jax.nn.sigmoid(x)


def _silu(x):
    return x * _sigmoid(x)


def _softplus(x):
    return jnp.maximum(x, 0.0) + jnp.log(1.0 + jnp.exp(-jnp.abs(x)))


def _rmsnorm_rows(x, g):
    return x * lax.rsqrt(jnp.mean(x * x, axis=-1, keepdims=True) + NORM_EPS) * g


def _chunk_masks(rows, seq_len):
    sh = int(math.log2(seq_len))
    ri = lax.broadcasted_iota(jnp.int32, (rows, rows), 0)
    ci = lax.broadcasted_iota(jnp.int32, (rows, rows), 1)
    same = (ri >> sh) == (ci >> sh)
    return same & (ri >= ci), same & (ri > ci), same


def _eye(rows):
    return (lax.broadcasted_iota(jnp.int32, (rows, rows), 0)
            == lax.broadcasted_iota(jnp.int32, (rows, rows), 1)).astype(F32)


def _inv_unit_lower(lms, eye, nil):
    xs = [eye - lm for lm in lms]
    ps = list(lms)
    k = 2
    while k < nil:
        ps = [_bdot(p, p) for p in ps]
        xs = [x + _bdot(x, p) for x, p in zip(xs, ps)]
        k *= 2
    return xs


def _row_valid(rows, seq_len, null_rows, width):
    r = lax.broadcasted_iota(jnp.int32, (rows, width), 0)
    return (r & (seq_len - 1)) >= null_rows


def _seq_rows(rows, seq_len, s, width):
    r = lax.broadcasted_iota(jnp.int32, (rows, width), 0) & (CHUNK - 1)
    return (r >> int(math.log2(seq_len))) == s


def _pick(sel, xv):
    return xv if sel is None else jnp.where(sel, xv, 0.0)


def _ffn_body(*refs, d_ff, norm_out):
    if norm_out:
        x_ref, g_ref, g2_ref, wg_ref, wu_ref, wd_ref, o_ref, n_ref, xn_ref = refs
    else:
        x_ref, g_ref, wg_ref, wu_ref, wd_ref, o_ref, xn_ref = refs
    j = pl.program_id(1)

    @pl.when(j == 0)
    def _():
        xn_ref[...] = _rmsnorm_rows(x_ref[...], g_ref[...]).astype(BF16)
        o_ref[...] = jnp.zeros_like(o_ref)

    dup = j * FF_TILE - _ff_start(j, d_ff)
    col_ok = lax.broadcasted_iota(jnp.int32, (1, FF_TILE), 1) >= dup
    row_ok = lax.broadcasted_iota(jnp.int32, (FF_TILE, 1), 0) >= dup
    xn = xn_ref[...]
    gate = jnp.dot(xn, wg_ref[0], preferred_element_type=F32)
    up = jnp.dot(xn, wu_ref[0], preferred_element_type=F32)
    act = jnp.where(col_ok, _silu(gate) * up, 0.0).astype(BF16)
    wd = jnp.where(row_ok, wd_ref[0], jnp.zeros((), BF16))
    o_ref[...] += jnp.dot(act, wd, preferred_element_type=F32)

    @pl.when(j == pl.num_programs(1) - 1)
    def _():
        if not norm_out:
            o_ref[...] = x_ref[...] + 0.5 * o_ref[...]
        else:
            tm = o_ref.shape[0]
            slab = tm // 4 if tm % 64 == 0 else tm
            for r0 in range(0, tm, slab):
                rows = pl.ds(r0, slab)
                out = x_ref[rows, :] + 0.5 * o_ref[rows, :]
                o_ref[rows, :] = out
                n_ref[rows, :] = _rmsnorm_rows(out, g2_ref[...]).astype(n_ref.dtype)


def _ff_start(j, d_ff):
    return jnp.minimum(j * FF_TILE, d_ff - FF_TILE)


def _ffn(x, g, w_up, w_down, layer, g_next=None):
    n, d = x.shape
    tm = _pick_tile(n, (1088, 544, 512, 256, 128, 64, 32, 16, 8))
    d_ff = w_down.shape[1]
    assert d_ff % LANES == 0 and d_ff >= FF_TILE
    nf = pl.cdiv(d_ff, FF_TILE)

    def up_spec(col0):
        return pl.BlockSpec((pl.Element(1), pl.Element(d), pl.Element(FF_TILE)),
                            lambda i, j: (layer, 0, pl.multiple_of(col0 + _ff_start(j, d_ff), LANES)))

    norm_out = g_next is not None
    vec_spec = pl.BlockSpec((1, d), lambda i, j: (0, 0))
    out_specs = [pl.BlockSpec((tm, d), lambda i, j: (i, 0))]
    out_shape = [jax.ShapeDtypeStruct((n, d), F32)]
    if norm_out:
        out_specs.append(pl.BlockSpec((tm, d), lambda i, j: (i, 0), pipeline_mode=pl.Buffered(1)))
        out_shape.append(jax.ShapeDtypeStruct((n, d), BF16))
    out = pl.pallas_call(
        functools.partial(_ffn_body, d_ff=d_ff, norm_out=norm_out),
        grid=(n // tm, nf),
        in_specs=[pl.BlockSpec((tm, d), lambda i, j: (i, 0)), vec_spec] + ([vec_spec] if norm_out else []) + [
            up_spec(0),
            up_spec(d_ff),
            pl.BlockSpec((pl.Element(1), pl.Element(FF_TILE), pl.Element(d)),
                         lambda i, j: (layer, pl.multiple_of(_ff_start(j, d_ff), LANES), 0)),
        ],
        out_specs=out_specs,
        out_shape=out_shape,
        scratch_shapes=[pltpu.VMEM((tm, d), BF16)],
        compiler_params=_cparams(2, VMEM_LIMIT_FFN if norm_out else VMEM_LIMIT),
        name="ffn",
    )(x, g.reshape(1, d), *([g_next.reshape(1, d)] if norm_out else []), w_up, w_up, w_down)
    return out if norm_out else out[0]


def _norm_cast_body(x_ref, g_ref, o_ref):
    o_ref[...] = _rmsnorm_rows(x_ref[...], g_ref[...]).astype(o_ref.dtype)


def _norm_cast(x, g, dtype, start=0, count=None):
    n, d = x.shape
    count = n - start if count is None else count
    tm = _pick_tile(math.gcd(start, count), (544, 512, 256, 128, 64, 32, 16, 8))
    first = start // tm
    return pl.pallas_call(
        _norm_cast_body,
        grid=(count // tm,),
        in_specs=[pl.BlockSpec((tm, d), lambda i: (first + i, 0)), pl.BlockSpec((1, d), lambda i: (0, 0))],
        out_specs=pl.BlockSpec((tm, d), lambda i: (i, 0)),
        out_shape=jax.ShapeDtypeStruct((count, d), dtype),
        compiler_params=_cparams(1),
        name="norm",
    )(x, g.reshape(1, d))


def _pad_rows(x):
    n, w = x.shape
    x3 = x.reshape(n // SUBLANES, SUBLANES, w)
    tail = lax.broadcasted_iota(jnp.int32, x3.shape, 1) >= SUBLANES // 2
    first = jnp.where(tail, pltpu.roll(x3, SUBLANES // 2, 1), 0.0)
    second = jnp.where(tail, x3, 0.0)
    return jnp.stack([first, second], axis=1).reshape(2 * n, w)


def _compact_rows(x):
    n, w = x.shape
    x4 = x.reshape(n // (2 * SUBLANES), 2, SUBLANES, w)
    head = lax.broadcasted_iota(jnp.int32, (n // (2 * SUBLANES), SUBLANES, w), 1) < SUBLANES // 2
    return jnp.where(head, pltpu.roll(x4[:, 0], SUBLANES // 2, 1), x4[:, 1]).reshape(n // 2, w)


WPREP_ROWS = 256


def _wprep_body(*refs, depth, n_main, misc):
    if misc:
        w_ref, ma_ref, mb_ref, o_ref = refs
    else:
        w_ref, o_ref = refs
    i = pl.program_id(0)

    @pl.when(i < n_main)
    def _():
        wb = w_ref[...].astype(BF16)
        for l in range(depth):
            o_ref[l] = wb[:, l, :]

    if misc:
        na, nb = misc

        @pl.when(i == n_main)
        def _():
            rows = lax.broadcasted_iota(jnp.int32, ma_ref.shape[0:1] + ma_ref.shape[2:3], 0)
            for l in range(depth):
                o_ref[l] = jnp.zeros(o_ref.shape[1:], BF16)
                o_ref[l, 0:ma_ref.shape[0], :] = jnp.where(rows < na, ma_ref[:, l, :], 0.0).astype(BF16)
                o_ref[l, LANES:LANES + mb_ref.shape[0], :] = jnp.where(rows < nb, mb_ref[:, l, :], 0.0).astype(BF16)


def _wprep(wt, row0, nrows, misc=None):
    n_all, depth, k = wt.shape
    assert nrows % WPREP_ROWS == 0
    n_main = nrows // WPREP_ROWS

    def win(rows, start_of):
        return pl.BlockSpec((pl.Element(rows), pl.Element(depth), pl.Element(k)),
                            lambda i: (start_of(i), 0, 0))

    in_specs = [win(WPREP_ROWS, lambda i: row0 + jnp.minimum(i, n_main - 1) * WPREP_ROWS)]
    args = [wt]
    body_misc = None
    if misc:
        row_a, n_a, row_b, n_b = misc
        small = 2 * SUBLANES
        assert n_a <= small and n_b <= small
        in_specs += [win(small, lambda i: row_a), win(small, lambda i: row_b)]
        args += [wt, wt]
        body_misc = (n_a, n_b)
    nblk = n_main + (1 if misc else 0)
    return pl.pallas_call(
        functools.partial(_wprep_body, depth=depth, n_main=n_main, misc=body_misc),
        grid=(nblk,),
        in_specs=in_specs,
        out_specs=pl.BlockSpec((depth, WPREP_ROWS, k), lambda i: (0, i, 0)),
        out_shape=jax.ShapeDtypeStruct((depth, nblk * WPREP_ROWS, k), BF16),
        compiler_params=_cparams(1),
        name="wprep",
    )(*args)


def _proj_body(x_ref, w_ref, o_ref, *pad_ref, sample_off, n_sample):
    res = lax.dot_general(x_ref[...], w_ref[...], (((1,), (1,)), ((), ())), preferred_element_type=F32)
    o_ref[...] = res.astype(o_ref.dtype)
    if pad_ref:
        @pl.when(pl.program_id(1) == pl.num_programs(1) - 1)
        def _():
            pad_ref[0][...] = _pad_rows(res[sample_off:sample_off + n_sample])


def _proj(xn, w, layer, tn, out_dtype=F32, n_sample=0):
    n, d = xn.shape
    nn = w.shape[1]
    tm = _pick_tile(n, (1088, 544, 512, 256, 128, 64, 32, 16))
    out_specs = [pl.BlockSpec((tm, tn), lambda j, i: (i, j))]
    out_shape = [jax.ShapeDtypeStruct((n, nn), out_dtype)]
    sample_off = tm - n_sample
    if n_sample:
        assert 0 <= sample_off and sample_off % SUBLANES == 0 and n_sample % SUBLANES == 0
        out_specs.append(pl.BlockSpec((2 * n_sample, tn), lambda j, i: (0, j)))
        out_shape.append(jax.ShapeDtypeStruct((2 * n_sample, nn), F32))
    out = pl.pallas_call(
        functools.partial(_proj_body, sample_off=sample_off, n_sample=n_sample),
        grid=(nn // tn, n // tm),
        in_specs=[
            pl.BlockSpec((tm, d), lambda j, i: (i, 0)),
            pl.BlockSpec((None, tn, d), lambda j, i: (layer, j, 0)),
        ],
        out_specs=out_specs,
        out_shape=out_shape,
        compiler_params=_cparams(2),
        name="proj",
    )(xn, w)
    return out if n_sample else out[0]


MERGE_TN = 2048


def _merge_body(yap_ref, ybp_ref, ycp_ref, yas_ref, ybs_ref, ycs_ref, ga_ref, gb_ref, gc_ref,
                wb_ref, wo_ref, h_ref, o_ref, *, prompt_tiles):
    i = pl.program_id(0)
    j = pl.program_id(1)

    @pl.when(j == 0)
    def _():
        o_ref[...] = h_ref[...]

    is_prompt = i < prompt_tiles
    ya = jnp.where(is_prompt, yap_ref[...], yas_ref[...])
    yb = jnp.where(is_prompt, ybp_ref[...], ybs_ref[...])
    yc = jnp.where(is_prompt, ycp_ref[...], ycs_ref[...])
    m = _sigmoid(ga_ref[...].astype(F32)) * jnp.dot(ya, wb_ref[0], preferred_element_type=F32)
    m += _sigmoid(gb_ref[...].astype(F32)) * jnp.dot(yb, wb_ref[1], preferred_element_type=F32)
    m += _sigmoid(gc_ref[...].astype(F32)) * jnp.dot(yc, wb_ref[2], preferred_element_type=F32)
    o_ref[...] += jnp.dot(m.astype(BF16), wo_ref[...], preferred_element_type=F32)


def _merge(ys_prompt, ys_sample, pg, wb, wo, h, layer):
    n, d = h.shape
    n_p, n_s = ys_prompt[0].shape[0], ys_sample[0].shape[0]
    tm = _pick_tile(math.gcd(n_p, n_s), (256, 128, 64, 32, 16))
    prompt_tiles = n_p // tm
    nj = d // MERGE_TN
    yp_spec = pl.BlockSpec((tm, BRANCH_W), lambda i, j: (jnp.minimum(i, prompt_tiles - 1), 0))
    ys_spec = pl.BlockSpec((tm, BRANCH_W), lambda i, j: (jnp.maximum(i - prompt_tiles, 0), 0))
    return pl.pallas_call(
        functools.partial(_merge_body, prompt_tiles=prompt_tiles),
        grid=(n // tm, nj),
        in_specs=[
            yp_spec, yp_spec, yp_spec, ys_spec, ys_spec, ys_spec,
            pl.BlockSpec((tm, MERGE_TN), lambda i, j: (i, j)),
            pl.BlockSpec((tm, MERGE_TN), lambda i, j: (i, nj + j)),
            pl.BlockSpec((tm, MERGE_TN), lambda i, j: (i, 2 * nj + j)),
            pl.BlockSpec((None, N_BRANCH, BRANCH_W, MERGE_TN), lambda i, j: (layer, 0, 0, j),
                         pipeline_mode=pl.Buffered(1)),
            pl.BlockSpec((None, MERGE_TN, d), lambda i, j: (layer, j, 0), pipeline_mode=pl.Buffered(1)),
            pl.BlockSpec((tm, d), lambda i, j: (i, 0)),
        ],
        out_specs=pl.BlockSpec((tm, d), lambda i, j: (i, 0)),
        out_shape=jax.ShapeDtypeStruct((n, d), F32),
        compiler_params=_cparams(2),
        name="merge",
    )(*ys_prompt, *ys_sample, pg, pg, pg, wb, wo, h)


def _layer_state_spec(layer, nseq, tail):
    zeros = (0,) * len(tail)
    return pl.BlockSpec((None, nseq) + tail, lambda b, c: (layer, b) + zeros)


def _stack_io(stack, depth, nstate, tail, n_args):
    shape = jax.ShapeDtypeStruct((depth, nstate) + tail, F32)
    if stack is None:
        return [], [], shape, {}
    return [pl.BlockSpec(memory_space=pl.ANY)], [stack], shape, {n_args: 1}


def _split3_dot(lmat, x):
    lb = lmat.astype(BF16)
    hi = x.astype(BF16)
    r1 = x - hi.astype(F32)
    mid = r1.astype(BF16)
    lo = (r1 - mid.astype(F32)).astype(BF16)
    return (jnp.dot(lb, hi, preferred_element_type=F32) + jnp.dot(lb, mid, preferred_element_type=F32)
            + jnp.dot(lb, lo, preferred_element_type=F32))


def _gdn_body(*refs, ngroup, nseq, null_rows, has_state, has_stack):
    refs = list(refs)
    xbuf = refs.pop()
    so_ref = refs.pop()
    y_ref = refs.pop()
    if has_stack:
        refs.pop()
    row_refs = [refs[3 * g:3 * g + 3] for g in range(ngroup)]
    refs = refs[3 * ngroup:]
    if has_state:
        cs_ref, cw_ref, prm_ref, nw_ref, s0_ref = refs
    else:
        cw_ref, prm_ref, nw_ref = refs
        s0_ref = so_ref
    rows = CHUNK
    seq_len = rows // nseq
    c = pl.program_id(1)
    chains = [(g, h) for g in range(ngroup) for h in range(H_A)]

    @pl.when(c == 0)
    def _():
        for g in range(ngroup):
            xbuf[g, 0:SUBLANES, :] = jnp.zeros((SUBLANES, QKV_A), F32)
        if not has_state:
            so_ref[...] = jnp.zeros_like(so_ref)

    @pl.when(c > 0)
    def _():
        for g in range(ngroup):
            xbuf[g, 0:SUBLANES, :] = xbuf[g, rows:rows + SUBLANES, :]

    for g in range(ngroup):
        xbuf[g, SUBLANES:SUBLANES + rows, :] = row_refs[g][0][...]
        if has_state:
            for s in range(nseq):
                r0 = SUBLANES + s * seq_len + null_rows - (CONV_W - 1)
                xbuf[g, r0:r0 + CONV_W - 1, :] = cs_ref[g * nseq + s]

    incl, strict, same = _chunk_masks(rows, seq_len)
    eye = _eye(rows)
    valid = _row_valid(rows, seq_len, null_rows, LANES) if null_rows else None
    sels = [_seq_rows(rows, seq_len, s, LANES) for s in range(nseq)] if nseq > 1 else [None]
    sels2 = [_seq_rows(2 * rows, seq_len, s, LANES) for s in range(nseq)] if nseq > 1 else [None]
    lstack = jnp.concatenate([incl.astype(F32), same.astype(F32)], axis=0)

    def conv_cols(g, c0):
        acc = xbuf[g, SUBLANES - 3:SUBLANES - 3 + rows, c0:c0 + LANES] * cw_ref[0:1, c0:c0 + LANES]
        for j in range(1, CONV_W):
            acc += (xbuf[g, SUBLANES - 3 + j:SUBLANES - 3 + j + rows, c0:c0 + LANES]
                    * cw_ref[j:j + 1, c0:c0 + LANES])
        return _silu(acc)

    qs, ks, kbs, rhs, decays, wq_parts, kds, gtcs = [], [], [], [], [], [], [], []
    for g in range(ngroup):
        ba = row_refs[g][2][...]
        beta_all = _sigmoid(ba)
        g_all = -jnp.exp(prm_ref[0:1, :]) * _softplus(ba + prm_ref[1:2, :])
        if null_rows:
            g_all = jnp.where(valid, g_all, 0.0)
        cs = _split3_dot(lstack, g_all)
        gc_all, gt_all = cs[0:rows], cs[rows:2 * rows]
        gc_t = gc_all.T
        for h in range(H_A):
            q = conv_cols(g, h * DK_A)
            k = conv_cols(g, H_A * DK_A + h * DK_A)
            v = conv_cols(g, 2 * H_A * DK_A + h * DV_A)
            q = q * lax.rsqrt(jnp.sum(q * q, axis=-1, keepdims=True) + 1e-6) * DK_A ** -0.5
            k = k * lax.rsqrt(jnp.sum(k * k, axis=-1, keepdims=True) + 1e-6)
            if null_rows:
                q = jnp.where(valid, q, 0.0)
                k = jnp.where(valid, k, 0.0)
                v = jnp.where(valid, v, 0.0)
            beta = beta_all[:, h:h + 1]
            gcc = gc_all[:, H_A + h:H_A + h + 1]
            gtc = gt_all[:, H_A + h:H_A + h + 1]
            gcr = gc_t[H_A + h:H_A + h + 1, :]
            egc = jnp.exp(gcc)
            kb = k * beta
            qs.append(q)
            ks.append(k)
            kbs.append(kb)
            rhs.append(jnp.concatenate([v * beta, kb * egc], axis=1))
            decays.append(jnp.where(incl, jnp.exp(jnp.where(incl, gcc - gcr, 0.0)), 0.0))
            wq_parts.append(q * egc)
            kds.append(k * jnp.exp(gtc - gcc))
            gtcs.append(gtc)

    n = range(len(chains))
    lms = [jnp.where(strict, _bdot_nt(kbs[i], ks[i]) * decays[i], 0.0) for i in n]
    attns = [_bdot_nt(qs[i], ks[i]) * decays[i] for i in n]
    tmats = _inv_unit_lower(lms, eye, seq_len)
    uws = [_bdot(tmats[i], rhs[i]) for i in n]
    us, os_ = [], []
    for i, (g, h) in enumerate(chains):
        u = uws[i][:, 0:DV_A]
        wq = jnp.concatenate([uws[i][:, DV_A:DV_A + DK_A], wq_parts[i]], axis=0)
        o = None
        for s in range(nseq):
            both = _bdot(_pick(sels2[s], wq), s0_ref[g * nseq + s, h])
            u = u - both[0:rows]
            o = both[rows:2 * rows] if o is None else o + both[rows:2 * rows]
        us.append(u)
        os_.append(o)
    os_ = [os_[i] + _bdot(attns[i], us[i]) for i in n]
    for i, (g, h) in enumerate(chains):
        for s in range(nseq):
            g_last = jnp.exp(gtcs[i][s * seq_len:s * seq_len + 1, :])
            so_ref[g * nseq + s, h] = (g_last * s0_ref[g * nseq + s, h]
                                       + _bdot_tn(_pick(sels[s], kds[i]), us[i]))
    for i, (g, h) in enumerate(chains):
        z = row_refs[g][1][:, h * DV_A:(h + 1) * DV_A]
        _store_y(y_ref, g, h * DV_A, _rmsnorm_rows(os_[i], nw_ref[...]) * _silu(z), null_rows)


def _row_spec(width, col, g, ngroup, nchunk):
    return pl.BlockSpec((CHUNK, width), lambda b, c: ((b * ngroup + g) * nchunk + c, col))


def _y_io(nbatch, nchunk, ngroup, null_rows):
    out_rows = CHUNK // 2 if null_rows else CHUNK
    return (pl.BlockSpec((ngroup, out_rows, BRANCH_W), lambda b, c: (b, c, 0)),
            jax.ShapeDtypeStruct((nbatch, nchunk * out_rows, BRANCH_W), BF16))


def _store_y(y_ref, g, c0, yv, null_rows):
    if null_rows:
        yv = _compact_rows(yv)
    y_ref[g, :, c0:c0 + yv.shape[1]] = yv.astype(y_ref.dtype)


def _gdn(p1, p3, conv_state, state, stack, layer, depth, conv_w, prm, norm_w, *,
         nbatch, nchunk, nseq, null_rows, ngroup):
    has_state = state is not None
    tail = (H_A, DK_A, DV_A)
    in_specs, args = [], []
    for g in range(ngroup):
        in_specs += [_row_spec(QKV_A, 0, g, ngroup, nchunk),
                     _row_spec(BRANCH_W, QKV_A // BRANCH_W, g, ngroup, nchunk),
                     _row_spec(LANES, N_CC // LANES, g, ngroup, nchunk)]
        args += [p1, p1, p3]
    if has_state:
        in_specs.append(_layer_state_spec(layer, ngroup * nseq, (CONV_W - 1, QKV_A)))
        args.append(conv_state)
    in_specs += [
        pl.BlockSpec((CONV_W, QKV_A), lambda b, c: (0, 0)),
        pl.BlockSpec((SUBLANES, LANES), lambda b, c: (0, 0)),
        pl.BlockSpec((1, DV_A), lambda b, c: (0, 0)),
    ]
    args += [conv_w, prm, norm_w]
    if has_state:
        in_specs.append(_layer_state_spec(layer, ngroup * nseq, tail))
        args.append(state)
    stack_specs, stack_args, stack_shape, aliases = _stack_io(stack, depth, nbatch * nseq, tail, len(args))
    y_spec, y_shape = _y_io(nbatch, nchunk, ngroup, null_rows)
    return pl.pallas_call(
        functools.partial(_gdn_body, ngroup=ngroup, nseq=nseq, null_rows=null_rows, has_state=has_state,
                          has_stack=stack is not None),
        grid=(nbatch // ngroup, nchunk),
        in_specs=in_specs + stack_specs,
        out_specs=[y_spec, _layer_state_spec(layer, ngroup * nseq, tail)],
        out_shape=[y_shape, stack_shape],
        input_output_aliases=aliases,
        scratch_shapes=[pltpu.VMEM((ngroup, CHUNK + SUBLANES, QKV_A), F32)],
        compiler_params=_cparams(2),
        name="gdn",
    )(*args, *stack_args)


def _gla_body(*refs, ngroup, nseq, null_rows, has_state, has_stack):
    refs = list(refs)
    so_ref = refs.pop()
    y_ref = refs.pop()
    if has_stack:
        refs.pop()
    row_refs = [refs[5 * g:5 * g + 5] for g in range(ngroup)]
    refs = refs[5 * ngroup:]
    if has_state:
        up_ref, bias_ref, nw_ref, s0_ref = refs
    else:
        up_ref, bias_ref, nw_ref = refs
        s0_ref = so_ref
    rows = CHUNK
    seq_len = rows // nseq
    c = pl.program_id(1)
    chains = [(g, h) for g in range(ngroup) for h in range(H_B)]
    if not has_state:
        @pl.when(c == 0)
        def _():
            so_ref[...] = jnp.zeros_like(so_ref)

    incl, _, same = _chunk_masks(rows, seq_len)
    ci = lax.broadcasted_iota(jnp.int32, (rows, rows), 1)
    first_half = same & ((ci & (seq_len - 1)) < seq_len // 2)
    wide = H_B * DK_B
    lstack = jnp.concatenate([incl.astype(F32), same.astype(F32), first_half.astype(F32)], axis=0)
    valid_w = _row_valid(rows, seq_len, null_rows, wide) if null_rows else None
    valid_k = _row_valid(rows, seq_len, null_rows, DK_B) if null_rows else None
    valid_v = _row_valid(rows, seq_len, null_rows, DV_B) if null_rows else None
    sels = [_seq_rows(rows, seq_len, s, DK_B) for s in range(nseq)] if nseq > 1 else [None]

    qis, kis, vs, q_ins, k_decs, a_lasts = [], [], [], [], [], []
    for g in range(ngroup):
        q_ref, k_ref, v_ref, _, gkl_ref = row_refs[g]
        x = _bdot(gkl_ref[...], up_ref[...]) + bias_ref[...]
        gk_all = (jnp.minimum(x, 0.0) - jnp.log(1.0 + jnp.exp(-jnp.abs(x)))) / GLA_NORMALIZER
        if null_rows:
            gk_all = jnp.where(valid_w, gk_all, 0.0)
        cs = _split3_dot(lstack, gk_all)
        bc_all, bt_all, an_all = cs[0:rows], cs[rows:2 * rows], cs[2 * rows:3 * rows]
        bt_t = bt_all.T
        for h in range(H_B):
            c0 = h * DK_B
            q = q_ref[:, c0:c0 + DK_B] * DK_B ** -0.5
            k = k_ref[:, c0:c0 + DK_B]
            v = v_ref[:, h * DV_B:(h + 1) * DV_B]
            if null_rows:
                q = jnp.where(valid_k, q, 0.0)
                k = jnp.where(valid_k, k, 0.0)
                v = jnp.where(valid_v, v, 0.0)
            bc = bc_all[:, c0:c0 + DK_B]
            bt = bt_all[:, c0:c0 + DK_B]
            an = an_all[:, c0:c0 + DK_B]
            qis.append(q * jnp.exp(bc - an))
            kis.append(k * jnp.exp(an - bc))
            vs.append(v)
            q_ins.append(q * jnp.exp(bc))
            k_decs.append(k * jnp.exp(bt - bc))
            a_lasts.append([jnp.exp(bt_t[c0:c0 + DK_B, s * seq_len:s * seq_len + 1])
                            for s in range(nseq)])

    n = range(len(chains))
    attns = [jnp.where(incl, _bdot_nt(qis[i], kis[i]), 0.0) for i in n]
    os_ = [_bdot(attns[i], vs[i]) for i in n]
    for i, (g, h) in enumerate(chains):
        for s in range(nseq):
            os_[i] = os_[i] + _bdot(_pick(sels[s], q_ins[i]), s0_ref[g * nseq + s, h])
    for i, (g, h) in enumerate(chains):
        for s in range(nseq):
            so_ref[g * nseq + s, h] = (a_lasts[i][s] * s0_ref[g * nseq + s, h]
                                       + _bdot_tn(_pick(sels[s], k_decs[i]), vs[i]))
    for i, (g, h) in enumerate(chains):
        gate = row_refs[g][3][:, h * DV_B:(h + 1) * DV_B]
        _store_y(y_ref, g, h * DV_B, _rmsnorm_rows(os_[i], nw_ref[...]) * _silu(gate), null_rows)


def _gla(p2, p3, state, stack, layer, depth, up_w, bias, norm_w, *, nbatch, nchunk, nseq, null_rows, ngroup):
    has_state = state is not None
    wide = H_B * DK_B
    tail = (H_B, DK_B, DV_B)
    in_specs, args = [], []
    for g in range(ngroup):
        in_specs += [_row_spec(wide, 0, g, ngroup, nchunk), _row_spec(wide, 1, g, ngroup, nchunk),
                     _row_spec(BRANCH_W, 1, g, ngroup, nchunk), _row_spec(BRANCH_W, 2, g, ngroup, nchunk)]
        in_specs.append(_row_spec(LANES, N_CC // LANES + 1, g, ngroup, nchunk))
        args += [p2, p2, p2, p2, p3]
    in_specs += [pl.BlockSpec((LANES, wide), lambda b, c: (0, 0)),
                 pl.BlockSpec((1, wide), lambda b, c: (0, 0)),
                 pl.BlockSpec((1, DV_B), lambda b, c: (0, 0))]
    args += [up_w, bias, norm_w]
    if has_state:
        in_specs.append(_layer_state_spec(layer, ngroup * nseq, tail))
        args.append(state)
    stack_specs, stack_args, stack_shape, aliases = _stack_io(stack, depth, nbatch * nseq, tail, len(args))
    y_spec, y_shape = _y_io(nbatch, nchunk, ngroup, null_rows)
    return pl.pallas_call(
        functools.partial(_gla_body, ngroup=ngroup, nseq=nseq, null_rows=null_rows, has_state=has_state,
                          has_stack=stack is not None),
        grid=(nbatch // ngroup, nchunk),
        in_specs=in_specs + stack_specs,
        out_specs=[y_spec, _layer_state_spec(layer, ngroup * nseq, tail)],
        out_shape=[y_shape, stack_shape],
        input_output_aliases=aliases,
        compiler_params=_cparams(2),
        name="gla",
    )(*args, *stack_args)


def _rwkv_body(*refs, ngroup, nseq, null_rows, has_state, has_stack):
    refs = list(refs)
    sp_ref = refs.pop()
    pbuf = refs.pop()
    so_ref = refs.pop()
    y_ref = refs.pop()
    if has_stack:
        refs.pop()
    pc_refs = refs[:ngroup]
    refs = refs[ngroup:]
    if has_state:
        sh_ref, mu_ref, wwa_ref, gup_ref, vec_ref, s0_ref = refs
    else:
        mu_ref, wwa_ref, gup_ref, vec_ref = refs
    rows = CHUNK
    seq_len = rows // nseq
    c = pl.program_id(1)
    chains = [(g, p) for g in range(ngroup) for p in range(N_PAIR)]
    pairs = range(len(chains))

    @pl.when(c == 0)
    def _():
        for g in range(ngroup):
            pbuf[g, 0:SUBLANES, :] = jnp.zeros((SUBLANES, N_CC), F32)
        if not has_state:
            sp_ref[...] = jnp.zeros_like(sp_ref)

    @pl.when(c > 0)
    def _():
        for g in range(ngroup):
            pbuf[g, 0:SUBLANES, :] = pbuf[g, rows:rows + SUBLANES, :]

    for g in range(ngroup):
        pbuf[g, SUBLANES:SUBLANES + rows, :] = pc_refs[g][...]
        if has_state:
            for s in range(nseq):
                r0 = SUBLANES + s * seq_len + null_rows - 1
                pbuf[g, r0:r0 + 1, :] = sh_ref[g * nseq + s]

    def xc_cols(g, c0, w):
        cur = pbuf[g, SUBLANES:SUBLANES + rows, c0:c0 + w]
        prev = pbuf[g, SUBLANES - 1:SUBLANES - 1 + rows, c0:c0 + w]
        return cur + (prev - cur) * mu_ref[:, c0:c0 + w]

    incl, strict, same = _chunk_masks(rows, seq_len)
    eye = _eye(rows)
    lane = lax.broadcasted_iota(jnp.int32, (rows, LANES), 1)
    lo = lane < N_C
    lo2 = lax.broadcasted_iota(jnp.int32, (2 * rows, LANES), 1) < N_C
    valid = _row_valid(rows, seq_len, null_rows, LANES) if null_rows else None
    valid_w = _row_valid(rows, seq_len, null_rows, BRANCH_W) if null_rows else None
    sels2 = [_seq_rows(2 * rows, seq_len, s, LANES) for s in range(nseq)] if nseq > 1 else [None]
    lstack = jnp.concatenate([incl.astype(F32), same.astype(F32)], axis=0)
    w0 = vec_ref[0:1, :]
    a0 = vec_ref[1:2, :]

    r2 = lax.broadcasted_iota(jnp.int32, (2 * N_C, LANES), 0)
    c2 = lax.broadcasted_iota(jnp.int32, (2 * N_C, LANES), 1)
    blockdiag = (r2 >= N_C) == (c2 >= N_C)

    def seg_sum(xv):
        s_lo = jnp.sum(jnp.where(lo, xv, 0.0), axis=-1, keepdims=True)
        s_hi = jnp.sum(jnp.where(lo, 0.0, xv), axis=-1, keepdims=True)
        return jnp.where(lo, s_lo, s_hi)

    def by_head(stacked):
        return jnp.where(lo, stacked[0:rows], stacked[rows:2 * rows])

    rs, k2s, vs, ar_stacks, b_ts, k_ts, tots, gates = [], [], [], [], [], [], [], []
    for g in range(ngroup):
        lora = xc_cols(g, 3 * BRANCH_W, DECAY_LORA + AAA_LORA + GATE_LORA)
        wa_in = lora[:, 0:LANES]
        wa_in = jnp.where(lo, jnp.tanh(wa_in), wa_in)
        wa = _bdot(wa_in, wwa_ref[...])
        g_all = _bdot(_sigmoid(lora[:, LANES:2 * LANES]), gup_ref[...])
        w_log = -_softplus(-(w0 + wa[:, 0:BRANCH_W])) - 0.5
        lw_all = -jnp.exp(w_log)
        a_all = _sigmoid(a0 + wa[:, BRANCH_W:2 * BRANCH_W])
        if null_rows:
            lw_all = jnp.where(valid_w, lw_all, 0.0)
        cs = _split3_dot(lstack, lw_all)
        cum_all, tot_all = cs[0:rows], cs[rows:2 * rows]
        for p in range(N_PAIR):
            c0 = p * LANES
            r = xc_cols(g, c0, LANES)
            k = xc_cols(g, BRANCH_W + c0, LANES)
            v = xc_cols(g, 2 * BRANCH_W + c0, LANES)
            a_p = a_all[:, c0:c0 + LANES]
            kkr = k * vec_ref[2:3, c0:c0 + LANES]
            kk = kkr * lax.rsqrt(seg_sum(kkr * kkr) + 1e-6)
            k2 = k * (1.0 + (a_p - 1.0) * vec_ref[3:4, c0:c0 + LANES])
            av = -kk
            bv = kk * a_p
            if null_rows:
                r = jnp.where(valid, r, 0.0)
                k2 = jnp.where(valid, k2, 0.0)
                v = jnp.where(valid, v, 0.0)
                av = jnp.where(valid, av, 0.0)
                bv = jnp.where(valid, bv, 0.0)
            cum = cum_all[:, c0:c0 + LANES]
            lw = lw_all[:, c0:c0 + LANES]
            g_inv = jnp.exp(-cum)
            rs.append(r)
            k2s.append(k2)
            vs.append(v)
            ar_stacks.append(jnp.concatenate([av * jnp.exp(cum - lw), r * jnp.exp(cum)], axis=0))
            b_ts.append(bv * g_inv)
            k_ts.append(k2 * g_inv)
            tots.append(tot_all[:, c0:c0 + LANES])
            gates.append(g_all[:, c0:c0 + LANES])

    a_abs, a_rbs, a_aks, a_rks = [], [], [], []
    for p in pairs:
        for hh in range(2):
            lhs = jnp.where(lo2 if hh == 0 else jnp.logical_not(lo2), ar_stacks[p], 0.0)
            mb = _bdot_nt(lhs, b_ts[p])
            mk = _bdot_nt(lhs, k_ts[p])
            a_abs.append(jnp.where(strict, mb[0:rows], 0.0))
            a_rbs.append(jnp.where(incl, mb[rows:2 * rows], 0.0))
            a_aks.append(jnp.where(strict, mk[0:rows], 0.0))
            a_rks.append(jnp.where(incl, mk[rows:2 * rows], 0.0))
    tmats = _inv_unit_lower([-m for m in a_abs], eye, seq_len)

    def pair_stack(mats, p):
        return jnp.concatenate([mats[2 * p], mats[2 * p + 1]], axis=0)

    states, xss = [], []
    for i, (g, p) in enumerate(chains):
        sps, xs = [], None
        for s in range(nseq):
            if has_state:
                sq = g * nseq + s
                sv = jnp.concatenate([s0_ref[sq, 2 * p], s0_ref[sq, 2 * p + 1]], axis=0)
                sp = jnp.where(blockdiag, jnp.concatenate([sv, sv], axis=1), 0.0)
            else:
                sp = sp_ref[i]
            both = _bdot_nt(_pick(sels2[s], ar_stacks[i]), sp)
            xs = both if xs is None else xs + both
            sps.append(sp)
        states.append(sps)
        xss.append(xs)
    yvs = [xss[i][0:rows] + by_head(_bdot(pair_stack(a_aks, i), vs[i])) for i in pairs]
    us = [by_head(_bdot(pair_stack(tmats, i), yvs[i])) for i in pairs]
    os_ = [xss[i][rows:2 * rows]
           + by_head(_bdot(pair_stack(a_rbs, i), us[i]) + _bdot(pair_stack(a_rks, i), vs[i]))
           for i in pairs]
    for i, (g, p) in enumerate(chains):
        uv = jnp.concatenate([us[i], vs[i]], axis=0)
        bk = jnp.concatenate([b_ts[i], k_ts[i]], axis=0)
        for s in range(nseq):
            upd = jnp.where(blockdiag, _bdot_tn(_pick(sels2[s], uv), bk), 0.0)
            sp_new = (states[i][s] + upd) * jnp.exp(tots[i][s * seq_len:s * seq_len + 1, :])
            if has_state:
                sq = g * nseq + s
                so_ref[sq, 2 * p] = sp_new[0:N_C, 0:N_C]
                so_ref[sq, 2 * p + 1] = pltpu.roll(sp_new[N_C:2 * N_C, :], N_C, 1)[:, 0:N_C]
            else:
                sp_ref[i] = sp_new

    for i, (g, p) in enumerate(chains):
        c0 = p * LANES
        o = os_[i]
        mean = seg_sum(o) * (1.0 / N_C)
        d = o - mean
        var = seg_sum(d * d) * (1.0 / N_C)
        on = d * lax.rsqrt(var + GN_EPS) * vec_ref[5:6, c0:c0 + LANES] + vec_ref[6:7, c0:c0 + LANES]
        bonus = seg_sum(rs[i] * k2s[i] * vec_ref[4:5, c0:c0 + LANES]) * vs[i]
        _store_y(y_ref, g, c0, (on + bonus) * gates[i], null_rows)

    if not has_state:
        @pl.when(c == pl.num_programs(1) - 1)
        def _():
            for i, (g, p) in enumerate(chains):
                sp = sp_ref[i]
                so_ref[g, 2 * p] = sp[0:N_C, 0:N_C]
                so_ref[g, 2 * p + 1] = pltpu.roll(sp[N_C:2 * N_C, :], N_C, 1)[:, 0:N_C]


def _rwkv(p3, shift_state, state, stack, layer, depth, mu, wwa, gup, vec, *,
          nbatch, nchunk, nseq, null_rows, ngroup):
    has_state = state is not None
    tail = (H_C, N_C, N_C)
    in_specs = [_row_spec(N_CC, 0, g, ngroup, nchunk) for g in range(ngroup)]
    args = [p3] * ngroup
    if has_state:
        in_specs.append(_layer_state_spec(layer, ngroup * nseq, (1, N_CC)))
        args.append(shift_state)
    in_specs += [
        pl.BlockSpec((1, N_CC), lambda b, c: (0, 0)),
        pl.BlockSpec((LANES, 2 * BRANCH_W), lambda b, c: (0, 0)),
        pl.BlockSpec((GATE_LORA, BRANCH_W), lambda b, c: (0, 0)),
        pl.BlockSpec((SUBLANES, BRANCH_W), lambda b, c: (0, 0)),
    ]
    args += [mu, wwa, gup, vec]
    if has_state:
        in_specs.append(_layer_state_spec(layer, ngroup * nseq, tail))
        args.append(state)
    stack_specs, stack_args, stack_shape, aliases = _stack_io(stack, depth, nbatch * nseq, tail, len(args))
    y_spec, y_shape = _y_io(nbatch, nchunk, ngroup, null_rows)
    return pl.pallas_call(
        functools.partial(_rwkv_body, ngroup=ngroup, nseq=nseq, null_rows=null_rows, has_state=has_state,
                          has_stack=stack is not None),
        grid=(nbatch // ngroup, nchunk),
        in_specs=in_specs + stack_specs,
        out_specs=[y_spec, _layer_state_spec(layer, ngroup * nseq, tail)],
        out_shape=[y_shape, stack_shape],
        input_output_aliases=aliases,
        scratch_shapes=[pltpu.VMEM((ngroup, CHUNK + SUBLANES, N_CC), F32),
                        pltpu.VMEM((ngroup * N_PAIR, 2 * N_C, 2 * N_C), F32)],
        compiler_params=_cparams(2),
        name="rwkv",
    )(*args, *stack_args)


def _prep_in(w_in):
    wt = jnp.transpose(w_in, (2, 0, 1))
    ab = QKV_A + H_A * DV_A
    w1 = _wprep(wt, 0, ab)
    w2 = _wprep(wt, N_A, N_B - GLA_RANK)
    w3 = _wprep(wt, N_A + N_B, N_CC, misc=(ab, 2 * H_A, N_A + N_B - GLA_RANK, GLA_RANK))
    wg = _wprep(wt, N_A + N_B + N_CC, N_BRANCH * D_MODEL)
    return w1, w2, w3, wg


def _prep_ffn(w_up, w_down):
    return w_up.astype(BF16), w_down.astype(BF16)


def _tail_rows(p, nb, t, n_tail, width):
    return jnp.stack([p[(b + 1) * t - n_tail:(b + 1) * t, :width] for b in range(nb)])


def kernel(x_prompt, x_sample, state_gdn, state_gdn_conv, state_gla, state_rwkv, state_rwkv_shift, w_in, conv_a, a_log, dt_bias, gdn_norm, gla_gk_up, gla_gk_bias, gla_norm, rwkv_mu, rwkv_w0, rwkv_w_up, rwkv_a0, rwkv_a_up, rwkv_g_up, rwkv_k_k, rwkv_k_a, rwkv_r_k, rwkv_ln_w, rwkv_ln_b, w_branch, w_out, norm_ff1, w_ff1_up, w_ff1_down, norm_mix, norm_ff2, w_ff2_up, w_ff2_down, norm_final):
    bp, tp, d = x_prompt.shape
    bs, ts, _ = x_sample.shape
    depth = w_in.shape[0]
    n_p, n_s = bp * tp, bs * ts
    null_rows = SAMPLE_ROWS - ts
    assert tp % CHUNK == 0 and null_rows == SAMPLE_ROWS // 2 and CONV_W - 1 <= null_rows
    seq_per_chunk = CHUNK // SAMPLE_ROWS
    assert bs % seq_per_chunk == 0
    nchunk_p = tp // CHUNK
    nstep_s = bs // seq_per_chunk
    prompt = dict(nbatch=bp, nchunk=nchunk_p, nseq=1, null_rows=0, ngroup=math.gcd(bp, PROMPT_GROUPS))
    sample = dict(nbatch=nstep_s, nchunk=1, nseq=seq_per_chunk, null_rows=null_rows,
                  ngroup=math.gcd(nstep_s, SAMPLE_GROUPS))

    x = jnp.concatenate([x_prompt.reshape(n_p, d), x_sample.reshape(n_s, d)], axis=0)
    gdn_p = gdn_s = gla_p = gla_s = rwkv_p = rwkv_s = None
    conv_p, conv_s, shift_p, shift_s = [], [], [], []
    ff1 = _prep_ffn(w_ff1_up, w_ff1_down)
    ff2 = _prep_ffn(w_ff2_up, w_ff2_down)
    w1, w2, w3, wg = _prep_in(w_in)
    wbr = w_branch.astype(BF16)
    wout = w_out.astype(BF16)
    for l in range(depth):
        h, hn = _ffn(x, norm_ff1[l], *ff1, l, g_next=norm_mix[l])
        p1, p1s = _proj(hn, w1, l, w1.shape[1] // 2, n_sample=n_s)
        p2, p2s = _proj(hn, w2, l, w2.shape[1] // 2, n_sample=n_s)
        p3, p3s = _proj(hn, w3, l, w3.shape[1] // 2, n_sample=n_s)
        pg = _proj(hn, wg, l, wg.shape[1] // 4, BF16)

        prm = jnp.zeros((SUBLANES, LANES), F32)
        prm = prm.at[0, H_A:2 * H_A].set(a_log[l]).at[1, H_A:2 * H_A].set(dt_bias[l])
        gnorm = gdn_norm[l].reshape(1, DV_A)
        up_w = jnp.pad(gla_gk_up[l], ((0, LANES - GLA_RANK), (0, 0))).astype(BF16)
        gk_bias = gla_gk_bias[l].reshape(1, H_B * DK_B)
        lnorm = gla_norm[l].reshape(1, DV_B)
        mu = rwkv_mu[l].reshape(1, N_CC)
        wwa = jnp.zeros((LANES, 2 * BRANCH_W), F32)
        wwa = wwa.at[0:DECAY_LORA, 0:BRANCH_W].set(rwkv_w_up[l])
        wwa = wwa.at[DECAY_LORA:DECAY_LORA + AAA_LORA, BRANCH_W:].set(rwkv_a_up[l]).astype(BF16)
        gup = rwkv_g_up[l].astype(BF16)
        vec = jnp.stack([rwkv_w0[l], rwkv_a0[l], rwkv_k_k[l], rwkv_k_a[l], rwkv_r_k[l],
                         rwkv_ln_w[l], rwkv_ln_b[l], jnp.zeros((BRANCH_W,), F32)], axis=0)

        ya_p, gdn_p = _gdn(p1, p3, None, None, gdn_p, l, depth, conv_a[l], prm, gnorm, **prompt)
        ya_s, gdn_s = _gdn(p1s, p3s, state_gdn_conv, state_gdn, gdn_s, l, depth, conv_a[l], prm, gnorm, **sample)
        yb_p, gla_p = _gla(p2, p3, None, gla_p, l, depth, up_w, gk_bias, lnorm, **prompt)
        yb_s, gla_s = _gla(p2s, p3s, state_gla, gla_s, l, depth, up_w, gk_bias, lnorm, **sample)
        yc_p, rwkv_p = _rwkv(p3, None, None, rwkv_p, l, depth, mu, wwa, gup, vec, **prompt)
        yc_s, rwkv_s = _rwkv(p3s, state_rwkv_shift, state_rwkv, rwkv_s, l, depth, mu, wwa, gup, vec, **sample)

        h = _merge([y.reshape(n_p, BRANCH_W) for y in (ya_p, yb_p, yc_p)],
                   [y.reshape(n_s, BRANCH_W) for y in (ya_s, yb_s, yc_s)], pg, wbr, wout, h, l)
        x = _ffn(h, norm_ff2[l], *ff2, l)

        conv_p.append(_tail_rows(p1, bp, tp, CONV_W - 1, QKV_A))
        conv_s.append(p1[n_p:].reshape(bs, ts, -1)[:, ts - (CONV_W - 1):, :QKV_A])
        shift_p.append(_tail_rows(p3, bp, tp, 1, N_CC))
        shift_s.append(p3[n_p:].reshape(bs, ts, -1)[:, ts - 1:, :N_CC])

    y_p = _norm_cast(x, norm_final, F32, 0, n_p)
    y_s = _norm_cast(x, norm_final, F32, n_p, n_s)
    return (y_p.reshape(bp, tp, d), y_s.reshape(bs, ts, d),
            gdn_p, gdn_s, jnp.stack(conv_p), jnp.stack(conv_s), gla_p, gla_s,
            rwkv_p, rwkv_s, jnp.stack(shift_p), jnp.stack(shift_s))
```

```python
import functools
import math

import jax
import jax.numpy as jnp
from jax import lax
from jax.experimental import pallas as pl
from jax.experimental.pallas import tpu as pltpu

F32 = jnp.float32
BF16 = jnp.bfloat16

D_MODEL = 2048
N_BRANCH = 3
BRANCH_W = 768
DK_A, DV_A, H_A, CONV_W = 128, 128, 6, 4
DK_B, DV_B, H_B, GLA_RANK, GLA_NORMALIZER = 64, 128, 6, 16, 16.0
N_C, H_C, DECAY_LORA, AAA_LORA, GATE_LORA = 64, 12, 64, 64, 128
GN_EPS = 64e-5
D_FF = 5504
NORM_EPS = 1e-6
QKV_A = 2 * H_A * DK_A + H_A * DV_A
N_A = QKV_A + H_A * DV_A + 2 * H_A
N_B = 2 * H_B * DK_B + 2 * H_B * DV_B + GLA_RANK
N_CC = 3 * BRANCH_W + DECAY_LORA + AAA_LORA + GATE_LORA
N_PAIR = H_C // 2

LANES = 128
SUBLANES = 8
CHUNK = 64
PROMPT_GROUPS = 4
SAMPLE_GROUPS = 2
SAMPLE_ROWS = 8
FF_TILE = 512
VMEM_LIMIT = 60 * 1024 * 1024


def _cparams(n_axes):
    return pltpu.CompilerParams(dimension_semantics=("arbitrary",) * n_axes,
                                vmem_limit_bytes=VMEM_LIMIT)


def _pick_tile(n, prefs):
    for t in prefs:
        if n % t == 0:
            return t
    raise ValueError(f"no tile for {n}")


def _bdot(a, b):
    return jnp.dot(a.astype(BF16), b.astype(BF16), preferred_element_type=F32)


def _bdot_nt(a, b):
    return lax.dot_general(a.astype(BF16), b.astype(BF16), (((1,), (1,)), ((), ())),
                           preferred_element_type=F32)


def _bdot_tn(a, b):
    return lax.dot_general(a.astype(BF16), b.astype(BF16), (((0,), (0,)), ((), ())),
                           preferred_element_type=F32)


def _sigmoid(x):
    return jax.nn.sigmoid(x)


def _silu(x):
    return x * _sigmoid(x)


def _softplus(x):
    return jnp.maximum(x, 0.0) + jnp.log(1.0 + jnp.exp(-jnp.abs(x)))


def _rmsnorm_rows(x, g):
    return x * lax.rsqrt(jnp.mean(x * x, axis=-1, keepdims=True) + NORM_EPS) * g


def _chunk_masks(rows, seq_len):
    sh = int(math.log2(seq_len))
    ri = lax.broadcasted_iota(jnp.int32, (rows, rows), 0)
    ci = lax.broadcasted_iota(jnp.int32, (rows, rows), 1)
    same = (ri >> sh) == (ci >> sh)
    return same & (ri >= ci), same & (ri > ci), same


def _eye(rows):
    return (lax.broadcasted_iota(jnp.int32, (rows, rows), 0)
            == lax.broadcasted_iota(jnp.int32, (rows, rows), 1)).astype(F32)


def _inv_unit_lower(lms, eye, nil):
    xs = [eye - lm for lm in lms]
    ps = list(lms)
    k = 2
    while k < nil:
        ps = [_bdot(p, p) for p in ps]
        xs = [x + _bdot(x, p) for x, p in zip(xs, ps)]
        k *= 2
    return xs


def _row_valid(rows, seq_len, null_rows, width):
    r = lax.broadcasted_iota(jnp.int32, (rows, width), 0)
    return (r & (seq_len - 1)) >= null_rows


def _seq_rows(rows, seq_len, s, width):
    r = lax.broadcasted_iota(jnp.int32, (rows, width), 0) & (CHUNK - 1)
    return (r >> int(math.log2(seq_len))) == s


def _pick(sel, xv):
    return xv if sel is None else jnp.where(sel, xv, 0.0)


def _ffn_body(x_ref, g_ref, wg_ref, wu_ref, wd_ref, o_ref, xn_ref, *, d_ff):
    j = pl.program_id(1)

    @pl.when(j == 0)
    def _():
        xn_ref[...] = _rmsnorm_rows(x_ref[...], g_ref[...]).astype(BF16)
        o_ref[...] = jnp.zeros_like(o_ref)

    dup = j * FF_TILE - _ff_start(j, d_ff)
    col_ok = lax.broadcasted_iota(jnp.int32, (1, FF_TILE), 1) >= dup
    row_ok = lax.broadcasted_iota(jnp.int32, (FF_TILE, 1), 0) >= dup
    xn = xn_ref[...]
    gate = jnp.dot(xn, wg_ref[0], preferred_element_type=F32)
    up = jnp.dot(xn, wu_ref[0], preferred_element_type=F32)
    act = jnp.where(col_ok, _silu(gate) * up, 0.0).astype(BF16)
    wd = jnp.where(row_ok, wd_ref[0], jnp.zeros((), BF16))
    o_ref[...] += jnp.dot(act, wd, preferred_element_type=F32)

    @pl.when(j == pl.num_programs(1) - 1)
    def _():
        o_ref[...] = x_ref[...] + 0.5 * o_ref[...]


def _ff_start(j, d_ff):
    return jnp.minimum(j * FF_TILE, d_ff - FF_TILE)


def _ffn(x, g, w_up, w_down, layer):
    n, d = x.shape
    tm = _pick_tile(n, (1088, 544, 512, 256, 128, 64, 32, 16, 8))
    d_ff = w_down.shape[1]
    assert d_ff % LANES == 0 and d_ff >= FF_TILE
    nf = pl.cdiv(d_ff, FF_TILE)

    def up_spec(col0):
        return pl.BlockSpec((pl.Element(1), pl.Element(d), pl.Element(FF_TILE)),
                            lambda i, j: (layer, 0, pl.multiple_of(col0 + _ff_start(j, d_ff), LANES)))

    return pl.pallas_call(
        functools.partial(_ffn_body, d_ff=d_ff),
        grid=(n // tm, nf),
        in_specs=[
            pl.BlockSpec((tm, d), lambda i, j: (i, 0)),
            pl.BlockSpec((1, d), lambda i, j: (0, 0)),
            up_spec(0),
            up_spec(d_ff),
            pl.BlockSpec((pl.Element(1), pl.Element(FF_TILE), pl.Element(d)),
                         lambda i, j: (layer, pl.multiple_of(_ff_start(j, d_ff), LANES), 0)),
        ],
        out_specs=pl.BlockSpec((tm, d), lambda i, j: (i, 0)),
        out_shape=jax.ShapeDtypeStruct((n, d), F32),
        scratch_shapes=[pltpu.VMEM((tm, d), BF16)],
        compiler_params=_cparams(2),
        name="ffn",
    )(x, g.reshape(1, d), w_up, w_up, w_down)


def _norm_cast_body(x_ref, g_ref, o_ref):
    o_ref[...] = _rmsnorm_rows(x_ref[...], g_ref[...]).astype(o_ref.dtype)


def _norm_cast(x, g, dtype, start=0, count=None):
    n, d = x.shape
    count = n - start if count is None else count
    tm = _pick_tile(math.gcd(start, count), (544, 512, 256, 128, 64, 32, 16, 8))
    first = start // tm
    return pl.pallas_call(
        _norm_cast_body,
        grid=(count // tm,),
        in_specs=[pl.BlockSpec((tm, d), lambda i: (first + i, 0)), pl.BlockSpec((1, d), lambda i: (0, 0))],
        out_specs=pl.BlockSpec((tm, d), lambda i: (i, 0)),
        out_shape=jax.ShapeDtypeStruct((count, d), dtype),
        compiler_params=_cparams(1),
        name="norm",
    )(x, g.reshape(1, d))


def _pad_rows(x):
    n, w = x.shape
    x3 = x.reshape(n // SUBLANES, SUBLANES, w)
    tail = lax.broadcasted_iota(jnp.int32, x3.shape, 1) >= SUBLANES // 2
    first = jnp.where(tail, pltpu.roll(x3, SUBLANES // 2, 1), 0.0)
    second = jnp.where(tail, x3, 0.0)
    return jnp.stack([first, second], axis=1).reshape(2 * n, w)


def _compact_rows(x):
    n, w = x.shape
    x4 = x.reshape(n // (2 * SUBLANES), 2, SUBLANES, w)
    head = lax.broadcasted_iota(jnp.int32, (n // (2 * SUBLANES), SUBLANES, w), 1) < SUBLANES // 2
    return jnp.where(head, pltpu.roll(x4[:, 0], SUBLANES // 2, 1), x4[:, 1]).reshape(n // 2, w)


WPREP_ROWS = 256


def _wprep_body(*refs, depth, n_main, misc):
    if misc:
        w_ref, ma_ref, mb_ref, o_ref = refs
    else:
        w_ref, o_ref = refs
    i = pl.program_id(0)

    @pl.when(i < n_main)
    def _():
        wb = w_ref[...].astype(BF16)
        for l in range(depth):
            o_ref[l] = wb[:, l, :]

    if misc:
        na, nb = misc

        @pl.when(i == n_main)
        def _():
            rows = lax.broadcasted_iota(jnp.int32, ma_ref.shape[0:1] + ma_ref.shape[2:3], 0)
            for l in range(depth):
                o_ref[l] = jnp.zeros(o_ref.shape[1:], BF16)
                o_ref[l, 0:ma_ref.shape[0], :] = jnp.where(rows < na, ma_ref[:, l, :], 0.0).astype(BF16)
                o_ref[l, LANES:LANES + mb_ref.shape[0], :] = jnp.where(rows < nb, mb_ref[:, l, :], 0.0).astype(BF16)


def _wprep(wt, row0, nrows, misc=None):
    n_all, depth, k = wt.shape
    assert nrows % WPREP_ROWS == 0
    n_main = nrows // WPREP_ROWS

    def win(rows, start_of):
        return pl.BlockSpec((pl.Element(rows), pl.Element(depth), pl.Element(k)),
                            lambda i: (start_of(i), 0, 0))

    in_specs = [win(WPREP_ROWS, lambda i: row0 + jnp.minimum(i, n_main - 1) * WPREP_ROWS)]
    args = [wt]
    body_misc = None
    if misc:
        row_a, n_a, row_b, n_b = misc
        small = 2 * SUBLANES
        assert n_a <= small and n_b <= small
        in_specs += [win(small, lambda i: row_a), win(small, lambda i: row_b)]
        args += [wt, wt]
        body_misc = (n_a, n_b)
    nblk = n_main + (1 if misc else 0)
    return pl.pallas_call(
        functools.partial(_wprep_body, depth=depth, n_main=n_main, misc=body_misc),
        grid=(nblk,),
        in_specs=in_specs,
        out_specs=pl.BlockSpec((depth, WPREP_ROWS, k), lambda i: (0, i, 0)),
        out_shape=jax.ShapeDtypeStruct((depth, nblk * WPREP_ROWS, k), BF16),
        compiler_params=_cparams(1),
        name="wprep",
    )(*args)


def _proj_body(x_ref, w_ref, o_ref, *pad_ref, sample_off, n_sample):
    res = lax.dot_general(x_ref[...], w_ref[...], (((1,), (1,)), ((), ())), preferred_element_type=F32)
    o_ref[...] = res.astype(o_ref.dtype)
    if pad_ref:
        @pl.when(pl.program_id(1) == pl.num_programs(1) - 1)
        def _():
            pad_ref[0][...] = _pad_rows(res[sample_off:sample_off + n_sample])


def _proj(xn, w, layer, tn, out_dtype=F32, n_sample=0):
    n, d = xn.shape
    nn = w.shape[1]
    tm = _pick_tile(n, (1088, 544, 512, 256, 128, 64, 32, 16))
    out_specs = [pl.BlockSpec((tm, tn), lambda j, i: (i, j))]
    out_shape = [jax.ShapeDtypeStruct((n, nn), out_dtype)]
    sample_off = tm - n_sample
    if n_sample:
        assert 0 <= sample_off and sample_off % SUBLANES == 0 and n_sample % SUBLANES == 0
        out_specs.append(pl.BlockSpec((2 * n_sample, tn), lambda j, i: (0, j)))
        out_shape.append(jax.ShapeDtypeStruct((2 * n_sample, nn), F32))
    out = pl.pallas_call(
        functools.partial(_proj_body, sample_off=sample_off, n_sample=n_sample),
        grid=(nn // tn, n // tm),
        in_specs=[
            pl.BlockSpec((tm, d), lambda j, i: (i, 0)),
            pl.BlockSpec((None, tn, d), lambda j, i: (layer, j, 0)),
        ],
        out_specs=out_specs,
        out_shape=out_shape,
        compiler_params=_cparams(2),
        name="proj",
    )(xn, w)
    return out if n_sample else out[0]


MERGE_TN = 2048


def _merge_body(yap_ref, ybp_ref, ycp_ref, yas_ref, ybs_ref, ycs_ref, ga_ref, gb_ref, gc_ref,
                wb_ref, wo_ref, h_ref, o_ref, *, prompt_tiles):
    i = pl.program_id(0)
    j = pl.program_id(1)

    @pl.when(j == 0)
    def _():
        o_ref[...] = h_ref[...]

    is_prompt = i < prompt_tiles
    ya = jnp.where(is_prompt, yap_ref[...], yas_ref[...])
    yb = jnp.where(is_prompt, ybp_ref[...], ybs_ref[...])
    yc = jnp.where(is_prompt, ycp_ref[...], ycs_ref[...])
    m = _sigmoid(ga_ref[...].astype(F32)) * jnp.dot(ya, wb_ref[0], preferred_element_type=F32)
    m += _sigmoid(gb_ref[...].astype(F32)) * jnp.dot(yb, wb_ref[1], preferred_element_type=F32)
    m += _sigmoid(gc_ref[...].astype(F32)) * jnp.dot(yc, wb_ref[2], preferred_element_type=F32)
    o_ref[...] += jnp.dot(m.astype(BF16), wo_ref[...], preferred_element_type=F32)


def _merge(ys_prompt, ys_sample, pg, wb, wo, h, layer):
    n, d = h.shape
    n_p, n_s = ys_prompt[0].shape[0], ys_sample[0].shape[0]
    tm = _pick_tile(math.gcd(n_p, n_s), (256, 128, 64, 32, 16))
    prompt_tiles = n_p // tm
    nj = d // MERGE_TN
    yp_spec = pl.BlockSpec((tm, BRANCH_W), lambda i, j: (jnp.minimum(i, prompt_tiles - 1), 0))
    ys_spec = pl.BlockSpec((tm, BRANCH_W), lambda i, j: (jnp.maximum(i - prompt_tiles, 0), 0))
    return pl.pallas_call(
        functools.partial(_merge_body, prompt_tiles=prompt_tiles),
        grid=(n // tm, nj),
        in_specs=[
            yp_spec, yp_spec, yp_spec, ys_spec, ys_spec, ys_spec,
            pl.BlockSpec((tm, MERGE_TN), lambda i, j: (i, j)),
            pl.BlockSpec((tm, MERGE_TN), lambda i, j: (i, nj + j)),
            pl.BlockSpec((tm, MERGE_TN), lambda i, j: (i, 2 * nj + j)),
            pl.BlockSpec((None, N_BRANCH, BRANCH_W, MERGE_TN), lambda i, j: (layer, 0, 0, j),
                         pipeline_mode=pl.Buffered(1)),
            pl.BlockSpec((None, MERGE_TN, d), lambda i, j: (layer, j, 0), pipeline_mode=pl.Buffered(1)),
            pl.BlockSpec((tm, d), lambda i, j: (i, 0)),
        ],
        out_specs=pl.BlockSpec((tm, d), lambda i, j: (i, 0)),
        out_shape=jax.ShapeDtypeStruct((n, d), F32),
        compiler_params=_cparams(2),
        name="merge",
    )(*ys_prompt, *ys_sample, pg, pg, pg, wb, wo, h)


def _layer_state_spec(layer, nseq, tail):
    zeros = (0,) * len(tail)
    return pl.BlockSpec((None, nseq) + tail, lambda b, c: (layer, b) + zeros)


def _stack_io(stack, depth, nstate, tail, n_args, out_index=1):
    shape = jax.ShapeDtypeStruct((depth, nstate) + tail, F32)
    if stack is None:
        return [], [], shape, {}
    return [pl.BlockSpec(memory_space=pl.ANY)], [stack], shape, {n_args: out_index}


def _split3_dot(lmat, x):
    lb = lmat.astype(BF16)
    hi = x.astype(BF16)
    r1 = x - hi.astype(F32)
    mid = r1.astype(BF16)
    lo = (r1 - mid.astype(F32)).astype(BF16)
    return (jnp.dot(lb, hi, preferred_element_type=F32) + jnp.dot(lb, mid, preferred_element_type=F32)
            + jnp.dot(lb, lo, preferred_element_type=F32))


def _gdn_body(*refs, ngroup, nseq, null_rows, has_state, has_stack):
    refs = list(refs)
    xbuf = refs.pop()
    ct_ref = refs.pop()
    so_ref = refs.pop()
    y_ref = refs.pop()
    if has_stack:
        del refs[-2:]
    row_refs = [refs[3 * g:3 * g + 3] for g in range(ngroup)]
    refs = refs[3 * ngroup:]
    if has_state:
        cs_ref, cw_ref, prm_ref, nw_ref, s0_ref = refs
    else:
        cw_ref, prm_ref, nw_ref = refs
        s0_ref = so_ref
    rows = CHUNK
    seq_len = rows // nseq
    c = pl.program_id(1)
    chains = [(g, h) for g in range(ngroup) for h in range(H_A)]

    @pl.when(c == 0)
    def _():
        for g in range(ngroup):
            xbuf[g, 0:SUBLANES, :] = jnp.zeros((SUBLANES, QKV_A), F32)
        if not has_state:
            so_ref[...] = jnp.zeros_like(so_ref)

    @pl.when(c > 0)
    def _():
        for g in range(ngroup):
            xbuf[g, 0:SUBLANES, :] = xbuf[g, rows:rows + SUBLANES, :]

    for g in range(ngroup):
        xbuf[g, SUBLANES:SUBLANES + rows, :] = row_refs[g][0][...]
        if has_state:
            for s in range(nseq):
                r0 = SUBLANES + s * seq_len + null_rows - (CONV_W - 1)
                xbuf[g, r0:r0 + CONV_W - 1, :] = cs_ref[g * nseq + s]

    @pl.when(c == pl.num_programs(1) - 1)
    def _():
        for g in range(ngroup):
            for s in range(nseq):
                r1 = SUBLANES + (s + 1) * seq_len
                ct_ref[g * nseq + s] = xbuf[g, r1 - (CONV_W - 1):r1, :]

    incl, strict, same = _chunk_masks(rows, seq_len)
    eye = _eye(rows)
    valid = _row_valid(rows, seq_len, null_rows, LANES) if null_rows else None
    sels = [_seq_rows(rows, seq_len, s, LANES) for s in range(nseq)] if nseq > 1 else [None]
    sels2 = [_seq_rows(2 * rows, seq_len, s, LANES) for s in range(nseq)] if nseq > 1 else [None]
    lstack = jnp.concatenate([incl.astype(F32), same.astype(F32)], axis=0)

    def conv_cols(g, c0):
        acc = xbuf[g, SUBLANES - 3:SUBLANES - 3 + rows, c0:c0 + LANES] * cw_ref[0:1, c0:c0 + LANES]
        for j in range(1, CONV_W):
            acc += (xbuf[g, SUBLANES - 3 + j:SUBLANES - 3 + j + rows, c0:c0 + LANES]
                    * cw_ref[j:j + 1, c0:c0 + LANES])
        return _silu(acc)

    qs, ks, kbs, rhs, decays, wq_parts, kds, gtcs = [], [], [], [], [], [], [], []
    for g in range(ngroup):
        ba = row_refs[g][2][...]
        beta_all = _sigmoid(ba)
        g_all = -jnp.exp(prm_ref[0:1, :]) * _softplus(ba + prm_ref[1:2, :])
        if null_rows:
            g_all = jnp.where(valid, g_all, 0.0)
        cs = _split3_dot(lstack, g_all)
        gc_all, gt_all = cs[0:rows], cs[rows:2 * rows]
        gc_t = gc_all.T
        for h in range(H_A):
            q = conv_cols(g, h * DK_A)
            k = conv_cols(g, H_A * DK_A + h * DK_A)
            v = conv_cols(g, 2 * H_A * DK_A + h * DV_A)
            q = q * lax.rsqrt(jnp.sum(q * q, axis=-1, keepdims=True) + 1e-6) * DK_A ** -0.5
            k = k * lax.rsqrt(jnp.sum(k * k, axis=-1, keepdims=True) + 1e-6)
            if null_rows:
                q = jnp.where(valid, q, 0.0)
                k = jnp.where(valid, k, 0.0)
                v = jnp.where(valid, v, 0.0)
            beta = beta_all[:, h:h + 1]
            gcc = gc_all[:, H_A + h:H_A + h + 1]
            gtc = gt_all[:, H_A + h:H_A + h + 1]
            gcr = gc_t[H_A + h:H_A + h + 1, :]
            egc = jnp.exp(gcc)
            kb = k * beta
            qs.append(q)
            ks.append(k)
            kbs.append(kb)
            rhs.append(jnp.concatenate([v * beta, kb * egc], axis=1))
            decays.append(jnp.where(incl, jnp.exp(jnp.where(incl, gcc - gcr, 0.0)), 0.0))
            wq_parts.append(q * egc)
            kds.append(k * jnp.exp(gtc - gcc))
            gtcs.append(gtc)

    n = range(len(chains))
    lms = [jnp.where(strict, _bdot_nt(kbs[i], ks[i]) * decays[i], 0.0) for i in n]
    attns = [_bdot_nt(qs[i], ks[i]) * decays[i] for i in n]
    tmats = _inv_unit_lower(lms, eye, seq_len)
    uws = [_bdot(tmats[i], rhs[i]) for i in n]
    us, os_ = [], []
    for i, (g, h) in enumerate(chains):
        u = uws[i][:, 0:DV_A]
        wq = jnp.concatenate([uws[i][:, DV_A:DV_A + DK_A], wq_parts[i]], axis=0)
        o = None
        for s in range(nseq):
            both = _bdot(_pick(sels2[s], wq), s0_ref[g * nseq + s, h])
            u = u - both[0:rows]
            o = both[rows:2 * rows] if o is None else o + both[rows:2 * rows]
        us.append(u)
        os_.append(o)
    os_ = [os_[i] + _bdot(attns[i], us[i]) for i in n]
    for i, (g, h) in enumerate(chains):
        for s in range(nseq):
            g_last = jnp.exp(gtcs[i][s * seq_len:s * seq_len + 1, :])
            so_ref[g * nseq + s, h] = (g_last * s0_ref[g * nseq + s, h]
                                       + _bdot_tn(_pick(sels[s], kds[i]), us[i]))
    for i, (g, h) in enumerate(chains):
        z = row_refs[g][1][:, h * DV_A:(h + 1) * DV_A]
        _store_y(y_ref, g, h * DV_A, _rmsnorm_rows(os_[i], nw_ref[...]) * _silu(z), null_rows)


def _row_spec(width, col, g, ngroup, nchunk):
    return pl.BlockSpec((CHUNK, width), lambda b, c: ((b * ngroup + g) * nchunk + c, col))


def _y_io(nbatch, nchunk, ngroup, null_rows):
    out_rows = CHUNK // 2 if null_rows else CHUNK
    return (pl.BlockSpec((ngroup, out_rows, BRANCH_W), lambda b, c: (b, c, 0)),
            jax.ShapeDtypeStruct((nbatch, nchunk * out_rows, BRANCH_W), BF16))


def _store_y(y_ref, g, c0, yv, null_rows):
    if null_rows:
        yv = _compact_rows(yv)
    y_ref[g, :, c0:c0 + yv.shape[1]] = yv.astype(y_ref.dtype)


def _gdn(p1, p3, conv_state, state, stacks, layer, depth, conv_w, prm, norm_w, *,
         nbatch, nchunk, nseq, null_rows, ngroup):
    has_state = state is not None
    stack, conv_stack = stacks
    tail = (H_A, DK_A, DV_A)
    conv_tail = (CONV_W - 1, QKV_A)
    in_specs, args = [], []
    for g in range(ngroup):
        in_specs += [_row_spec(QKV_A, 0, g, ngroup, nchunk),
                     _row_spec(BRANCH_W, QKV_A // BRANCH_W, g, ngroup, nchunk),
                     _row_spec(LANES, N_CC // LANES, g, ngroup, nchunk)]
        args += [p1, p1, p3]
    if has_state:
        in_specs.append(_layer_state_spec(layer, ngroup * nseq, (CONV_W - 1, QKV_A)))
        args.append(conv_state)
    in_specs += [
        pl.BlockSpec((CONV_W, QKV_A), lambda b, c: (0, 0)),
        pl.BlockSpec((SUBLANES, LANES), lambda b, c: (0, 0)),
        pl.BlockSpec((1, DV_A), lambda b, c: (0, 0)),
    ]
    args += [conv_w, prm, norm_w]
    if has_state:
        in_specs.append(_layer_state_spec(layer, ngroup * nseq, tail))
        args.append(state)
    stack_specs, stack_args, stack_shape, aliases = _stack_io(stack, depth, nbatch * nseq, tail, len(args))
    cs_specs, cs_args, cs_shape, cs_aliases = _stack_io(conv_stack, depth, nbatch * nseq, conv_tail,
                                                        len(args) + len(stack_args), out_index=2)
    y_spec, y_shape = _y_io(nbatch, nchunk, ngroup, null_rows)
    y, new_stack, new_conv = pl.pallas_call(
        functools.partial(_gdn_body, ngroup=ngroup, nseq=nseq, null_rows=null_rows, has_state=has_state,
                          has_stack=stack is not None),
        grid=(nbatch // ngroup, nchunk),
        in_specs=in_specs + stack_specs + cs_specs,
        out_specs=[y_spec, _layer_state_spec(layer, ngroup * nseq, tail),
                   _layer_state_spec(layer, ngroup * nseq, conv_tail)],
        out_shape=[y_shape, stack_shape, cs_shape],
        input_output_aliases={**aliases, **cs_aliases},
        scratch_shapes=[pltpu.VMEM((ngroup, CHUNK + SUBLANES, QKV_A), F32)],
        compiler_params=_cparams(2),
        name="gdn",
    )(*args, *stack_args, *cs_args)
    return y, (new_stack, new_conv)


def _gla_body(*refs, ngroup, nseq, null_rows, has_state, has_stack):
    refs = list(refs)
    so_ref = refs.pop()
    y_ref = refs.pop()
    if has_stack:
        refs.pop()
    row_refs = [refs[5 * g:5 * g + 5] for g in range(ngroup)]
    refs = refs[5 * ngroup:]
    if has_state:
        up_ref, bias_ref, nw_ref, s0_ref = refs
    else:
        up_ref, bias_ref, nw_ref = refs
        s0_ref = so_ref
    rows = CHUNK
    seq_len = rows // nseq
    c = pl.program_id(1)
    chains = [(g, h) for g in range(ngroup) for h in range(H_B)]
    if not has_state:
        @pl.when(c == 0)
        def _():
            so_ref[...] = jnp.zeros_like(so_ref)

    incl, _, same = _chunk_masks(rows, seq_len)
    ci = lax.broadcasted_iota(jnp.int32, (rows, rows), 1)
    first_half = same & ((ci & (seq_len - 1)) < seq_len // 2)
    wide = H_B * DK_B
    lstack = jnp.concatenate([incl.astype(F32), same.astype(F32), first_half.astype(F32)], axis=0)
    valid_w = _row_valid(rows, seq_len, null_rows, wide) if null_rows else None
    valid_k = _row_valid(rows, seq_len, null_rows, DK_B) if null_rows else None
    valid_v = _row_valid(rows, seq_len, null_rows, DV_B) if null_rows else None
    sels = [_seq_rows(rows, seq_len, s, DK_B) for s in range(nseq)] if nseq > 1 else [None]

    qis, kis, vs, q_ins, k_decs, a_lasts = [], [], [], [], [], []
    for g in range(ngroup):
        q_ref, k_ref, v_ref, _, gkl_ref = row_refs[g]
        x = _bdot(gkl_ref[...], up_ref[...]) + bias_ref[...]
        gk_all = (jnp.minimum(x, 0.0) - jnp.log(1.0 + jnp.exp(-jnp.abs(x)))) / GLA_NORMALIZER
        if null_rows:
            gk_all = jnp.where(valid_w, gk_all, 0.0)
        cs = _split3_dot(lstack, gk_all)
        bc_all, bt_all, an_all = cs[0:rows], cs[rows:2 * rows], cs[2 * rows:3 * rows]
        bt_t = bt_all.T
        for h in range(H_B):
            c0 = h * DK_B
            q = q_ref[:, c0:c0 + DK_B] * DK_B ** -0.5
            k = k_ref[:, c0:c0 + DK_B]
            v = v_ref[:, h * DV_B:(h + 1) * DV_B]
            if null_rows:
                q = jnp.where(valid_k, q, 0.0)
                k = jnp.where(valid_k, k, 0.0)
                v = jnp.where(valid_v, v, 0.0)
            bc = bc_all[:, c0:c0 + DK_B]
            bt = bt_all[:, c0:c0 + DK_B]
            an = an_all[:, c0:c0 + DK_B]
            qis.append(q * jnp.exp(bc - an))
            kis.append(k * jnp.exp(an - bc))
            vs.append(v)
            q_ins.append(q * jnp.exp(bc))
            k_decs.append(k * jnp.exp(bt - bc))
            a_lasts.append([jnp.exp(bt_t[c0:c0 + DK_B, s * seq_len:s * seq_len + 1])
                            for s in range(nseq)])

    n = range(len(chains))
    attns = [jnp.where(incl, _bdot_nt(qis[i], kis[i]), 0.0) for i in n]
    os_ = [_bdot(attns[i], vs[i]) for i in n]
    for i, (g, h) in enumerate(chains):
        for s in range(nseq):
            os_[i] = os_[i] + _bdot(_pick(sels[s], q_ins[i]), s0_ref[g * nseq + s, h])
    for i, (g, h) in enumerate(chains):
        for s in range(nseq):
            so_ref[g * nseq + s, h] = (a_lasts[i][s] * s0_ref[g * nseq + s, h]
                                       + _bdot_tn(_pick(sels[s], k_decs[i]), vs[i]))
    for i, (g, h) in enumerate(chains):
        gate = row_refs[g][3][:, h * DV_B:(h + 1) * DV_B]
        _store_y(y_ref, g, h * DV_B, _rmsnorm_rows(os_[i], nw_ref[...]) * _silu(gate), null_rows)


def _gla(p2, p3, state, stack, layer, depth, up_w, bias, norm_w, *, nbatch, nchunk, nseq, null_rows, ngroup):
    has_state = state is not None
    wide = H_B * DK_B
    tail = (H_B, DK_B, DV_B)
    in_specs, args = [], []
    for g in range(ngroup):
        in_specs += [_row_spec(wide, 0, g, ngroup, nchunk), _row_spec(wide, 1, g, ngroup, nchunk),
                     _row_spec(BRANCH_W, 1, g, ngroup, nchunk), _row_spec(BRANCH_W, 2, g, ngroup, nchunk)]
        in_specs.append(_row_spec(LANES, N_CC // LANES + 1, g, ngroup, nchunk))
        args += [p2, p2, p2, p2, p3]
    in_specs += [pl.BlockSpec((LANES, wide), lambda b, c: (0, 0)),
                 pl.BlockSpec((1, wide), lambda b, c: (0, 0)),
                 pl.BlockSpec((1, DV_B), lambda b, c: (0, 0))]
    args += [up_w, bias, norm_w]
    if has_state:
        in_specs.append(_layer_state_spec(layer, ngroup * nseq, tail))
        args.append(state)
    stack_specs, stack_args, stack_shape, aliases = _stack_io(stack, depth, nbatch * nseq, tail, len(args))
    y_spec, y_shape = _y_io(nbatch, nchunk, ngroup, null_rows)
    return pl.pallas_call(
        functools.partial(_gla_body, ngroup=ngroup, nseq=nseq, null_rows=null_rows, has_state=has_state,
                          has_stack=stack is not None),
        grid=(nbatch // ngroup, nchunk),
        in_specs=in_specs + stack_specs,
        out_specs=[y_spec, _layer_state_spec(layer, ngroup * nseq, tail)],
        out_shape=[y_shape, stack_shape],
        input_output_aliases=aliases,
        compiler_params=_cparams(2),
        name="gla",
    )(*args, *stack_args)


def _rwkv_body(*refs, ngroup, nseq, null_rows, has_state, has_stack):
    refs = list(refs)
    sp_ref = refs.pop()
    pbuf = refs.pop()
    st_ref = refs.pop()
    so_ref = refs.pop()
    y_ref = refs.pop()
    if has_stack:
        del refs[-2:]
    pc_refs = refs[:ngroup]
    refs = refs[ngroup:]
    if has_state:
        sh_ref, mu_ref, wwa_ref, gup_ref, vec_ref, s0_ref = refs
    else:
        mu_ref, wwa_ref, gup_ref, vec_ref = refs
    rows = CHUNK
    seq_len = rows // nseq
    c = pl.program_id(1)
    chains = [(g, p) for g in range(ngroup) for p in range(N_PAIR)]
    pairs = range(len(chains))

    @pl.when(c == 0)
    def _():
        for g in range(ngroup):
            pbuf[g, 0:SUBLANES, :] = jnp.zeros((SUBLANES, N_CC), F32)
        if not has_state:
            sp_ref[...] = jnp.zeros_like(sp_ref)

    @pl.when(c > 0)
    def _():
        for g in range(ngroup):
            pbuf[g, 0:SUBLANES, :] = pbuf[g, rows:rows + SUBLANES, :]

    for g in range(ngroup):
        pbuf[g, SUBLANES:SUBLANES + rows, :] = pc_refs[g][...]
        if has_state:
            for s in range(nseq):
                r0 = SUBLANES + s * seq_len + null_rows - 1
                pbuf[g, r0:r0 + 1, :] = sh_ref[g * nseq + s]

    @pl.when(c == pl.num_programs(1) - 1)
    def _():
        for g in range(ngroup):
            for s in range(nseq):
                r1 = SUBLANES + (s + 1) * seq_len
                st_ref[g * nseq + s] = pbuf[g, r1 - 1:r1, :]

    def xc_cols(g, c0, w):
        cur = pbuf[g, SUBLANES:SUBLANES + rows, c0:c0 + w]
        prev = pbuf[g, SUBLANES - 1:SUBLANES - 1 + rows, c0:c0 + w]
        return cur + (prev - cur) * mu_ref[:, c0:c0 + w]

    incl, strict, same = _chunk_masks(rows, seq_len)
    eye = _eye(rows)
    lane = lax.broadcasted_iota(jnp.int32, (rows, LANES), 1)
    lo = lane < N_C
    lo2 = lax.broadcasted_iota(jnp.int32, (2 * rows, LANES), 1) < N_C
    valid = _row_valid(rows, seq_len, null_rows, LANES) if null_rows else None
    valid_w = _row_valid(rows, seq_len, null_rows, BRANCH_W) if null_rows else None
    sels2 = [_seq_rows(2 * rows, seq_len, s, LANES) for s in range(nseq)] if nseq > 1 else [None]
    lstack = jnp.concatenate([incl.astype(F32), same.astype(F32)], axis=0)
    w0 = vec_ref[0:1, :]
    a0 = vec_ref[1:2, :]

    r2 = lax.broadcasted_iota(jnp.int32, (2 * N_C, LANES), 0)
    c2 = lax.broadcasted_iota(jnp.int32, (2 * N_C, LANES), 1)
    blockdiag = (r2 >= N_C) == (c2 >= N_C)

    def seg_sum(xv):
        s_lo = jnp.sum(jnp.where(lo, xv, 0.0), axis=-1, keepdims=True)
        s_hi = jnp.sum(jnp.where(lo, 0.0, xv), axis=-1, keepdims=True)
        return jnp.where(lo, s_lo, s_hi)

    def by_head(stacked):
        return jnp.where(lo, stacked[0:rows], stacked[rows:2 * rows])

    rs, k2s, vs, ar_stacks, b_ts, k_ts, tots, gates = [], [], [], [], [], [], [], []
    for g in range(ngroup):
        lora = xc_cols(g, 3 * BRANCH_W, DECAY_LORA + AAA_LORA + GATE_LORA)
        wa_in = lora[:, 0:LANES]
        wa_in = jnp.where(lo, jnp.tanh(wa_in), wa_in)
        wa = _bdot(wa_in, wwa_ref[...])
        g_all = _bdot(_sigmoid(lora[:, LANES:2 * LANES]), gup_ref[...])
        w_log = -_softplus(-(w0 + wa[:, 0:BRANCH_W])) - 0.5
        lw_all = -jnp.exp(w_log)
        a_all = _sigmoid(a0 + wa[:, BRANCH_W:2 * BRANCH_W])
        if null_rows:
            lw_all = jnp.where(valid_w, lw_all, 0.0)
        cs = _split3_dot(lstack, lw_all)
        cum_all, tot_all = cs[0:rows], cs[rows:2 * rows]
        for p in range(N_PAIR):
            c0 = p * LANES
            r = xc_cols(g, c0, LANES)
            k = xc_cols(g, BRANCH_W + c0, LANES)
            v = xc_cols(g, 2 * BRANCH_W + c0, LANES)
            a_p = a_all[:, c0:c0 + LANES]
            kkr = k * vec_ref[2:3, c0:c0 + LANES]
            kk = kkr * lax.rsqrt(seg_sum(kkr * kkr) + 1e-6)
            k2 = k * (1.0 + (a_p - 1.0) * vec_ref[3:4, c0:c0 + LANES])
            av = -kk
            bv = kk * a_p
            if null_rows:
                r = jnp.where(valid, r, 0.0)
                k2 = jnp.where(valid, k2, 0.0)
                v = jnp.where(valid, v, 0.0)
                av = jnp.where(valid, av, 0.0)
                bv = jnp.where(valid, bv, 0.0)
            cum = cum_all[:, c0:c0 + LANES]
            lw = lw_all[:, c0:c0 + LANES]
            g_inv = jnp.exp(-cum)
            rs.append(r)
            k2s.append(k2)
            vs.append(v)
            ar_stacks.append(jnp.concatenate([av * jnp.exp(cum - lw), r * jnp.exp(cum)], axis=0))
            b_ts.append(bv * g_inv)
            k_ts.append(k2 * g_inv)
            tots.append(tot_all[:, c0:c0 + LANES])
            gates.append(g_all[:, c0:c0 + LANES])

    a_abs, a_rbs, a_aks, a_rks = [], [], [], []
    for p in pairs:
        for hh in range(2):
            lhs = jnp.where(lo2 if hh == 0 else jnp.logical_not(lo2), ar_stacks[p], 0.0)
            mb = _bdot_nt(lhs, b_ts[p])
            mk = _bdot_nt(lhs, k_ts[p])
            a_abs.append(jnp.where(strict, mb[0:rows], 0.0))
            a_rbs.append(jnp.where(incl, mb[rows:2 * rows], 0.0))
            a_aks.append(jnp.where(strict, mk[0:rows], 0.0))
            a_rks.append(jnp.where(incl, mk[rows:2 * rows], 0.0))
    tmats = _inv_unit_lower([-m for m in a_abs], eye, seq_len)

    def pair_stack(mats, p):
        return jnp.concatenate([mats[2 * p], mats[2 * p + 1]], axis=0)

    states, xss = [], []
    for i, (g, p) in enumerate(chains):
        sps, xs = [], None
        for s in range(nseq):
            if has_state:
                sq = g * nseq + s
                sv = jnp.concatenate([s0_ref[sq, 2 * p], s0_ref[sq, 2 * p + 1]], axis=0)
                sp = jnp.where(blockdiag, jnp.concatenate([sv, sv], axis=1), 0.0)
            else:
                sp = sp_ref[i]
            both = _bdot_nt(_pick(sels2[s], ar_stacks[i]), sp)
            xs = both if xs is None else xs + both
            sps.append(sp)
        states.append(sps)
        xss.append(xs)
    yvs = [xss[i][0:rows] + by_head(_bdot(pair_stack(a_aks, i), vs[i])) for i in pairs]
    us = [by_head(_bdot(pair_stack(tmats, i), yvs[i])) for i in pairs]
    os_ = [xss[i][rows:2 * rows]
           + by_head(_bdot(pair_stack(a_rbs, i), us[i]) + _bdot(pair_stack(a_rks, i), vs[i]))
           for i in pairs]
    for i, (g, p) in enumerate(chains):
        uv = jnp.concatenate([us[i], vs[i]], axis=0)
        bk = jnp.concatenate([b_ts[i], k_ts[i]], axis=0)
        for s in range(nseq):
            upd = jnp.where(blockdiag, _bdot_tn(_pick(sels2[s], uv), bk), 0.0)
            sp_new = (states[i][s] + upd) * jnp.exp(tots[i][s * seq_len:s * seq_len + 1, :])
            if has_state:
                sq = g * nseq + s
                so_ref[sq, 2 * p] = sp_new[0:N_C, 0:N_C]
                so_ref[sq, 2 * p + 1] = pltpu.roll(sp_new[N_C:2 * N_C, :], N_C, 1)[:, 0:N_C]
            else:
                sp_ref[i] = sp_new

    for i, (g, p) in enumerate(chains):
        c0 = p * LANES
        o = os_[i]
        mean = seg_sum(o) * (1.0 / N_C)
        d = o - mean
        var = seg_sum(d * d) * (1.0 / N_C)
        on = d * lax.rsqrt(var + GN_EPS) * vec_ref[5:6, c0:c0 + LANES] + vec_ref[6:7, c0:c0 + LANES]
        bonus = seg_sum(rs[i] * k2s[i] * vec_ref[4:5, c0:c0 + LANES]) * vs[i]
        _store_y(y_ref, g, c0, (on + bonus) * gates[i], null_rows)

    if not has_state:
        @pl.when(c == pl.num_programs(1) - 1)
        def _():
            for i, (g, p) in enumerate(chains):
                sp = sp_ref[i]
                so_ref[g, 2 * p] = sp[0:N_C, 0:N_C]
                so_ref[g, 2 * p + 1] = pltpu.roll(sp[N_C:2 * N_C, :], N_C, 1)[:, 0:N_C]


def _rwkv(p3, shift_state, state, stacks, layer, depth, mu, wwa, gup, vec, *,
          nbatch, nchunk, nseq, null_rows, ngroup):
    has_state = state is not None
    stack, shift_stack = stacks
    tail = (H_C, N_C, N_C)
    shift_tail = (1, N_CC)
    in_specs = [_row_spec(N_CC, 0, g, ngroup, nchunk) for g in range(ngroup)]
    args = [p3] * ngroup
    if has_state:
        in_specs.append(_layer_state_spec(layer, ngroup * nseq, (1, N_CC)))
        args.append(shift_state)
    in_specs += [
        pl.BlockSpec((1, N_CC), lambda b, c: (0, 0)),
        pl.BlockSpec((LANES, 2 * BRANCH_W), lambda b, c: (0, 0)),
        pl.BlockSpec((GATE_LORA, BRANCH_W), lambda b, c: (0, 0)),
        pl.BlockSpec((SUBLANES, BRANCH_W), lambda b, c: (0, 0)),
    ]
    args += [mu, wwa, gup, vec]
    if has_state:
        in_specs.append(_layer_state_spec(layer, ngroup * nseq, tail))
        args.append(state)
    stack_specs, stack_args, stack_shape, aliases = _stack_io(stack, depth, nbatch * nseq, tail, len(args))
    sh_specs, sh_args, sh_shape, sh_aliases = _stack_io(shift_stack, depth, nbatch * nseq, shift_tail,
                                                        len(args) + len(stack_args), out_index=2)
    y_spec, y_shape = _y_io(nbatch, nchunk, ngroup, null_rows)
    y, new_stack, new_shift = pl.pallas_call(
        functools.partial(_rwkv_body, ngroup=ngroup, nseq=nseq, null_rows=null_rows, has_state=has_state,
                          has_stack=stack is not None),
        grid=(nbatch // ngroup, nchunk),
        in_specs=in_specs + stack_specs + sh_specs,
        out_specs=[y_spec, _layer_state_spec(layer, ngroup * nseq, tail),
                   _layer_state_spec(layer, ngroup * nseq, shift_tail)],
        out_shape=[y_shape, stack_shape, sh_shape],
        input_output_aliases={**aliases, **sh_aliases},
        scratch_shapes=[pltpu.VMEM((ngroup, CHUNK + SUBLANES, N_CC), F32),
                        pltpu.VMEM((ngroup * N_PAIR, 2 * N_C, 2 * N_C), F32)],
        compiler_params=_cparams(2),
        name="rwkv",
    )(*args, *stack_args, *sh_args)
    return y, (new_stack, new_shift)


def _prep_in(w_in):
    wt = jnp.transpose(w_in, (2, 0, 1))
    ab = QKV_A + H_A * DV_A
    w1 = _wprep(wt, 0, ab)
    w2 = _wprep(wt, N_A, N_B - GLA_RANK)
    w3 = _wprep(wt, N_A + N_B, N_CC, misc=(ab, 2 * H_A, N_A + N_B - GLA_RANK, GLA_RANK))
    wg = _wprep(wt, N_A + N_B + N_CC, N_BRANCH * D_MODEL)
    return w1, w2, w3, wg


def _prep_ffn(w_up, w_down):
    return w_up.astype(BF16), w_down.astype(BF16)


def kernel(x_prompt, x_sample, state_gdn, state_gdn_conv, state_gla, state_rwkv, state_rwkv_shift, w_in, conv_a, a_log, dt_bias, gdn_norm, gla_gk_up, gla_gk_bias, gla_norm, rwkv_mu, rwkv_w0, rwkv_w_up, rwkv_a0, rwkv_a_up, rwkv_g_up, rwkv_k_k, rwkv_k_a, rwkv_r_k, rwkv_ln_w, rwkv_ln_b, w_branch, w_out, norm_ff1, w_ff1_up, w_ff1_down, norm_mix, norm_ff2, w_ff2_up, w_ff2_down, norm_final):
    bp, tp, d = x_prompt.shape
    bs, ts, _ = x_sample.shape
    depth = w_in.shape[0]
    n_p, n_s = bp * tp, bs * ts
    null_rows = SAMPLE_ROWS - ts
    assert tp % CHUNK == 0 and null_rows == SAMPLE_ROWS // 2 and CONV_W - 1 <= null_rows
    seq_per_chunk = CHUNK // SAMPLE_ROWS
    assert bs % seq_per_chunk == 0
    nchunk_p = tp // CHUNK
    nstep_s = bs // seq_per_chunk
    prompt = dict(nbatch=bp, nchunk=nchunk_p, nseq=1, null_rows=0, ngroup=math.gcd(bp, PROMPT_GROUPS))
    sample = dict(nbatch=nstep_s, nchunk=1, nseq=seq_per_chunk, null_rows=null_rows,
                  ngroup=math.gcd(nstep_s, SAMPLE_GROUPS))

    x = jnp.concatenate([x_prompt.reshape(n_p, d), x_sample.reshape(n_s, d)], axis=0)
    gla_p = gla_s = None
    gdn_p = gdn_s = rwkv_p = rwkv_s = (None, None)
    ff1 = _prep_ffn(w_ff1_up, w_ff1_down)
    ff2 = _prep_ffn(w_ff2_up, w_ff2_down)
    w1, w2, w3, wg = _prep_in(w_in)
    wbr = w_branch.astype(BF16)
    wout = w_out.astype(BF16)
    for l in range(depth):
        h = _ffn(x, norm_ff1[l], *ff1, l)

        hn = _norm_cast(h, norm_mix[l], BF16)
        p1, p1s = _proj(hn, w1, l, w1.shape[1] // 2, n_sample=n_s)
        p2, p2s = _proj(hn, w2, l, w2.shape[1] // 2, n_sample=n_s)
        p3, p3s = _proj(hn, w3, l, w3.shape[1] // 2, n_sample=n_s)
        pg = _proj(hn, wg, l, wg.shape[1] // 4, BF16)

        prm = jnp.zeros((SUBLANES, LANES), F32)
        prm = prm.at[0, H_A:2 * H_A].set(a_log[l]).at[1, H_A:2 * H_A].set(dt_bias[l])
        gnorm = gdn_norm[l].reshape(1, DV_A)
        up_w = jnp.pad(gla_gk_up[l], ((0, LANES - GLA_RANK), (0, 0))).astype(BF16)
        gk_bias = gla_gk_bias[l].reshape(1, H_B * DK_B)
        lnorm = gla_norm[l].reshape(1, DV_B)
        mu = rwkv_mu[l].reshape(1, N_CC)
        wwa = jnp.zeros((LANES, 2 * BRANCH_W), F32)
        wwa = wwa.at[0:DECAY_LORA, 0:BRANCH_W].set(rwkv_w_up[l])
        wwa = wwa.at[DECAY_LORA:DECAY_LORA + AAA_LORA, BRANCH_W:].set(rwkv_a_up[l]).astype(BF16)
        gup = rwkv_g_up[l].astype(BF16)
        vec = jnp.stack([rwkv_w0[l], rwkv_a0[l], rwkv_k_k[l], rwkv_k_a[l], rwkv_r_k[l],
                         rwkv_ln_w[l], rwkv_ln_b[l], jnp.zeros((BRANCH_W,), F32)], axis=0)

        ya_p, gdn_p = _gdn(p1, p3, None, None, gdn_p, l, depth, conv_a[l], prm, gnorm, **prompt)
        ya_s, gdn_s = _gdn(p1s, p3s, state_gdn_conv, state_gdn, gdn_s, l, depth, conv_a[l], prm, gnorm, **sample)
        yb_p, gla_p = _gla(p2, p3, None, gla_p, l, depth, up_w, gk_bias, lnorm, **prompt)
        yb_s, gla_s = _gla(p2s, p3s, state_gla, gla_s, l, depth, up_w, gk_bias, lnorm, **sample)
        yc_p, rwkv_p = _rwkv(p3, None, None, rwkv_p, l, depth, mu, wwa, gup, vec, **prompt)
        yc_s, rwkv_s = _rwkv(p3s, state_rwkv_shift, state_rwkv, rwkv_s, l, depth, mu, wwa, gup, vec, **sample)

        h = _merge([y.reshape(n_p, BRANCH_W) for y in (ya_p, yb_p, yc_p)],
                   [y.reshape(n_s, BRANCH_W) for y in (ya_s, yb_s, yc_s)], pg, wbr, wout, h, l)
        x = _ffn(h, norm_ff2[l], *ff2, l)

    y_p = _norm_cast(x, norm_final, F32, 0, n_p)
    y_s = _norm_cast(x, norm_final, F32, n_p, n_s)
    return (y_p.reshape(bp, tp, d), y_s.reshape(bs, ts, d),
            gdn_p[0], gdn_s[0], gdn_p[1], gdn_s[1], gla_p, gla_s,
            rwkv_p[0], rwkv_s[0], rwkv_p[1], rwkv_s[1])
```

```python
import functools
import math

import jax
import jax.numpy as jnp
from jax import lax
from jax.experimental import pallas as pl
from jax.experimental.pallas import tpu as pltpu

F32 = jnp.float32
BF16 = jnp.bfloat16

D_MODEL = 2048
N_BRANCH = 3
BRANCH_W = 768
DK_A, DV_A, H_A, CONV_W = 128, 128, 6, 4
DK_B, DV_B, H_B, GLA_RANK, GLA_NORMALIZER = 64, 128, 6, 16, 16.0
N_C, H_C, DECAY_LORA, AAA_LORA, GATE_LORA = 64, 12, 64, 64, 128
GN_EPS = 64e-5
D_FF = 5504
NORM_EPS = 1e-6
QKV_A = 2 * H_A * DK_A + H_A * DV_A
N_A = QKV_A + H_A * DV_A + 2 * H_A
N_B = 2 * H_B * DK_B + 2 * H_B * DV_B + GLA_RANK
N_CC = 3 * BRANCH_W + DECAY_LORA + AAA_LORA + GATE_LORA
N_PAIR = H_C // 2

LANES = 128
SUBLANES = 8
CHUNK = 64
PROMPT_GROUPS = 4
SAMPLE_GROUPS = 2
SAMPLE_ROWS = 8
FF_TILE = 512
VMEM_LIMIT = 60 * 1024 * 1024


def _cparams(n_axes):
    return pltpu.CompilerParams(dimension_semantics=("arbitrary",) * n_axes,
                                vmem_limit_bytes=VMEM_LIMIT)


def _pick_tile(n, prefs):
    for t in prefs:
        if n % t == 0:
            return t
    raise ValueError(f"no tile for {n}")


def _bdot(a, b):
    return jnp.dot(a.astype(BF16), b.astype(BF16), preferred_element_type=F32)


def _bdot_nt(a, b):
    return lax.dot_general(a.astype(BF16), b.astype(BF16), (((1,), (1,)), ((), ())),
                           preferred_element_type=F32)


def _bdot_tn(a, b):
    return lax.dot_general(a.astype(BF16), b.astype(BF16), (((0,), (0,)), ((), ())),
                           preferred_element_type=F32)


def _sigmoid(x):
    return jax.nn.sigmoid(x)


def _silu(x):
    return x * _sigmoid(x)


def _softplus(x):
    return jnp.maximum(x, 0.0) + jnp.log(1.0 + jnp.exp(-jnp.abs(x)))


def _rmsnorm_rows(x, g):
    return x * lax.rsqrt(jnp.mean(x * x, axis=-1, keepdims=True) + NORM_EPS) * g


def _chunk_masks(rows, seq_len):
    sh = int(math.log2(seq_len))
    ri = lax.broadcasted_iota(jnp.int32, (rows, rows), 0)
    ci = lax.broadcasted_iota(jnp.int32, (rows, rows), 1)
    same = (ri >> sh) == (ci >> sh)
    return same & (ri >= ci), same & (ri > ci), same


def _eye(rows):
    return (lax.broadcasted_iota(jnp.int32, (rows, rows), 0)
            == lax.broadcasted_iota(jnp.int32, (rows, rows), 1)).astype(F32)


def _inv_unit_lower(lms, eye, nil):
    xs = [eye - lm for lm in lms]
    ps = list(lms)
    k = 2
    while k < nil:
        ps = [_bdot(p, p) for p in ps]
        xs = [x + _bdot(x, p) for x, p in zip(xs, ps)]
        k *= 2
    return xs


def _row_valid(rows, seq_len, null_rows, width):
    r = lax.broadcasted_iota(jnp.int32, (rows, width), 0)
    return (r & (seq_len - 1)) >= null_rows


def _seq_rows(rows, seq_len, s, width):
    r = lax.broadcasted_iota(jnp.int32, (rows, width), 0) & (CHUNK - 1)
    return (r >> int(math.log2(seq_len))) == s


def _pick(sel, xv):
    return xv if sel is None else jnp.where(sel, xv, 0.0)


def _ffn_body(x_ref, g_ref, wg_ref, wu_ref, wd_ref, o_ref, xn_ref, *, d_ff):
    j = pl.program_id(1)

    @pl.when(j == 0)
    def _():
        xn_ref[...] = _rmsnorm_rows(x_ref[...], g_ref[...]).astype(BF16)
        o_ref[...] = jnp.zeros_like(o_ref)

    dup = j * FF_TILE - _ff_start(j, d_ff)
    col_ok = lax.broadcasted_iota(jnp.int32, (1, FF_TILE), 1) >= dup
    row_ok = lax.broadcasted_iota(jnp.int32, (FF_TILE, 1), 0) >= dup
    xn = xn_ref[...]
    gate = jnp.dot(xn, wg_ref[0], preferred_element_type=F32)
    up = jnp.dot(xn, wu_ref[0], preferred_element_type=F32)
    act = jnp.where(col_ok, _silu(gate) * up, 0.0).astype(BF16)
    wd = jnp.where(row_ok, wd_ref[0], jnp.zeros((), BF16))
    o_ref[...] += jnp.dot(act, wd, preferred_element_type=F32)

    @pl.when(j == pl.num_programs(1) - 1)
    def _():
        o_ref[...] = x_ref[...] + 0.5 * o_ref[...]


def _ff_start(j, d_ff):
    return jnp.minimum(j * FF_TILE, d_ff - FF_TILE)


def _ffn(x, g, w_up, w_down, layer):
    n, d = x.shape
    tm = _pick_tile(n, (1088, 544, 512, 256, 128, 64, 32, 16, 8))
    d_ff = w_down.shape[1]
    assert d_ff % LANES == 0 and d_ff >= FF_TILE
    nf = pl.cdiv(d_ff, FF_TILE)

    def up_spec(col0):
        return pl.BlockSpec((pl.Element(1), pl.Element(d), pl.Element(FF_TILE)),
                            lambda i, j: (layer, 0, pl.multiple_of(col0 + _ff_start(j, d_ff), LANES)))

    return pl.pallas_call(
        functools.partial(_ffn_body, d_ff=d_ff),
        grid=(n // tm, nf),
        in_specs=[
            pl.BlockSpec((tm, d), lambda i, j: (i, 0)),
            pl.BlockSpec((1, d), lambda i, j: (0, 0)),
            up_spec(0),
            up_spec(d_ff),
            pl.BlockSpec((pl.Element(1), pl.Element(FF_TILE), pl.Element(d)),
                         lambda i, j: (layer, pl.multiple_of(_ff_start(j, d_ff), LANES), 0)),
        ],
        out_specs=pl.BlockSpec((tm, d), lambda i, j: (i, 0)),
        out_shape=jax.ShapeDtypeStruct((n, d), F32),
        scratch_shapes=[pltpu.VMEM((tm, d), BF16)],
        compiler_params=_cparams(2),
        name="ffn",
    )(x, g.reshape(1, d), w_up, w_up, w_down)


def _norm_cast_body(x_ref, g_ref, o_ref):
    o_ref[...] = _rmsnorm_rows(x_ref[...], g_ref[...]).astype(o_ref.dtype)


def _norm_cast(x, g, dtype, start=0, count=None):
    n, d = x.shape
    count = n - start if count is None else count
    tm = _pick_tile(math.gcd(start, count), (544, 512, 256, 128, 64, 32, 16, 8))
    first = start // tm
    return pl.pallas_call(
        _norm_cast_body,
        grid=(count // tm,),
        in_specs=[pl.BlockSpec((tm, d), lambda i: (first + i, 0)), pl.BlockSpec((1, d), lambda i: (0, 0))],
        out_specs=pl.BlockSpec((tm, d), lambda i: (i, 0)),
        out_shape=jax.ShapeDtypeStruct((count, d), dtype),
        compiler_params=_cparams(1),
        name="norm",
    )(x, g.reshape(1, d))


def _pad_rows(x):
    n, w = x.shape
    x3 = x.reshape(n // SUBLANES, SUBLANES, w)
    tail = lax.broadcasted_iota(jnp.int32, x3.shape, 1) >= SUBLANES // 2
    first = jnp.where(tail, pltpu.roll(x3, SUBLANES // 2, 1), 0.0)
    second = jnp.where(tail, x3, 0.0)
    return jnp.stack([first, second], axis=1).reshape(2 * n, w)


def _compact_rows(x):
    n, w = x.shape
    x4 = x.reshape(n // (2 * SUBLANES), 2, SUBLANES, w)
    head = lax.broadcasted_iota(jnp.int32, (n // (2 * SUBLANES), SUBLANES, w), 1) < SUBLANES // 2
    return jnp.where(head, pltpu.roll(x4[:, 0], SUBLANES // 2, 1), x4[:, 1]).reshape(n // 2, w)


WPREP_ROWS = 256


def _wprep_body(*refs, depth, n_main, misc):
    if misc:
        w_ref, ma_ref, mb_ref, o_ref = refs
    else:
        w_ref, o_ref = refs
    i = pl.program_id(0)

    @pl.when(i < n_main)
    def _():
        wb = w_ref[...].astype(BF16)
        for l in range(depth):
            o_ref[l] = wb[:, l, :]

    if misc:
        na, nb = misc

        @pl.when(i == n_main)
        def _():
            rows = lax.broadcasted_iota(jnp.int32, ma_ref.shape[0:1] + ma_ref.shape[2:3], 0)
            for l in range(depth):
                o_ref[l] = jnp.zeros(o_ref.shape[1:], BF16)
                o_ref[l, 0:ma_ref.shape[0], :] = jnp.where(rows < na, ma_ref[:, l, :], 0.0).astype(BF16)
                o_ref[l, LANES:LANES + mb_ref.shape[0], :] = jnp.where(rows < nb, mb_ref[:, l, :], 0.0).astype(BF16)


def _wprep(wt, row0, nrows, misc=None):
    n_all, depth, k = wt.shape
    assert nrows % WPREP_ROWS == 0
    n_main = nrows // WPREP_ROWS

    def win(rows, start_of):
        return pl.BlockSpec((pl.Element(rows), pl.Element(depth), pl.Element(k)),
                            lambda i: (start_of(i), 0, 0))

    in_specs = [win(WPREP_ROWS, lambda i: row0 + jnp.minimum(i, n_main - 1) * WPREP_ROWS)]
    args = [wt]
    body_misc = None
    if misc:
        row_a, n_a, row_b, n_b = misc
        small = 2 * SUBLANES
        assert n_a <= small and n_b <= small
        in_specs += [win(small, lambda i: row_a), win(small, lambda i: row_b)]
        args += [wt, wt]
        body_misc = (n_a, n_b)
    nblk = n_main + (1 if misc else 0)
    return pl.pallas_call(
        functools.partial(_wprep_body, depth=depth, n_main=n_main, misc=body_misc),
        grid=(nblk,),
        in_specs=in_specs,
        out_specs=pl.BlockSpec((depth, WPREP_ROWS, k), lambda i: (0, i, 0)),
        out_shape=jax.ShapeDtypeStruct((depth, nblk * WPREP_ROWS, k), BF16),
        compiler_params=_cparams(1),
        name="wprep",
    )(*args)


def _proj_body(x_ref, w_ref, o_ref, *pad_ref, sample_off, n_sample):
    res = lax.dot_general(x_ref[...], w_ref[...], (((1,), (1,)), ((), ())), preferred_element_type=F32)
    o_ref[...] = res.astype(o_ref.dtype)
    if pad_ref:
        @pl.when(pl.program_id(1) == pl.num_programs(1) - 1)
        def _():
            pad_ref[0][...] = _pad_rows(res[sample_off:sample_off + n_sample])


def _proj(xn, w, layer, tn, out_dtype=F32, n_sample=0):
    n, d = xn.shape
    nn = w.shape[1]
    tm = _pick_tile(n, (1088, 544, 512, 256, 128, 64, 32, 16))
    out_specs = [pl.BlockSpec((tm, tn), lambda j, i: (i, j))]
    out_shape = [jax.ShapeDtypeStruct((n, nn), out_dtype)]
    sample_off = tm - n_sample
    if n_sample:
        assert 0 <= sample_off and sample_off % SUBLANES == 0 and n_sample % SUBLANES == 0
        out_specs.append(pl.BlockSpec((2 * n_sample, tn), lambda j, i: (0, j)))
        out_shape.append(jax.ShapeDtypeStruct((2 * n_sample, nn), F32))
    out = pl.pallas_call(
        functools.partial(_proj_body, sample_off=sample_off, n_sample=n_sample),
        grid=(nn // tn, n // tm),
        in_specs=[
            pl.BlockSpec((tm, d), lambda j, i: (i, 0)),
            pl.BlockSpec((None, tn, d), lambda j, i: (layer, j, 0)),
        ],
        out_specs=out_specs,
        out_shape=out_shape,
        compiler_params=_cparams(2),
        name="proj",
    )(xn, w)
    return out if n_sample else out[0]


MERGE_TN = 2048


def _merge_body(yap_ref, ybp_ref, ycp_ref, yas_ref, ybs_ref, ycs_ref, ga_ref, gb_ref, gc_ref,
                wb_ref, wo_ref, h_ref, o_ref, *, prompt_tiles):
    i = pl.program_id(0)
    j = pl.program_id(1)

    @pl.when(j == 0)
    def _():
        o_ref[...] = h_ref[...]

    is_prompt = i < prompt_tiles
    ya = jnp.where(is_prompt, yap_ref[...], yas_ref[...])
    yb = jnp.where(is_prompt, ybp_ref[...], ybs_ref[...])
    yc = jnp.where(is_prompt, ycp_ref[...], ycs_ref[...])
    m = _sigmoid(ga_ref[...].astype(F32)) * jnp.dot(ya, wb_ref[0], preferred_element_type=F32)
    m += _sigmoid(gb_ref[...].astype(F32)) * jnp.dot(yb, wb_ref[1], preferred_element_type=F32)
    m += _sigmoid(gc_ref[...].astype(F32)) * jnp.dot(yc, wb_ref[2], preferred_element_type=F32)
    o_ref[...] += jnp.dot(m.astype(BF16), wo_ref[...], preferred_element_type=F32)


def _merge(ys_prompt, ys_sample, pg, wb, wo, h, layer):
    n, d = h.shape
    n_p, n_s = ys_prompt[0].shape[0], ys_sample[0].shape[0]
    tm = _pick_tile(math.gcd(n_p, n_s), (256, 128, 64, 32, 16))
    prompt_tiles = n_p // tm
    nj = d // MERGE_TN
    yp_spec = pl.BlockSpec((tm, BRANCH_W), lambda i, j: (jnp.minimum(i, prompt_tiles - 1), 0))
    ys_spec = pl.BlockSpec((tm, BRANCH_W), lambda i, j: (jnp.maximum(i - prompt_tiles, 0), 0))
    return pl.pallas_call(
        functools.partial(_merge_body, prompt_tiles=prompt_tiles),
        grid=(n // tm, nj),
        in_specs=[
            yp_spec, yp_spec, yp_spec, ys_spec, ys_spec, ys_spec,
            pl.BlockSpec((tm, MERGE_TN), lambda i, j: (i, j)),
            pl.BlockSpec((tm, MERGE_TN), lambda i, j: (i, nj + j)),
            pl.BlockSpec((tm, MERGE_TN), lambda i, j: (i, 2 * nj + j)),
            pl.BlockSpec((None, N_BRANCH, BRANCH_W, MERGE_TN), lambda i, j: (layer, 0, 0, j),
                         pipeline_mode=pl.Buffered(1)),
            pl.BlockSpec((None, MERGE_TN, d), lambda i, j: (layer, j, 0), pipeline_mode=pl.Buffered(1)),
            pl.BlockSpec((tm, d), lambda i, j: (i, 0)),
        ],
        out_specs=pl.BlockSpec((tm, d), lambda i, j: (i, 0)),
        out_shape=jax.ShapeDtypeStruct((n, d), F32),
        compiler_params=_cparams(2),
        name="merge",
    )(*ys_prompt, *ys_sample, pg, pg, pg, wb, wo, h)


def _layer_state_spec(layer, nseq, tail):
    zeros = (0,) * len(tail)
    return pl.BlockSpec((None, nseq) + tail, lambda b, c: (layer, b) + zeros)


def _stack_io(stack, depth, nstate, tail, n_args, out_index=1):
    shape = jax.ShapeDtypeStruct((depth, nstate) + tail, F32)
    if stack is None:
        return [], [], shape, {}
    return [pl.BlockSpec(memory_space=pl.ANY)], [stack], shape, {n_args: out_index}


def _split3_dot(lmat, x):
    lb = lmat.astype(BF16)
    hi = x.astype(BF16)
    r1 = x - hi.astype(F32)
    mid = r1.astype(BF16)
    lo = (r1 - mid.astype(F32)).astype(BF16)
    return (jnp.dot(lb, hi, preferred_element_type=F32) + jnp.dot(lb, mid, preferred_element_type=F32)
            + jnp.dot(lb, lo, preferred_element_type=F32))


def _gdn_body(*refs, ngroup, nseq, null_rows, has_state, has_stack):
    refs = list(refs)
    xbuf = refs.pop()
    ct_ref = refs.pop()
    so_ref = refs.pop()
    y_ref = refs.pop()
    if has_stack:
        del refs[-2:]
    row_refs = [refs[3 * g:3 * g + 3] for g in range(ngroup)]
    refs = refs[3 * ngroup:]
    if has_state:
        cs_ref, cw_ref, prm_ref, nw_ref, s0_ref = refs
    else:
        cw_ref, prm_ref, nw_ref = refs
        s0_ref = so_ref
    rows = CHUNK
    seq_len = rows // nseq
    c = pl.program_id(1)
    chains = [(g, h) for g in range(ngroup) for h in range(H_A)]

    @pl.when(c == 0)
    def _():
        for g in range(ngroup):
            xbuf[g, 0:SUBLANES, :] = jnp.zeros((SUBLANES, QKV_A), F32)
        if not has_state:
            so_ref[...] = jnp.zeros_like(so_ref)

    @pl.when(c > 0)
    def _():
        for g in range(ngroup):
            xbuf[g, 0:SUBLANES, :] = xbuf[g, rows:rows + SUBLANES, :]

    for g in range(ngroup):
        xbuf[g, SUBLANES:SUBLANES + rows, :] = row_refs[g][0][...]
        if has_state:
            for s in range(nseq):
                r0 = SUBLANES + s * seq_len + null_rows - (CONV_W - 1)
                xbuf[g, r0:r0 + CONV_W - 1, :] = cs_ref[g * nseq + s]

    @pl.when(c == pl.num_programs(1) - 1)
    def _():
        for g in range(ngroup):
            for s in range(nseq):
                r1 = SUBLANES + (s + 1) * seq_len
                ct_ref[g * nseq + s] = xbuf[g, r1 - (CONV_W - 1):r1, :]

    incl, strict, same = _chunk_masks(rows, seq_len)
    eye = _eye(rows)
    valid = _row_valid(rows, seq_len, null_rows, LANES) if null_rows else None
    sels = [_seq_rows(rows, seq_len, s, LANES) for s in range(nseq)] if nseq > 1 else [None]
    sels2 = [_seq_rows(2 * rows, seq_len, s, LANES) for s in range(nseq)] if nseq > 1 else [None]
    lstack = jnp.concatenate([incl.astype(F32), same.astype(F32)], axis=0)

    def conv_cols(g, c0):
        acc = xbuf[g, SUBLANES - 3:SUBLANES - 3 + rows, c0:c0 + LANES] * cw_ref[0:1, c0:c0 + LANES]
        for j in range(1, CONV_W):
            acc += (xbuf[g, SUBLANES - 3 + j:SUBLANES - 3 + j + rows, c0:c0 + LANES]
                    * cw_ref[j:j + 1, c0:c0 + LANES])
        return _silu(acc)

    qs, ks, kbs, rhs, decays, wq_parts, kds, gtcs = [], [], [], [], [], [], [], []
    for g in range(ngroup):
        ba = row_refs[g][2][...]
        beta_all = _sigmoid(ba)
        g_all = -jnp.exp(prm_ref[0:1, :]) * _softplus(ba + prm_ref[1:2, :])
        if null_rows:
            g_all = jnp.where(valid, g_all, 0.0)
        cs = _split3_dot(lstack, g_all)
        gc_all, gt_all = cs[0:rows], cs[rows:2 * rows]
        gc_t = gc_all.T
        for h in range(H_A):
            q = conv_cols(g, h * DK_A)
            k = conv_cols(g, H_A * DK_A + h * DK_A)
            v = conv_cols(g, 2 * H_A * DK_A + h * DV_A)
            q = q * lax.rsqrt(jnp.sum(q * q, axis=-1, keepdims=True) + 1e-6) * DK_A ** -0.5
            k = k * lax.rsqrt(jnp.sum(k * k, axis=-1, keepdims=True) + 1e-6)
            if null_rows:
                q = jnp.where(valid, q, 0.0)
                k = jnp.where(valid, k, 0.0)
                v = jnp.where(valid, v, 0.0)
            beta = beta_all[:, h:h + 1]
            gcc = gc_all[:, H_A + h:H_A + h + 1]
            gtc = gt_all[:, H_A + h:H_A + h + 1]
            gcr = gc_t[H_A + h:H_A + h + 1, :]
            egc = jnp.exp(gcc)
            kb = k * beta
            qs.append(q)
            ks.append(k)
            kbs.append(kb)
            rhs.append(jnp.concatenate([v * beta, kb * egc], axis=1))
            decays.append(jnp.where(incl, jnp.exp(jnp.where(incl, gcc - gcr, 0.0)), 0.0))
            wq_parts.append(q * egc)
            kds.append(k * jnp.exp(gtc - gcc))
            gtcs.append(gtc)

    n = range(len(chains))
    lms = [jnp.where(strict, _bdot_nt(kbs[i], ks[i]) * decays[i], 0.0) for i in n]
    attns = [_bdot_nt(qs[i], ks[i]) * decays[i] for i in n]
    tmats = _inv_unit_lower(lms, eye, seq_len)
    uws = [_bdot(tmats[i], rhs[i]) for i in n]
    us, os_ = [], []
    for i, (g, h) in enumerate(chains):
        u = uws[i][:, 0:DV_A]
        wq = jnp.concatenate([uws[i][:, DV_A:DV_A + DK_A], wq_parts[i]], axis=0)
        o = None
        for s in range(nseq):
            both = _bdot(_pick(sels2[s], wq), s0_ref[g * nseq + s, h])
            u = u - both[0:rows]
            o = both[rows:2 * rows] if o is None else o + both[rows:2 * rows]
        us.append(u)
        os_.append(o)
    os_ = [os_[i] + _bdot(attns[i], us[i]) for i in n]
    for i, (g, h) in enumerate(chains):
        for s in range(nseq):
            g_last = jnp.exp(gtcs[i][s * seq_len:s * seq_len + 1, :])
            so_ref[g * nseq + s, h] = (g_last * s0_ref[g * nseq + s, h]
                                       + _bdot_tn(_pick(sels[s], kds[i]), us[i]))
    for i, (g, h) in enumerate(chains):
        z = row_refs[g][1][:, h * DV_A:(h + 1) * DV_A]
        _store_y(y_ref, g, h * DV_A, _rmsnorm_rows(os_[i], nw_ref[...]) * _silu(z), null_rows)


def _row_spec(width, col, g, ngroup, nchunk):
    return pl.BlockSpec((CHUNK, width), lambda b, c: ((b * ngroup + g) * nchunk + c, col))


def _y_io(nbatch, nchunk, ngroup, null_rows):
    out_rows = CHUNK // 2 if null_rows else CHUNK
    return (pl.BlockSpec((ngroup, out_rows, BRANCH_W), lambda b, c: (b, c, 0)),
            jax.ShapeDtypeStruct((nbatch, nchunk * out_rows, BRANCH_W), BF16))


def _store_y(y_ref, g, c0, yv, null_rows):
    if null_rows:
        yv = _compact_rows(yv)
    y_ref[g, :, c0:c0 + yv.shape[1]] = yv.astype(y_ref.dtype)


def _gdn(p1, p3, conv_state, state, stacks, layer, depth, conv_w, prm, norm_w, *,
         nbatch, nchunk, nseq, null_rows, ngroup):
    has_state = state is not None
    stack, conv_stack = stacks
    tail = (H_A, DK_A, DV_A)
    conv_tail = (CONV_W - 1, QKV_A)
    in_specs, args = [], []
    for g in range(ngroup):
        in_specs += [_row_spec(QKV_A, 0, g, ngroup, nchunk),
                     _row_spec(BRANCH_W, QKV_A // BRANCH_W, g, ngroup, nchunk),
                     _row_spec(LANES, N_CC // LANES, g, ngroup, nchunk)]
        args += [p1, p1, p3]
    if has_state:
        in_specs.append(_layer_state_spec(layer, ngroup * nseq, (CONV_W - 1, QKV_A)))
        args.append(conv_state)
    in_specs += [
        pl.BlockSpec((CONV_W, QKV_A), lambda b, c: (0, 0)),
        pl.BlockSpec((SUBLANES, LANES), lambda b, c: (0, 0)),
        pl.BlockSpec((1, DV_A), lambda b, c: (0, 0)),
    ]
    args += [conv_w, prm, norm_w]
    if has_state:
        in_specs.append(_layer_state_spec(layer, ngroup * nseq, tail))
        args.append(state)
    stack_specs, stack_args, stack_shape, aliases = _stack_io(stack, depth, nbatch * nseq, tail, len(args))
    cs_specs, cs_args, cs_shape, cs_aliases = _stack_io(conv_stack, depth, nbatch * nseq, conv_tail,
                                                        len(args) + len(stack_args), out_index=2)
    y_spec, y_shape = _y_io(nbatch, nchunk, ngroup, null_rows)
    y, new_stack, new_conv = pl.pallas_call(
        functools.partial(_gdn_body, ngroup=ngroup, nseq=nseq, null_rows=null_rows, has_state=has_state,
                          has_stack=stack is not None),
        grid=(nbatch // ngroup, nchunk),
        in_specs=in_specs + stack_specs + cs_specs,
        out_specs=[y_spec, _layer_state_spec(layer, ngroup * nseq, tail),
                   _layer_state_spec(layer, ngroup * nseq, conv_tail)],
        out_shape=[y_shape, stack_shape, cs_shape],
        input_output_aliases={**aliases, **cs_aliases},
        scratch_shapes=[pltpu.VMEM((ngroup, CHUNK + SUBLANES, QKV_A), F32)],
        compiler_params=_cparams(2),
        name="gdn",
    )(*args, *stack_args, *cs_args)
    return y, (new_stack, new_conv)


def _gla_body(*refs, ngroup, nseq, null_rows, has_state, has_stack):
    refs = list(refs)
    so_ref = refs.pop()
    y_ref = refs.pop()
    if has_stack:
        refs.pop()
    row_refs = [refs[5 * g:5 * g + 5] for g in range(ngroup)]
    refs = refs[5 * ngroup:]
    if has_state:
        up_ref, bias_ref, nw_ref, s0_ref = refs
    else:
        up_ref, bias_ref, nw_ref = refs
        s0_ref = so_ref
    rows = CHUNK
    seq_len = rows // nseq
    c = pl.program_id(1)
    chains = [(g, h) for g in range(ngroup) for h in range(H_B)]
    if not has_state:
        @pl.when(c == 0)
        def _():
            so_ref[...] = jnp.zeros_like(so_ref)

    incl, _, same = _chunk_masks(rows, seq_len)
    ci = lax.broadcasted_iota(jnp.int32, (rows, rows), 1)
    first_half = same & ((ci & (seq_len - 1)) < seq_len // 2)
    wide = H_B * DK_B
    lstack = jnp.concatenate([incl.astype(F32), same.astype(F32), first_half.astype(F32)], axis=0)
    valid_w = _row_valid(rows, seq_len, null_rows, wide) if null_rows else None
    valid_k = _row_valid(rows, seq_len, null_rows, DK_B) if null_rows else None
    valid_v = _row_valid(rows, seq_len, null_rows, DV_B) if null_rows else None
    sels = [_seq_rows(rows, seq_len, s, DK_B) for s in range(nseq)] if nseq > 1 else [None]

    qis, kis, vs, q_ins, k_decs, a_lasts = [], [], [], [], [], []
    for g in range(ngroup):
        q_ref, k_ref, v_ref, _, gkl_ref = row_refs[g]
        x = _bdot(gkl_ref[...], up_ref[...]) + bias_ref[...]
        gk_all = (jnp.minimum(x, 0.0) - jnp.log(1.0 + jnp.exp(-jnp.abs(x)))) / GLA_NORMALIZER
        if null_rows:
            gk_all = jnp.where(valid_w, gk_all, 0.0)
        cs = _split3_dot(lstack, gk_all)
        bc_all, bt_all, an_all = cs[0:rows], cs[rows:2 * rows], cs[2 * rows:3 * rows]
        bt_t = bt_all.T
        for h in range(H_B):
            c0 = h * DK_B
            q = q_ref[:, c0:c0 + DK_B] * DK_B ** -0.5
            k = k_ref[:, c0:c0 + DK_B]
            v = v_ref[:, h * DV_B:(h + 1) * DV_B]
            if null_rows:
                q = jnp.where(valid_k, q, 0.0)
                k = jnp.where(valid_k, k, 0.0)
                v = jnp.where(valid_v, v, 0.0)
            bc = bc_all[:, c0:c0 + DK_B]
            bt = bt_all[:, c0:c0 + DK_B]
            an = an_all[:, c0:c0 + DK_B]
            qis.append(q * jnp.exp(bc - an))
            kis.append(k * jnp.exp(an - bc))
            vs.append(v)
            q_ins.append(q * jnp.exp(bc))
            k_decs.append(k * jnp.exp(bt - bc))
            a_lasts.append([jnp.exp(bt_t[c0:c0 + DK_B, s * seq_len:s * seq_len + 1])
                            for s in range(nseq)])

    n = range(len(chains))
    attns = [jnp.where(incl, _bdot_nt(qis[i], kis[i]), 0.0) for i in n]
    os_ = [_bdot(attns[i], vs[i]) for i in n]
    for i, (g, h) in enumerate(chains):
        for s in range(nseq):
            os_[i] = os_[i] + _bdot(_pick(sels[s], q_ins[i]), s0_ref[g * nseq + s, h])
    for i, (g, h) in enumerate(chains):
        for s in range(nseq):
            so_ref[g * nseq + s, h] = (a_lasts[i][s] * s0_ref[g * nseq + s, h]
                                       + _bdot_tn(_pick(sels[s], k_decs[i]), vs[i]))
    for i, (g, h) in enumerate(chains):
        gate = row_refs[g][3][:, h * DV_B:(h + 1) * DV_B]
        _store_y(y_ref, g, h * DV_B, _rmsnorm_rows(os_[i], nw_ref[...]) * _silu(gate), null_rows)


def _gla(p2, p3, state, stack, layer, depth, up_w, bias, norm_w, *, nbatch, nchunk, nseq, null_rows, ngroup):
    has_state = state is not None
    wide = H_B * DK_B
    tail = (H_B, DK_B, DV_B)
    in_specs, args = [], []
    for g in range(ngroup):
        in_specs += [_row_spec(wide, 0, g, ngroup, nchunk), _row_spec(wide, 1, g, ngroup, nchunk),
                     _row_spec(BRANCH_W, 1, g, ngroup, nchunk), _row_spec(BRANCH_W, 2, g, ngroup, nchunk)]
        in_specs.append(_row_spec(LANES, N_CC // LANES + 1, g, ngroup, nchunk))
        args += [p2, p2, p2, p2, p3]
    in_specs += [pl.BlockSpec((LANES, wide), lambda b, c: (0, 0)),
                 pl.BlockSpec((1, wide), lambda b, c: (0, 0)),
                 pl.BlockSpec((1, DV_B), lambda b, c: (0, 0))]
    args += [up_w, bias, norm_w]
    if has_state:
        in_specs.append(_layer_state_spec(layer, ngroup * nseq, tail))
        args.append(state)
    stack_specs, stack_args, stack_shape, aliases = _stack_io(stack, depth, nbatch * nseq, tail, len(args))
    y_spec, y_shape = _y_io(nbatch, nchunk, ngroup, null_rows)
    return pl.pallas_call(
        functools.partial(_gla_body, ngroup=ngroup, nseq=nseq, null_rows=null_rows, has_state=has_state,
                          has_stack=stack is not None),
        grid=(nbatch // ngroup, nchunk),
        in_specs=in_specs + stack_specs,
        out_specs=[y_spec, _layer_state_spec(layer, ngroup * nseq, tail)],
        out_shape=[y_shape, stack_shape],
        input_output_aliases=aliases,
        compiler_params=_cparams(2),
        name="gla",
    )(*args, *stack_args)


def _rwkv_body(*refs, ngroup, nseq, null_rows, has_state, has_stack):
    refs = list(refs)
    sp_ref = refs.pop()
    pbuf = refs.pop()
    st_ref = refs.pop()
    so_ref = refs.pop()
    y_ref = refs.pop()
    if has_stack:
        del refs[-2:]
    pc_refs = refs[:ngroup]
    refs = refs[ngroup:]
    if has_state:
        sh_ref, mu_ref, wwa_ref, gup_ref, vec_ref, s0_ref = refs
    else:
        mu_ref, wwa_ref, gup_ref, vec_ref = refs
    rows = CHUNK
    seq_len = rows // nseq
    c = pl.program_id(1)
    chains = [(g, p) for g in range(ngroup) for p in range(N_PAIR)]
    pairs = range(len(chains))

    @pl.when(c == 0)
    def _():
        for g in range(ngroup):
            pbuf[g, 0:SUBLANES, :] = jnp.zeros((SUBLANES, N_CC), F32)
        if not has_state:
            sp_ref[...] = jnp.zeros_like(sp_ref)

    @pl.when(c > 0)
    def _():
        for g in range(ngroup):
            pbuf[g, 0:SUBLANES, :] = pbuf[g, rows:rows + SUBLANES, :]

    for g in range(ngroup):
        pbuf[g, SUBLANES:SUBLANES + rows, :] = pc_refs[g][...]
        if has_state:
            for s in range(nseq):
                r0 = SUBLANES + s * seq_len + null_rows - 1
                pbuf[g, r0:r0 + 1, :] = sh_ref[g * nseq + s]

    @pl.when(c == pl.num_programs(1) - 1)
    def _():
        for g in range(ngroup):
            for s in range(nseq):
                r1 = SUBLANES + (s + 1) * seq_len
                st_ref[g * nseq + s] = pbuf[g, r1 - 1:r1, :]

    def xc_cols(g, c0, w):
        cur = pbuf[g, SUBLANES:SUBLANES + rows, c0:c0 + w]
        prev = pbuf[g, SUBLANES - 1:SUBLANES - 1 + rows, c0:c0 + w]
        return cur + (prev - cur) * mu_ref[:, c0:c0 + w]

    incl, strict, same = _chunk_masks(rows, seq_len)
    eye = _eye(rows)
    lane = lax.broadcasted_iota(jnp.int32, (rows, LANES), 1)
    lo = lane < N_C
    lo2 = lax.broadcasted_iota(jnp.int32, (2 * rows, LANES), 1) < N_C
    valid = _row_valid(rows, seq_len, null_rows, LANES) if null_rows else None
    valid_w = _row_valid(rows, seq_len, null_rows, BRANCH_W) if null_rows else None
    sels2 = [_seq_rows(2 * rows, seq_len, s, LANES) for s in range(nseq)] if nseq > 1 else [None]
    lstack = jnp.concatenate([incl.astype(F32), same.astype(F32)], axis=0)
    w0 = vec_ref[0:1, :]
    a0 = vec_ref[1:2, :]

    r2 = lax.broadcasted_iota(jnp.int32, (2 * N_C, LANES), 0)
    c2 = lax.broadcasted_iota(jnp.int32, (2 * N_C, LANES), 1)
    blockdiag = (r2 >= N_C) == (c2 >= N_C)

    def seg_sum(xv):
        s_lo = jnp.sum(jnp.where(lo, xv, 0.0), axis=-1, keepdims=True)
        s_hi = jnp.sum(jnp.where(lo, 0.0, xv), axis=-1, keepdims=True)
        return jnp.where(lo, s_lo, s_hi)

    def by_head(stacked):
        return jnp.where(lo, stacked[0:rows], stacked[rows:2 * rows])

    rs, k2s, vs, ar_stacks, b_ts, k_ts, tots, gates = [], [], [], [], [], [], [], []
    for g in range(ngroup):
        lora = xc_cols(g, 3 * BRANCH_W, DECAY_LORA + AAA_LORA + GATE_LORA)
        wa_in = lora[:, 0:LANES]
        wa_in = jnp.where(lo, jnp.tanh(wa_in), wa_in)
        wa = _bdot(wa_in, wwa_ref[...])
        g_all = _bdot(_sigmoid(lora[:, LANES:2 * LANES]), gup_ref[...])
        w_log = -_softplus(-(w0 + wa[:, 0:BRANCH_W])) - 0.5
        lw_all = -jnp.exp(w_log)
        a_all = _sigmoid(a0 + wa[:, BRANCH_W:2 * BRANCH_W])
        if null_rows:
            lw_all = jnp.where(valid_w, lw_all, 0.0)
        cs = _split3_dot(lstack, lw_all)
        cum_all, tot_all = cs[0:rows], cs[rows:2 * rows]
        for p in range(N_PAIR):
            c0 = p * LANES
            r = xc_cols(g, c0, LANES)
            k = xc_cols(g, BRANCH_W + c0, LANES)
            v = xc_cols(g, 2 * BRANCH_W + c0, LANES)
            a_p = a_all[:, c0:c0 + LANES]
            kkr = k * vec_ref[2:3, c0:c0 + LANES]
            kk = kkr * lax.rsqrt(seg_sum(kkr * kkr) + 1e-6)
            k2 = k * (1.0 + (a_p - 1.0) * vec_ref[3:4, c0:c0 + LANES])
            av = -kk
            bv = kk * a_p
            if null_rows:
                r = jnp.where(valid, r, 0.0)
                k2 = jnp.where(valid, k2, 0.0)
                v = jnp.where(valid, v, 0.0)
                av = jnp.where(valid, av, 0.0)
                bv = jnp.where(valid, bv, 0.0)
            cum = cum_all[:, c0:c0 + LANES]
            lw = lw_all[:, c0:c0 + LANES]
            g_inv = jnp.exp(-cum)
            rs.append(r)
            k2s.append(k2)
            vs.append(v)
            ar_stacks.append(jnp.concatenate([av * jnp.exp(cum - lw), r * jnp.exp(cum)], axis=0))
            b_ts.append(bv * g_inv)
            k_ts.append(k2 * g_inv)
            tots.append(tot_all[:, c0:c0 + LANES])
            gates.append(g_all[:, c0:c0 + LANES])

    a_abs, a_rbs, a_aks, a_rks = [], [], [], []
    for p in pairs:
        for hh in range(2):
            lhs = jnp.where(lo2 if hh == 0 else jnp.logical_not(lo2), ar_stacks[p], 0.0)
            mb = _bdot_nt(lhs, b_ts[p])
            mk = _bdot_nt(lhs, k_ts[p])
            a_abs.append(jnp.where(strict, mb[0:rows], 0.0))
            a_rbs.append(jnp.where(incl, mb[rows:2 * rows], 0.0))
            a_aks.append(jnp.where(strict, mk[0:rows], 0.0))
            a_rks.append(jnp.where(incl, mk[rows:2 * rows], 0.0))
    tmats = _inv_unit_lower([-m for m in a_abs], eye, seq_len)

    def pair_stack(mats, p):
        return jnp.concatenate([mats[2 * p], mats[2 * p + 1]], axis=0)

    states, xss = [], []
    for i, (g, p) in enumerate(chains):
        sps, xs = [], None
        for s in range(nseq):
            if has_state:
                sq = g * nseq + s
                sv = jnp.concatenate([s0_ref[sq, 2 * p], s0_ref[sq, 2 * p + 1]], axis=0)
                sp = jnp.where(blockdiag, jnp.concatenate([sv, sv], axis=1), 0.0)
            else:
                sp = sp_ref[i]
            both = _bdot_nt(_pick(sels2[s], ar_stacks[i]), sp)
            xs = both if xs is None else xs + both
            sps.append(sp)
        states.append(sps)
        xss.append(xs)
    yvs = [xss[i][0:rows] + by_head(_bdot(pair_stack(a_aks, i), vs[i])) for i in pairs]
    us = [by_head(_bdot(pair_stack(tmats, i), yvs[i])) for i in pairs]
    os_ = [xss[i][rows:2 * rows]
           + by_head(_bdot(pair_stack(a_rbs, i), us[i]) + _bdot(pair_stack(a_rks, i), vs[i]))
           for i in pairs]
    for i, (g, p) in enumerate(chains):
        uv = jnp.concatenate([us[i], vs[i]], axis=0)
        bk = jnp.concatenate([b_ts[i], k_ts[i]], axis=0)
        for s in range(nseq):
            upd = jnp.where(blockdiag, _bdot_tn(_pick(sels2[s], uv), bk), 0.0)
            sp_new = (states[i][s] + upd) * jnp.exp(tots[i][s * seq_len:s * seq_len + 1, :])
            if has_state:
                sq = g * nseq + s
                so_ref[sq, 2 * p] = sp_new[0:N_C, 0:N_C]
                so_ref[sq, 2 * p + 1] = pltpu.roll(sp_new[N_C:2 * N_C, :], N_C, 1)[:, 0:N_C]
            else:
                sp_ref[i] = sp_new

    for i, (g, p) in enumerate(chains):
        c0 = p * LANES
        o = os_[i]
        mean = seg_sum(o) * (1.0 / N_C)
        d = o - mean
        var = seg_sum(d * d) * (1.0 / N_C)
        on = d * lax.rsqrt(var + GN_EPS) * vec_ref[5:6, c0:c0 + LANES] + vec_ref[6:7, c0:c0 + LANES]
        bonus = seg_sum(rs[i] * k2s[i] * vec_ref[4:5, c0:c0 + LANES]) * vs[i]
        _store_y(y_ref, g, c0, (on + bonus) * gates[i], null_rows)

    if not has_state:
        @pl.when(c == pl.num_programs(1) - 1)
        def _():
            for i, (g, p) in enumerate(chains):
                sp = sp_ref[i]
                so_ref[g, 2 * p] = sp[0:N_C, 0:N_C]
                so_ref[g, 2 * p + 1] = pltpu.roll(sp[N_C:2 * N_C, :], N_C, 1)[:, 0:N_C]


def _rwkv(p3, shift_state, state, stacks, layer, depth, mu, wwa, gup, vec, *,
          nbatch, nchunk, nseq, null_rows, ngroup):
    has_state = state is not None
    stack, shift_stack = stacks
    tail = (H_C, N_C, N_C)
    shift_tail = (1, N_CC)
    in_specs = [_row_spec(N_CC, 0, g, ngroup, nchunk) for g in range(ngroup)]
    args = [p3] * ngroup
    if has_state:
        in_specs.append(_layer_state_spec(layer, ngroup * nseq, (1, N_CC)))
        args.append(shift_state)
    in_specs += [
        pl.BlockSpec((1, N_CC), lambda b, c: (0, 0)),
        pl.BlockSpec((LANES, 2 * BRANCH_W), lambda b, c: (0, 0)),
        pl.BlockSpec((GATE_LORA, BRANCH_W), lambda b, c: (0, 0)),
        pl.BlockSpec((SUBLANES, BRANCH_W), lambda b, c: (0, 0)),
    ]
    args += [mu, wwa, gup, vec]
    if has_state:
        in_specs.append(_layer_state_spec(layer, ngroup * nseq, tail))
        args.append(state)
    stack_specs, stack_args, stack_shape, aliases = _stack_io(stack, depth, nbatch * nseq, tail, len(args))
    sh_specs, sh_args, sh_shape, sh_aliases = _stack_io(shift_stack, depth, nbatch * nseq, shift_tail,
                                                        len(args) + len(stack_args), out_index=2)
    y_spec, y_shape = _y_io(nbatch, nchunk, ngroup, null_rows)
    y, new_stack, new_shift = pl.pallas_call(
        functools.partial(_rwkv_body, ngroup=ngroup, nseq=nseq, null_rows=null_rows, has_state=has_state,
                          has_stack=stack is not None),
        grid=(nbatch // ngroup, nchunk),
        in_specs=in_specs + stack_specs + sh_specs,
        out_specs=[y_spec, _layer_state_spec(layer, ngroup * nseq, tail),
                   _layer_state_spec(layer, ngroup * nseq, shift_tail)],
        out_shape=[y_shape, stack_shape, sh_shape],
        input_output_aliases={**aliases, **sh_aliases},
        scratch_shapes=[pltpu.VMEM((ngroup, CHUNK + SUBLANES, N_CC), F32),
                        pltpu.VMEM((ngroup * N_PAIR, 2 * N_C, 2 * N_C), F32)],
        compiler_params=_cparams(2),
        name="rwkv",
    )(*args, *stack_args, *sh_args)
    return y, (new_stack, new_shift)


def _prep_in(w_in):
    wt = jnp.transpose(w_in, (2, 0, 1))
    ab = QKV_A + H_A * DV_A
    w1 = _wprep(wt, 0, ab)
    w2 = _wprep(wt, N_A, N_B - GLA_RANK)
    w3 = _wprep(wt, N_A + N_B, N_CC, misc=(ab, 2 * H_A, N_A + N_B - GLA_RANK, GLA_RANK))
    wg = _wprep(wt, N_A + N_B + N_CC, N_BRANCH * D_MODEL)
    return w1, w2, w3, wg


def _prep_ffn(w_up, w_down):
    return w_up.astype(BF16), w_down.astype(BF16)


def kernel(x_prompt, x_sample, state_gdn, state_gdn_conv, state_gla, state_rwkv, state_rwkv_shift, w_in, conv_a, a_log, dt_bias, gdn_norm, gla_gk_up, gla_gk_bias, gla_norm, rwkv_mu, rwkv_w0, rwkv_w_up, rwkv_a0, rwkv_a_up, rwkv_g_up, rwkv_k_k, rwkv_k_a, rwkv_r_k, rwkv_ln_w, rwkv_ln_b, w_branch, w_out, norm_ff1, w_ff1_up, w_ff1_down, norm_mix, norm_ff2, w_ff2_up, w_ff2_down, norm_final):
    bp, tp, d = x_prompt.shape
    bs, ts, _ = x_sample.shape
    depth = w_in.shape[0]
    n_p, n_s = bp * tp, bs * ts
    null_rows = SAMPLE_ROWS - ts
    assert tp % CHUNK == 0 and null_rows == SAMPLE_ROWS // 2 and CONV_W - 1 <= null_rows
    seq_per_chunk = CHUNK // SAMPLE_ROWS
    assert bs % seq_per_chunk == 0
    nchunk_p = tp // CHUNK
    nstep_s = bs // seq_per_chunk
    prompt = dict(nbatch=bp, nchunk=nchunk_p, nseq=1, null_rows=0, ngroup=math.gcd(bp, PROMPT_GROUPS))
    sample = dict(nbatch=nstep_s, nchunk=1, nseq=seq_per_chunk, null_rows=null_rows,
                  ngroup=math.gcd(nstep_s, SAMPLE_GROUPS))

    x = jnp.concatenate([x_prompt.reshape(n_p, d), x_sample.reshape(n_s, d)], axis=0)
    gla_p = gla_s = None
    gdn_p = gdn_s = rwkv_p = rwkv_s = (None, None)
    ff1 = _prep_ffn(w_ff1_up, w_ff1_down)
    ff2 = _prep_ffn(w_ff2_up, w_ff2_down)
    w1, w2, w3, wg = _prep_in(w_in)
    wbr = w_branch.astype(BF16)
    wout = w_out.astype(BF16)
    for l in range(depth):
        h = _ffn(x, norm_ff1[l], *ff1, l)

        hn = _norm_cast(h, norm_mix[l], BF16)
        p1, p1s = _proj(hn, w1, l, w1.shape[1] // 2, n_sample=n_s)
        p2, p2s = _proj(hn, w2, l, w2.shape[1] // 2, n_sample=n_s)
        p3, p3s = _proj(hn, w3, l, w3.shape[1] // 2, n_sample=n_s)
        pg = _proj(hn, wg, l, wg.shape[1] // 3, BF16)

        prm = jnp.zeros((SUBLANES, LANES), F32)
        prm = prm.at[0, H_A:2 * H_A].set(a_log[l]).at[1, H_A:2 * H_A].set(dt_bias[l])
        gnorm = gdn_norm[l].reshape(1, DV_A)
        up_w = jnp.pad(gla_gk_up[l], ((0, LANES - GLA_RANK), (0, 0))).astype(BF16)
        gk_bias = gla_gk_bias[l].reshape(1, H_B * DK_B)
        lnorm = gla_norm[l].reshape(1, DV_B)
        mu = rwkv_mu[l].reshape(1, N_CC)
        wwa = jnp.zeros((LANES, 2 * BRANCH_W), F32)
        wwa = wwa.at[0:DECAY_LORA, 0:BRANCH_W].set(rwkv_w_up[l])
        wwa = wwa.at[DECAY_LORA:DECAY_LORA + AAA_LORA, BRANCH_W:].set(rwkv_a_up[l]).astype(BF16)
        gup = rwkv_g_up[l].astype(BF16)
        vec = jnp.stack([rwkv_w0[l], rwkv_a0[l], rwkv_k_k[l], rwkv_k_a[l], rwkv_r_k[l],
                         rwkv_ln_w[l], rwkv_ln_b[l], jnp.zeros((BRANCH_W,), F32)], axis=0)

        ya_p, gdn_p = _gdn(p1, p3, None, None, gdn_p, l, depth, conv_a[l], prm, gnorm, **prompt)
        ya_s, gdn_s = _gdn(p1s, p3s, state_gdn_conv, state_gdn, gdn_s, l, depth, conv_a[l], prm, gnorm, **sample)
        yb_p, gla_p = _gla(p2, p3, None, gla_p, l, depth, up_w, gk_bias, lnorm, **prompt)
        yb_s, gla_s = _gla(p2s, p3s, state_gla, gla_s, l, depth, up_w, gk_bias, lnorm, **sample)
        yc_p, rwkv_p = _rwkv(p3, None, None, rwkv_p, l, depth, mu, wwa, gup, vec, **prompt)
        yc_s, rwkv_s = _rwkv(p3s, state_rwkv_shift, state_rwkv, rwkv_s, l, depth, mu, wwa, gup, vec, **sample)

        h = _merge([y.reshape(n_p, BRANCH_W) for y in (ya_p, yb_p, yc_p)],
                   [y.reshape(n_s, BRANCH_W) for y in (ya_s, yb_s, yc_s)], pg, wbr, wout, h, l)
        x = _ffn(h, norm_ff2[l], *ff2, l)

    y_p = _norm_cast(x, norm_final, F32, 0, n_p)
    y_s = _norm_cast(x, norm_final, F32, n_p, n_s)
    return (y_p.reshape(bp, tp, d), y_s.reshape(bs, ts, d),
            gdn_p[0], gdn_s[0], gdn_p[1], gdn_s[1], gla_p, gla_s,
            rwkv_p[0], rwkv_s[0], rwkv_p[1], rwkv_s[1])
```
